```python
import jax, jax.numpy as jnp
from jax import lax
import numpy as np

D_MODEL = 4096
BATCH = 1
SEQ = 16384
DEPTH = 1

HEAD_DIM = 128
N_HGRN_HEADS = 16
HGRN_WIDTH = N_HGRN_HEADS * HEAD_DIM
N_FOX_HEADS = 16
FOX_WIDTH = N_FOX_HEADS * HEAD_DIM
D_FF = -(-(8 * D_MODEL) // (3 * 256)) * 256
CHUNK = 64
Q_BLOCK = 128
EPS = 1e-6
N_MOD = 6

IN_SIZES = (HGRN_WIDTH, HGRN_WIDTH, HGRN_WIDTH, HGRN_WIDTH,
            FOX_WIDTH, FOX_WIDTH, FOX_WIDTH, N_FOX_HEADS,
            D_MODEL, D_MODEL)
IN_COLS = sum(IN_SIZES)
IN_SPLITS = tuple(int(v) for v in np.cumsum(IN_SIZES)[:-1])

kernel_name = "hybrid_hgrn2_fox_adaln_block"


def _rms(x):
    xf = x.astype(jnp.float32)
    return (xf * lax.rsqrt(jnp.mean(xf * xf, axis=-1, keepdims=True) + EPS)).astype(x.dtype)


def _modulate(x, gain, shift, scale):
    return _rms(x) * gain * (1 + scale[:, None, :]) + shift[:, None, :]


def _hgrn2(q, f_logit, v, lb):
    f32 = jnp.float32
    B, S, H, DK = q.shape
    DV = v.shape[-1]
    f = lb + (1.0 - lb) * jax.nn.sigmoid(f_logit.astype(f32))
    logf = jnp.log(f)
    k = 1.0 - f
    nc = S // CHUNK

    def to_chunks(t):
        return t.reshape(B, nc, CHUNK, H, t.shape[-1]).transpose(1, 0, 3, 2, 4)

    xs = (to_chunks(q.astype(f32)), to_chunks(k), to_chunks(v.astype(f32)), to_chunks(logf))
    causal = jnp.tril(jnp.ones((CHUNK, CHUNK), dtype=bool))[:, :, None]

    def step(state, inp):
        qb, kb, vb, lfb = inp
        b = jnp.cumsum(lfb, axis=2)
        diff = b[:, :, :, None, :] - b[:, :, None, :, :]
        decay = jnp.exp(jnp.where(causal, diff, -jnp.inf))
        scores = jnp.einsum('bhtd,bhsd,bhtsd->bhts', qb, kb, decay)
        o_intra = jnp.einsum('bhts,bhsv->bhtv', scores, vb)
        o_inter = jnp.einsum('bhtd,bhdv->bhtv', qb * jnp.exp(b), state)
        b_last = b[:, :, -1:, :]
        new_state = jnp.exp(b_last[:, :, 0, :])[..., None] * state + jnp.einsum(
            'bhsd,bhsv->bhdv', kb * jnp.exp(b_last - b), vb)
        return new_state, o_intra + o_inter

    state0 = jnp.zeros((B, H, DK, DV), f32)
    _, o = lax.scan(step, state0, xs)
    return o.transpose(1, 0, 3, 2, 4).reshape(B, S, H, DV)


def _fox(q, k, v, f_logit):
    f32 = jnp.float32
    B, S, H, D = q.shape
    nb = S // Q_BLOCK
    scale = 1.0 / float(np.sqrt(D))
    cum = jnp.cumsum(jax.nn.log_sigmoid(f_logit.astype(f32)), axis=1).transpose(0, 2, 1)
    qh = q.transpose(0, 2, 1, 3)
    kh = k.transpose(0, 2, 1, 3)
    vh = v.transpose(0, 2, 1, 3)
    q_blocks = qh.reshape(B, H, nb, Q_BLOCK, D).transpose(2, 0, 1, 3, 4)
    c_blocks = cum.reshape(B, H, nb, Q_BLOCK).transpose(2, 0, 1, 3)
    key_pos = jnp.arange(S)

    def block(args):
        i, q_blk, c_blk = args
        s = jnp.einsum('bhtd,bhsd->bhts', q_blk, kh).astype(f32) * scale
        s = s + c_blk[..., None] - cum[:, :, None, :]
        q_pos = i * Q_BLOCK + jnp.arange(Q_BLOCK)
        s = jnp.where(key_pos[None, :] <= q_pos[:, None], s, -jnp.inf)
        p = jax.nn.softmax(s, axis=-1)
        return jnp.einsum('bhts,bhsd->bhtd', p.astype(vh.dtype), vh)

    o = lax.map(block, (jnp.arange(nb), q_blocks, c_blocks))
    return o.transpose(1, 0, 3, 2, 4).reshape(B, S, H * D)


def setup_inputs(seed: int = 0) -> dict:
    key = jax.random.key(seed)
    ks = jax.random.split(key, 17)
    f32 = jnp.float32
    nrm = lambda k, shape, s: jax.random.normal(k, shape, f32) * s
    return {
        "x": nrm(ks[0], (BATCH, SEQ, D_MODEL), 1.0),
        "c": nrm(ks[1], (BATCH, D_MODEL), 1.0),
        "w_ada": nrm(ks[2], (DEPTH, D_MODEL, N_MOD * D_MODEL), 0.5 * D_MODEL ** -0.5),
        "b_ada": nrm(ks[3], (DEPTH, N_MOD * D_MODEL), 0.01),
        "norm_mix_g": 1.0 + nrm(ks[4], (DEPTH, D_MODEL), 0.02),
        "norm_ffn_g": 1.0 + nrm(ks[5], (DEPTH, D_MODEL), 0.02),
        "w_in": nrm(ks[6], (DEPTH, D_MODEL, IN_COLS), D_MODEL ** -0.5),
        "fox_f_bias": 3.0 + nrm(ks[7], (DEPTH, N_FOX_HEADS), 0.5),
        "hgrn_lb_table": nrm(ks[8], (DEPTH + 1, HGRN_WIDTH), 0.5),
        "hgrn_onorm_g": 1.0 + nrm(ks[9], (DEPTH, N_HGRN_HEADS, HEAD_DIM), 0.02),
        "fox_q_norm_g": 1.0 + nrm(ks[10], (DEPTH, N_FOX_HEADS, HEAD_DIM), 0.02),
        "fox_k_norm_g": 1.0 + nrm(ks[11], (DEPTH, N_FOX_HEADS, HEAD_DIM), 0.02),
        "w_branch_a": nrm(ks[12], (DEPTH, HGRN_WIDTH, D_MODEL), HGRN_WIDTH ** -0.5),
        "w_branch_b": nrm(ks[13], (DEPTH, FOX_WIDTH, D_MODEL), FOX_WIDTH ** -0.5),
        "w_out": nrm(ks[14], (DEPTH, D_MODEL, D_MODEL), D_MODEL ** -0.5),
        "w_ffn_in": nrm(ks[15], (DEPTH, D_MODEL, 2 * D_FF), D_MODEL ** -0.5),
        "w_ffn_out": nrm(ks[16], (DEPTH, D_FF, D_MODEL), D_FF ** -0.5),
    }


def reference(x, c, w_ada, b_ada, norm_mix_g, norm_ffn_g, w_in, fox_f_bias, hgrn_lb_table,
              hgrn_onorm_g, fox_q_norm_g, fox_k_norm_g, w_branch_a, w_branch_b, w_out,
              w_ffn_in, w_ffn_out):
    B, S, _ = x.shape
    lb_all = jnp.cumsum(jax.nn.softmax(hgrn_lb_table.astype(jnp.float32), axis=0), axis=0)
    for l in range(DEPTH):
        mod = jax.nn.silu(c) @ w_ada[l] + b_ada[l]
        sh_m, sc_m, gt_m, sh_f, sc_f, gt_f = jnp.split(mod, N_MOD, axis=-1)

        h = _modulate(x, norm_mix_g[l], sh_m, sc_m)
        proj = h @ w_in[l]
        (hq, hf, hi, hg, fq, fk, fv, ff, ga, gb) = jnp.split(proj, IN_SPLITS, axis=-1)
        heads = lambda t, n: t.reshape(B, S, n, HEAD_DIM)

        lb = lb_all[l].reshape(N_HGRN_HEADS, HEAD_DIM)
        o_a = _hgrn2(heads(hq, N_HGRN_HEADS), heads(hf, N_HGRN_HEADS), heads(hi, N_HGRN_HEADS), lb)
        o_a = (_rms(o_a) * hgrn_onorm_g[l]).astype(x.dtype)
        o_a = o_a.reshape(B, S, HGRN_WIDTH) * jax.nn.silu(hg)

        q = _rms(heads(fq, N_FOX_HEADS)) * fox_q_norm_g[l]
        k = _rms(heads(fk, N_FOX_HEADS)) * fox_k_norm_g[l]
        o_b = _fox(q, k, heads(fv, N_FOX_HEADS), ff + fox_f_bias[l])

        y = jax.nn.sigmoid(ga) * (o_a @ w_branch_a[l]) + jax.nn.sigmoid(gb) * (o_b @ w_branch_b[l])
        x = x + gt_m[:, None, :] * (y @ w_out[l])

        h = _modulate(x, norm_ffn_g[l], sh_f, sc_f)
        gate, up = jnp.split(h @ w_ffn_in[l], 2, axis=-1)
        x = x + gt_f[:, None, :] * ((jax.nn.silu(gate) * up) @ w_ffn_out[l])
    return x
```

```python
import functools

import jax
import jax.numpy as jnp
from jax import lax
from jax.experimental import pallas as pl
from jax.experimental.pallas import tpu as pltpu

F32 = jnp.float32
BF16 = jnp.bfloat16

EPS = 1e-6
N_MOD = 6
LANES = 128
HEAD_DIM = 128
HGRN_SUB = 16
V7X_VMEM_LIMIT_BYTES = 56 * 1024 * 1024


def _cparams(semantics, vmem_bytes=V7X_VMEM_LIMIT_BYTES):
    return pltpu.CompilerParams(dimension_semantics=semantics, vmem_limit_bytes=vmem_bytes)


def _tile(dim, pref, align=LANES):
    if dim <= pref:
        return dim
    t = (pref // align) * align
    while t >= align:
        if dim % t == 0:
            return t
        t -= align
    raise ValueError(f"no {align}-aligned tile for {dim}")


def _sigmoid(x):
    return 1.0 / (1.0 + jnp.exp(-x))


def _ada_kernel(c_ref, w_ref, b_ref, o_ref):
    c = c_ref[...]
    s = c * _sigmoid(c)
    o_ref[...] = jnp.sum(w_ref[...] * s, axis=0, keepdims=True) + b_ref[...]


def _ada_mod(c, w_ada, b_ada):
    d, n = w_ada.shape
    tn = _tile(n, 512)
    return pl.pallas_call(
        _ada_kernel,
        out_shape=jax.ShapeDtypeStruct((1, n), F32),
        grid=(n // tn,),
        in_specs=[pl.BlockSpec((d, 1), lambda j: (0, 0)),
                  pl.BlockSpec((d, tn), lambda j: (0, j)),
                  pl.BlockSpec((1, tn), lambda j: (0, j))],
        out_specs=pl.BlockSpec((1, tn), lambda j: (0, j)),
        compiler_params=_cparams(("parallel",)),
        name="ada_mod",
    )(c.reshape(d, 1), w_ada, b_ada.reshape(1, n))


def _normmod_kernel(x_ref, g_ref, sh_ref, sc_ref, o_ref):
    x = x_ref[...]
    ms = jnp.mean(x * x, axis=-1, keepdims=True)
    xn = x * lax.rsqrt(ms + EPS)
    o_ref[...] = (xn * g_ref[...] * (1.0 + sc_ref[...]) + sh_ref[...]).astype(o_ref.dtype)


def _normmod(x, gain, mod, shift_idx, scale_idx):
    s, d = x.shape
    tm = _tile(s, 256, 8)
    return pl.pallas_call(
        _normmod_kernel,
        out_shape=jax.ShapeDtypeStruct((s, d), BF16),
        grid=(s // tm,),
        in_specs=[pl.BlockSpec((tm, d), lambda i: (i, 0)),
                  pl.BlockSpec((1, d), lambda i: (0, 0)),
                  pl.BlockSpec((1, d), lambda i: (0, shift_idx)),
                  pl.BlockSpec((1, d), lambda i: (0, scale_idx))],
        out_specs=pl.BlockSpec((tm, d), lambda i: (i, 0)),
        compiler_params=_cparams(("parallel",)),
        name="normmod",
    )(x, gain.reshape(1, d), mod, mod)


def _mm_kernel(x_ref, w_ref, o_ref):
    o_ref[...] = jnp.dot(x_ref[...], w_ref[...], preferred_element_type=F32).astype(o_ref.dtype)


def _matmul(x, w, out_dtype, name):
    m, k = x.shape
    _, n = w.shape
    tm = _tile(m, 1024, 8)
    tn = _tile(n, 1024)
    return pl.pallas_call(
        _mm_kernel,
        out_shape=jax.ShapeDtypeStruct((m, n), out_dtype),
        grid=(m // tm, n // tn),
        in_specs=[pl.BlockSpec((tm, k), lambda i, j: (i, 0)),
                  pl.BlockSpec((k, tn), lambda i, j: (0, j))],
        out_specs=pl.BlockSpec((tm, tn), lambda i, j: (i, j)),
        compiler_params=_cparams(("parallel", "arbitrary")),
        name=name,
    )(x, w)


def _hgrn_kernel(q_ref, i_ref, g_ref, f_ref, lbt_ref, og_ref, o_ref, st_ref, *, hb, tt):
    @pl.when(pl.program_id(1) == 0)
    def _():
        st_ref[...] = jnp.zeros_like(st_ref)

    tbl = lbt_ref[...]
    e = jnp.exp(tbl - jnp.max(tbl, axis=0, keepdims=True))
    lb = e[0:1, :] / jnp.sum(e, axis=0, keepdims=True)
    row = lax.broadcasted_iota(jnp.int32, (HGRN_SUB, LANES), 0)
    rowc = lax.broadcasted_iota(jnp.int32, (HGRN_SUB, 1), 0)

    def body(n, carry):
        r0 = pl.multiple_of(n * HGRN_SUB, HGRN_SUB)
        rows = pl.ds(r0, HGRN_SUB)
        for h in range(hb):
            cs = slice(h * LANES, (h + 1) * LANES)
            q = q_ref[rows, cs].astype(F32)
            v = i_ref[rows, cs]
            vf = v.astype(F32)
            lbh = lb[:, cs]
            f = lbh + (1.0 - lbh) * _sigmoid(f_ref[rows, cs])
            k = 1.0 - f
            b = jnp.log(f)
            for sh in (1, 2, 4, 8):
                b = b + jnp.where(row >= sh, pltpu.roll(b, sh, axis=0), 0.0)
            b_last = b[HGRN_SUB - 1:HGRN_SUB, :]
            st = st_ref[h]
            qt = (q * jnp.exp(b)).astype(BF16)
            o = lax.dot_general(qt, st.astype(BF16), (((1,), (1,)), ((), ())),
                                preferred_element_type=F32)
            for s in range(HGRN_SUB):
                dec = jnp.exp(jnp.minimum(b - b[s:s + 1, :], 0.0))
                col = jnp.sum(q * k[s:s + 1, :] * dec, axis=-1, keepdims=True)
                col = jnp.where(rowc >= s, col, 0.0)
                o = o + col * vf[s:s + 1, :]
            kt = (k * jnp.exp(b_last - b)).astype(BF16)
            upd = lax.dot_general(v, kt, (((0,), (0,)), ((), ())),
                                  preferred_element_type=F32)
            st_ref[h] = st * jnp.exp(b_last) + upd
            ms = jnp.mean(o * o, axis=-1, keepdims=True)
            on = o * lax.rsqrt(ms + EPS) * og_ref[:, cs]
            gate = g_ref[rows, cs].astype(F32)
            o_ref[rows, cs] = (on * (gate * _sigmoid(gate))).astype(o_ref.dtype)
        return carry

    lax.fori_loop(0, tt // HGRN_SUB, body, 0)


def _hgrn(proj, hf, lb_table, onorm_g, width, q_col, i_col, g_col):
    s = proj.shape[0]
    hb = 4
    bw = hb * HEAD_DIM
    tt = _tile(s, 512, HGRN_SUB)
    nl = lb_table.shape[0]
    kern = functools.partial(_hgrn_kernel, hb=hb, tt=tt)
    return pl.pallas_call(
        kern,
        out_shape=jax.ShapeDtypeStruct((s, width), BF16),
        grid=(width // bw, s // tt),
        in_specs=[pl.BlockSpec((tt, bw), lambda h, t: (t, q_col // bw + h)),
                  pl.BlockSpec((tt, bw), lambda h, t: (t, i_col // bw + h)),
                  pl.BlockSpec((tt, bw), lambda h, t: (t, g_col // bw + h)),
                  pl.BlockSpec((tt, bw), lambda h, t: (t, h)),
                  pl.BlockSpec((nl, bw), lambda h, t: (0, h)),
                  pl.BlockSpec((1, bw), lambda h, t: (0, h))],
        out_specs=pl.BlockSpec((tt, bw), lambda h, t: (t, h)),
        scratch_shapes=[pltpu.VMEM((hb, HEAD_DIM, HEAD_DIM), F32)],
        compiler_params=_cparams(("parallel", "arbitrary")),
        name="hgrn2",
    )(proj, proj, proj, hf, lb_table, onorm_g.reshape(1, width))


def _qknorm_kernel(q_ref, k_ref, gq_ref, gk_ref, qo_ref, kto_ref, *, nh, scale):
    for h in range(nh):
        cs = slice(h * LANES, (h + 1) * LANES)
        q = q_ref[:, cs].astype(F32)
        qn = q * lax.rsqrt(jnp.mean(q * q, axis=-1, keepdims=True) + EPS) * gq_ref[:, cs]
        qo_ref[:, cs] = (qn * scale).astype(qo_ref.dtype)
        k = k_ref[:, cs].astype(F32)
        kn = k * lax.rsqrt(jnp.mean(k * k, axis=-1, keepdims=True) + EPS) * gk_ref[:, cs]
        kto_ref[0, cs, :] = kn.T.astype(kto_ref.dtype)


def _qknorm(proj, gq, gk, width, q_col, k_col, tk):
    s = proj.shape[0]
    nh = width // HEAD_DIM
    kern = functools.partial(_qknorm_kernel, nh=nh, scale=float(HEAD_DIM) ** -0.5)
    return pl.pallas_call(
        kern,
        out_shape=(jax.ShapeDtypeStruct((s, width), BF16),
                   jax.ShapeDtypeStruct((s // tk, width, tk), BF16)),
        grid=(s // tk,),
        in_specs=[pl.BlockSpec((tk, width), lambda i: (i, q_col // width)),
                  pl.BlockSpec((tk, width), lambda i: (i, k_col // width)),
                  pl.BlockSpec((1, width), lambda i: (0, 0)),
                  pl.BlockSpec((1, width), lambda i: (0, 0))],
        out_specs=(pl.BlockSpec((tk, width), lambda i: (i, 0)),
                   pl.BlockSpec((1, width, tk), lambda i: (i, 0, 0))),
        compiler_params=_cparams(("parallel",)),
        name="fox_qknorm",
    )(proj, proj, gq.reshape(1, width), gk.reshape(1, width))


def _cum_kernel(x_ref, b_ref, o_ref):
    x = x_ref[0] + b_ref[0]
    ls = jnp.minimum(x, 0.0) - jnp.log(1.0 + jnp.exp(-jnp.abs(x)))
    r = x.shape[0]
    ci = lax.broadcasted_iota(jnp.int32, (LANES, LANES), 0)
    cj = lax.broadcasted_iota(jnp.int32, (LANES, LANES), 1)
    upper = (ci <= cj).astype(F32)
    w = jnp.dot(ls, upper, precision=lax.Precision.HIGHEST, preferred_element_type=F32)
    ri = lax.broadcasted_iota(jnp.int32, (r, r), 0)
    rj = lax.broadcasted_iota(jnp.int32, (r, r), 1)
    strict = (rj < ri).astype(F32)
    z = jnp.dot(strict, w, precision=lax.Precision.HIGHEST, preferred_element_type=F32)
    o_ref[0] = w + z[:, LANES - 1:LANES]


def _fox_cum(ff_t, bias):
    nh, s = ff_t.shape
    r = s // LANES
    out = pl.pallas_call(
        _cum_kernel,
        out_shape=jax.ShapeDtypeStruct((nh, r, LANES), F32),
        grid=(nh,),
        in_specs=[pl.BlockSpec((1, r, LANES), lambda h: (h, 0, 0)),
                  pl.BlockSpec((1, 1, LANES), lambda h: (h, 0, 0))],
        out_specs=pl.BlockSpec((1, r, LANES), lambda h: (h, 0, 0)),
        compiler_params=_cparams(("parallel",)),
        name="fox_cum",
    )(ff_t.reshape(nh, r, LANES), jnp.broadcast_to(bias.reshape(nh, 1, 1), (nh, 1, LANES)))
    return out.reshape(nh, s)


def _fox_kernel(q_ref, kt_ref, v_ref, crow_ref, ccol_ref, o_ref, *, tq, tk):
    qi = pl.program_id(1)
    q = q_ref[...]
    ct = ccol_ref[0]

    def scores(j):
        s = jnp.dot(q, kt_ref[j], preferred_element_type=F32)
        return s + (ct - crow_ref[0, j])

    def update(j, s, carry):
        m, l, acc = carry
        m_new = jnp.maximum(m, jnp.max(s, axis=-1, keepdims=True))
        p = jnp.exp(s - m_new)
        alpha = jnp.exp(m - m_new)
        l = alpha * l + jnp.sum(p, axis=-1, keepdims=True)
        r0 = pl.multiple_of(j * tk, tk)
        pv = jnp.dot(p.astype(BF16), v_ref[pl.ds(r0, tk), :], preferred_element_type=F32)
        return m_new, l, alpha * acc + pv

    def body(j, carry):
        return update(j, scores(j), carry)

    init = (jnp.full((tq, 1), -jnp.inf, F32), jnp.zeros((tq, 1), F32), jnp.zeros((tq, HEAD_DIM), F32))
    carry = lax.fori_loop(0, qi, body, init)
    rr = lax.broadcasted_iota(jnp.int32, (tq, tk), 0)
    cc = lax.broadcasted_iota(jnp.int32, (tq, tk), 1)
    s = jnp.where(cc <= rr, scores(qi), -jnp.inf)
    m, l, acc = update(qi, s, carry)
    o_ref[...] = (acc / l).astype(o_ref.dtype)


def _fox(q, kt, proj, cum, v_col, tq):
    s, width = q.shape
    nh = width // HEAD_DIM
    nb = s // tq
    kern = functools.partial(_fox_kernel, tq=tq, tk=tq)
    return pl.pallas_call(
        kern,
        out_shape=jax.ShapeDtypeStruct((s, width), BF16),
        grid=(nh, nb),
        in_specs=[pl.BlockSpec((tq, HEAD_DIM), lambda h, i: (i, h)),
                  pl.BlockSpec((nb, HEAD_DIM, tq), lambda h, i: (0, h, 0)),
                  pl.BlockSpec((s, HEAD_DIM), lambda h, i: (0, v_col // HEAD_DIM + h)),
                  pl.BlockSpec((1, nb, 1, tq), lambda h, i: (h, 0, 0, 0)),
                  pl.BlockSpec((1, tq, 1), lambda h, i: (h, i, 0))],
        out_specs=pl.BlockSpec((tq, HEAD_DIM), lambda h, i: (i, h)),
        compiler_params=_cparams(("parallel", "arbitrary")),
        name="fox_attn",
    )(q, kt, proj, cum.reshape(nh, nb, 1, tq), cum.reshape(nh, s, 1))


def _merge_kernel(oa_ref, ob_ref, wa_ref, wb_ref, ga_ref, gb_ref, y_ref):
    ya = jnp.dot(oa_ref[...], wa_ref[...], preferred_element_type=F32)
    yb = jnp.dot(ob_ref[...], wb_ref[...], preferred_element_type=F32)
    y = _sigmoid(ga_ref[...].astype(F32)) * ya + _sigmoid(gb_ref[...].astype(F32)) * yb
    y_ref[...] = y.astype(y_ref.dtype)


def _merge(oa, ob, wa, wb, proj, ga_col, gb_col):
    m, ka = oa.shape
    kb = ob.shape[1]
    n = wa.shape[1]
    tm = _tile(m, 1024, 8)
    tn = _tile(n, 512)
    return pl.pallas_call(
        _merge_kernel,
        out_shape=jax.ShapeDtypeStruct((m, n), BF16),
        grid=(m // tm, n // tn),
        in_specs=[pl.BlockSpec((tm, ka), lambda i, j: (i, 0)),
                  pl.BlockSpec((tm, kb), lambda i, j: (i, 0)),
                  pl.BlockSpec((ka, tn), lambda i, j: (0, j)),
                  pl.BlockSpec((kb, tn), lambda i, j: (0, j)),
                  pl.BlockSpec((tm, tn), lambda i, j: (i, ga_col // tn + j)),
                  pl.BlockSpec((tm, tn), lambda i, j: (i, gb_col // tn + j))],
        out_specs=pl.BlockSpec((tm, tn), lambda i, j: (i, j)),
        compiler_params=_cparams(("parallel", "arbitrary")),
        name="branch_merge",
    )(oa, ob, wa, wb, proj, proj)


def _resproj_kernel(a_ref, w_ref, x_ref, gt_ref, o_ref):
    y = jnp.dot(a_ref[...], w_ref[...], preferred_element_type=F32)
    o_ref[...] = x_ref[...] + gt_ref[...] * y


def _resproj(a, w, x, mod, gate_idx, tm_pref, tn_pref, name):
    m, k = a.shape
    n = w.shape[1]
    tm = _tile(m, tm_pref, 8)
    tn = _tile(n, tn_pref)
    return pl.pallas_call(
        _resproj_kernel,
        out_shape=jax.ShapeDtypeStruct((m, n), F32),
        grid=(m // tm, n // tn),
        in_specs=[pl.BlockSpec((tm, k), lambda i, j: (i, 0)),
                  pl.BlockSpec((k, tn), lambda i, j: (0, j)),
                  pl.BlockSpec((tm, tn), lambda i, j: (i, j)),
                  pl.BlockSpec((1, tn), lambda i, j: (0, gate_idx * (n // tn) + j))],
        out_specs=pl.BlockSpec((tm, tn), lambda i, j: (i, j)),
        compiler_params=_cparams(("parallel", "arbitrary")),
        name=name,
    )(a, w, x, mod)


def _swiglu_kernel(h_ref, wg_ref, wu_ref, o_ref):
    h = h_ref[...]
    g = jnp.dot(h, wg_ref[...], preferred_element_type=F32)
    u = jnp.dot(h, wu_ref[...], preferred_element_type=F32)
    o_ref[...] = (g * _sigmoid(g) * u).astype(o_ref.dtype)


def _swiglu(h, w):
    m, k = h.shape
    dff = w.shape[1] // 2
    tm = _tile(m, 1024, 8)
    tn = _tile(dff, 512)
    nj = dff // tn
    return pl.pallas_call(
        _swiglu_kernel,
        out_shape=jax.ShapeDtypeStruct((m, dff), BF16),
        grid=(m // tm, nj),
        in_specs=[pl.BlockSpec((tm, k), lambda i, j: (i, 0)),
                  pl.BlockSpec((k, tn), lambda i, j: (0, j)),
                  pl.BlockSpec((k, tn), lambda i, j: (0, nj + j))],
        out_specs=pl.BlockSpec((tm, tn), lambda i, j: (i, j)),
        compiler_params=_cparams(("parallel", "arbitrary")),
        name="ffn_swiglu",
    )(h, w, w)


def kernel(x, c, w_ada, b_ada, norm_mix_g, norm_ffn_g, w_in, fox_f_bias, hgrn_lb_table, hgrn_onorm_g,
           fox_q_norm_g, fox_k_norm_g, w_branch_a, w_branch_b, w_out, w_ffn_in, w_ffn_out):
    batch, seq, d = x.shape
    assert batch == 1 and w_ada.shape[0] == 1, "single sequence, single layer"
    hw = hgrn_onorm_g.shape[1] * hgrn_onorm_g.shape[2]
    fw = fox_q_norm_g.shape[1] * fox_q_norm_g.shape[2]
    nfh = fox_q_norm_g.shape[1]
    assert hgrn_onorm_g.shape[2] == HEAD_DIM and fox_q_norm_g.shape[2] == HEAD_DIM
    x2 = x.reshape(seq, d)

    sizes = (hw, hw, hw, hw, fw, fw, fw, nfh, d, d)
    offs = [0]
    for sz in sizes:
        offs.append(offs[-1] + sz)
    w = w_in[0]
    col = lambda i: w[:, offs[i]:offs[i + 1]]
    w_main = jnp.concatenate([col(0), col(2), col(3), col(4), col(5), col(6), col(8), col(9)], axis=1).astype(BF16)
    q_col, i_col, g_col = 0, hw, 2 * hw
    fq_col, fk_col, fv_col = 3 * hw, 3 * hw + fw, 3 * hw + 2 * fw
    ga_col, gb_col = 3 * hw + 3 * fw, 3 * hw + 3 * fw + d
    w_hf = col(1).astype(BF16)
    w_ff = jnp.pad(col(7), ((0, 0), (0, LANES - nfh))).astype(BF16)

    mod = _ada_mod(c, w_ada[0], b_ada[0])

    h = _normmod(x2, norm_mix_g[0], mod, 0, 1)
    proj = _matmul(h, w_main, BF16, "in_proj")
    hf = _matmul(h, w_hf, F32, "in_proj_f")
    ff = _matmul(h, w_ff, F32, "in_proj_ff")

    o_a = _hgrn(proj, hf, hgrn_lb_table, hgrn_onorm_g[0], hw, q_col, i_col, g_col)

    tq = _tile(seq, 512)
    qn, kt = _qknorm(proj, fox_q_norm_g[0], fox_k_norm_g[0], fw, fq_col, fk_col, tq)
    cum = _fox_cum(ff[:, :nfh].T, fox_f_bias[0])
    o_b = _fox(qn, kt, proj, cum, fv_col, tq)

    y = _merge(o_a, o_b, w_branch_a[0].astype(BF16), w_branch_b[0].astype(BF16), proj, ga_col, gb_col)
    x1 = _resproj(y, w_out[0].astype(BF16), x2, mod, 2, 1024, 512, "out_proj")

    h2 = _normmod(x1, norm_ffn_g[0], mod, 3, 4)
    act = _swiglu(h2, w_ffn_in[0].astype(BF16))
    x3 = _resproj(act, w_ffn_out[0].astype(BF16), x1, mod, 5, 512, 512, "ffn_out")
    return x3.reshape(batch, seq, d)
```

```python
import functools
import math

import jax
import jax.numpy as jnp
from jax import lax
from jax.experimental import pallas as pl
from jax.experimental.pallas import tpu as pltpu

F32 = jnp.float32
BF16 = jnp.bfloat16

EPS = 1e-6
LOG2E = 1.4426950408889634
LANES = 128
SUBLANES = 8
HEAD_DIM = 128
HGRN_SUB = 16
FOX_TK = 512
FOX_KDIM = 2 * HEAD_DIM
V7X_VMEM_LIMIT_BYTES = 56 * 1024 * 1024


def _cparams(semantics, vmem_bytes=V7X_VMEM_LIMIT_BYTES):
    return pltpu.CompilerParams(dimension_semantics=semantics, vmem_limit_bytes=vmem_bytes)


def _tile(dim, pref, align=LANES):
    if dim <= pref:
        return dim
    t = (pref // align) * align
    while t >= align:
        if dim % t == 0:
            return t
        t -= align
    raise ValueError(f"no {align}-aligned tile for {dim}")


def _sigmoid(x):
    return 1.0 / (1.0 + jnp.exp(-x))


def _ada_kernel(c_ref, w_ref, b_ref, o_ref):
    c = c_ref[...]
    s = c * _sigmoid(c)
    o_ref[...] = jnp.sum(w_ref[...] * s, axis=0, keepdims=True) + b_ref[...]


def _ada_mod(c, w_ada, b_ada):
    d, n = w_ada.shape
    tn = _tile(n, 512)
    return pl.pallas_call(
        _ada_kernel,
        out_shape=jax.ShapeDtypeStruct((1, n), F32),
        grid=(n // tn,),
        in_specs=[pl.BlockSpec((d, 1), lambda j: (0, 0)),
                  pl.BlockSpec((d, tn), lambda j: (0, j)),
                  pl.BlockSpec((1, tn), lambda j: (0, j))],
        out_specs=pl.BlockSpec((1, tn), lambda j: (0, j)),
        compiler_params=_cparams(("parallel",)),
        name="ada_mod",
    )(c.reshape(d, 1), w_ada, b_ada.reshape(1, n))


def _normmod_kernel(x_ref, g_ref, sh_ref, sc_ref, o_ref):
    x = x_ref[...]
    ms = jnp.mean(x * x, axis=-1, keepdims=True)
    xn = x * lax.rsqrt(ms + EPS)
    o_ref[...] = (xn * g_ref[...] * (1.0 + sc_ref[...]) + sh_ref[...]).astype(o_ref.dtype)


def _normmod(x, gain, mod, shift_idx, scale_idx):
    s, d = x.shape
    tm = _tile(s, 256, SUBLANES)
    return pl.pallas_call(
        _normmod_kernel,
        out_shape=jax.ShapeDtypeStruct((s, d), BF16),
        grid=(s // tm,),
        in_specs=[pl.BlockSpec((tm, d), lambda i: (i, 0)),
                  pl.BlockSpec((1, d), lambda i: (0, 0)),
                  pl.BlockSpec((1, d), lambda i: (0, shift_idx)),
                  pl.BlockSpec((1, d), lambda i: (0, scale_idx))],
        out_specs=pl.BlockSpec((tm, d), lambda i: (i, 0)),
        compiler_params=_cparams(("parallel",)),
        name="normmod",
    )(x, gain.reshape(1, d), mod, mod)


def _mm_kernel(x_ref, w_ref, o_ref):
    o_ref[...] = jnp.dot(x_ref[...], w_ref[...], preferred_element_type=F32).astype(o_ref.dtype)


def _matmul(x, w, col0, ncols, out_dtype, name):
    m, k = x.shape
    tm = _tile(m, 1024, SUBLANES)
    tn = _tile(math.gcd(ncols, col0) if col0 else ncols, 1024)
    return pl.pallas_call(
        _mm_kernel,
        out_shape=jax.ShapeDtypeStruct((m, ncols), out_dtype),
        grid=(m // tm, ncols // tn),
        in_specs=[pl.BlockSpec((tm, k), lambda i, j: (i, 0)),
                  pl.BlockSpec((k, tn), lambda i, j: (0, col0 // tn + j))],
        out_specs=pl.BlockSpec((tm, tn), lambda i, j: (i, j)),
        compiler_params=_cparams(("parallel", "arbitrary")),
        name=name,
    )(x, w)


def _hgrn_kernel(q_ref, i_ref, g_ref, f_ref, lbt_ref, og_ref, o_ref, st_ref, *, hb, tt):
    @pl.when(pl.program_id(1) == 0)
    def _():
        st_ref[...] = jnp.zeros_like(st_ref)

    tbl = lbt_ref[...]
    e = jnp.exp(tbl - jnp.max(tbl, axis=0, keepdims=True))
    lb = e[0:1, :] / jnp.sum(e, axis=0, keepdims=True)
    row = lax.broadcasted_iota(jnp.int32, (HGRN_SUB, LANES), 0)
    rowc = lax.broadcasted_iota(jnp.int32, (SUBLANES, 1), 0)
    half = SUBLANES

    def body(n, carry):
        r0 = pl.multiple_of(n * HGRN_SUB, HGRN_SUB)
        rows = pl.ds(r0, HGRN_SUB)
        for h in range(hb):
            cs = slice(h * LANES, (h + 1) * LANES)
            q = q_ref[rows, cs].astype(F32)
            v = i_ref[rows, cs]
            vf = v.astype(F32)
            lbh = lb[:, cs]
            f = lbh + (1.0 - lbh) * _sigmoid(f_ref[rows, cs])
            k = 1.0 - f
            b = jnp.log2(f)
            for sh in (1, 2, 4, 8):
                b = b + jnp.where(row >= sh, pltpu.roll(b, sh, axis=0), 0.0)
            b_last = b[HGRN_SUB - 1:HGRN_SUB, :]
            st = st_ref[h]
            qt = (q * jnp.exp2(b)).astype(BF16)
            o = lax.dot_general(qt, st.astype(BF16), (((1,), (1,)), ((), ())),
                                preferred_element_type=F32)
            o_lo, o_hi = o[:half], o[half:]
            q_lo, q_hi = q[:half], q[half:]
            b_lo, b_hi = b[:half], b[half:]
            for s in range(HGRN_SUB):
                ks, bs, vs = k[s:s + 1, :], b[s:s + 1, :], vf[s:s + 1, :]
                c_hi = jnp.sum(q_hi * ks * jnp.exp2(b_hi - bs), axis=-1, keepdims=True)
                if s < half:
                    c_lo = jnp.sum(q_lo * ks * jnp.exp2(b_lo - bs), axis=-1, keepdims=True)
                    o_lo = o_lo + jnp.where(rowc >= s, c_lo, 0.0) * vs
                else:
                    c_hi = jnp.where(rowc >= s - half, c_hi, 0.0)
                o_hi = o_hi + c_hi * vs
            kt = (k * jnp.exp2(b_last - b)).astype(BF16)
            upd = lax.dot_general(v, kt, (((0,), (0,)), ((), ())),
                                  preferred_element_type=F32)
            st_ref[h] = st * jnp.exp2(b_last) + upd
            o = jnp.concatenate([o_lo, o_hi], axis=0)
            ms = jnp.mean(o * o, axis=-1, keepdims=True)
            on = o * lax.rsqrt(ms + EPS) * og_ref[:, cs]
            gate = g_ref[rows, cs].astype(F32)
            o_ref[rows, cs] = (on * (gate * _sigmoid(gate))).astype(o_ref.dtype)
        return carry

    lax.fori_loop(0, tt // HGRN_SUB, body, 0)


def _hgrn(pq, pm, hf, lb_table, onorm_g, width, i_col, g_col):
    s = pq.shape[0]
    hb = 4
    bw = hb * HEAD_DIM
    tt = _tile(s, 512, HGRN_SUB)
    nl = lb_table.shape[0]
    kern = functools.partial(_hgrn_kernel, hb=hb, tt=tt)
    return pl.pallas_call(
        kern,
        out_shape=jax.ShapeDtypeStruct((s, width), BF16),
        grid=(width // bw, s // tt),
        in_specs=[pl.BlockSpec((tt, bw), lambda h, t: (t, h)),
                  pl.BlockSpec((tt, bw), lambda h, t: (t, i_col // bw + h)),
                  pl.BlockSpec((tt, bw), lambda h, t: (t, g_col // bw + h)),
                  pl.BlockSpec((tt, bw), lambda h, t: (t, h)),
                  pl.BlockSpec((nl, bw), lambda h, t: (0, h)),
                  pl.BlockSpec((1, bw), lambda h, t: (0, h))],
        out_specs=pl.BlockSpec((tt, bw), lambda h, t: (t, h)),
        scratch_shapes=[pltpu.VMEM((hb, HEAD_DIM, HEAD_DIM), F32)],
        compiler_params=_cparams(("parallel", "arbitrary")),
        name="hgrn2",
    )(pq, pm, pm, hf, lb_table, onorm_g.reshape(1, width))


def _split3(x):
    hi = x.astype(BF16).astype(F32)
    r = x - hi
    mid = r.astype(BF16).astype(F32)
    lo = (r - mid).astype(BF16).astype(F32)
    return hi, mid, lo


def _foxprep_kernel(q_ref, k_ref, ff_ref, fb_ref, gq_ref, gk_ref, qo_ref, kto_ref, r_ref, carry_ref, *, nh, scale):
    @pl.when(pl.program_id(0) == 0)
    def _():
        carry_ref[...] = jnp.zeros_like(carry_ref)

    tk = q_ref.shape[0]
    x = ff_ref[...] + fb_ref[...]
    ls = (jnp.minimum(x, 0.0) - jnp.log(1.0 + jnp.exp(-jnp.abs(x)))) * LOG2E
    ri = lax.broadcasted_iota(jnp.int32, (tk, tk), 0)
    ci = lax.broadcasted_iota(jnp.int32, (tk, tk), 1)
    lower = (ci <= ri).astype(F32)
    rel = jnp.dot(lower, ls, precision=lax.Precision.HIGHEST, preferred_element_type=F32)
    base = carry_ref[...]
    r_ref[0] = base
    carry_ref[...] = base + rel[tk - 1:tk, :]

    cq = _split3(rel)
    ck = _split3(rel.T)
    lane = lax.broadcasted_iota(jnp.int32, (tk, LANES), 1)
    sub = lax.broadcasted_iota(jnp.int32, (LANES, tk), 0)
    ones_q = jnp.where((lane >= 3) & (lane < 6), 1.0, 0.0)
    ones_k = jnp.where(sub < 3, 1.0, 0.0)
    for h in range(nh):
        cs = slice(h * LANES, (h + 1) * LANES)
        q = q_ref[:, cs].astype(F32)
        qn = q * lax.rsqrt(jnp.mean(q * q, axis=-1, keepdims=True) + EPS) * gq_ref[:, cs]
        k = k_ref[:, cs].astype(F32)
        kn = k * lax.rsqrt(jnp.mean(k * k, axis=-1, keepdims=True) + EPS) * gk_ref[:, cs]
        eq = ones_q
        ek = ones_k
        for t in range(3):
            eq = jnp.where(lane == t, cq[t][:, h:h + 1], eq)
            ek = jnp.where(sub == 3 + t, -ck[t][h:h + 1, :], ek)
        c0 = h * FOX_KDIM
        qo_ref[:, c0:c0 + LANES] = (qn * scale).astype(qo_ref.dtype)
        qo_ref[:, c0 + LANES:c0 + FOX_KDIM] = eq.astype(qo_ref.dtype)
        kto_ref[0, c0:c0 + LANES, :] = kn.T.astype(kto_ref.dtype)
        kto_ref[0, c0 + LANES:c0 + FOX_KDIM, :] = ek.astype(kto_ref.dtype)


def _foxprep(pm, ff, fbias, gq, gk, width, q_col, k_col, tk):
    s = pm.shape[0]
    nh = width // HEAD_DIM
    nb = s // tk
    kern = functools.partial(_foxprep_kernel, nh=nh, scale=LOG2E * float(HEAD_DIM) ** -0.5)
    qa, kta, r = pl.pallas_call(
        kern,
        out_shape=(jax.ShapeDtypeStruct((s, nh * FOX_KDIM), BF16),
                   jax.ShapeDtypeStruct((nb, nh * FOX_KDIM, tk), BF16),
                   jax.ShapeDtypeStruct((nb, 1, LANES), F32)),
        grid=(nb,),
        in_specs=[pl.BlockSpec((tk, width), lambda i: (i, q_col // width)),
                  pl.BlockSpec((tk, width), lambda i: (i, k_col // width)),
                  pl.BlockSpec((tk, LANES), lambda i: (i, 0)),
                  pl.BlockSpec((1, LANES), lambda i: (0, 0)),
                  pl.BlockSpec((1, width), lambda i: (0, 0)),
                  pl.BlockSpec((1, width), lambda i: (0, 0))],
        out_specs=(pl.BlockSpec((tk, nh * FOX_KDIM), lambda i: (i, 0)),
                   pl.BlockSpec((1, nh * FOX_KDIM, tk), lambda i: (i, 0, 0)),
                   pl.BlockSpec((1, 1, LANES), lambda i: (i, 0, 0))),
        scratch_shapes=[pltpu.VMEM((1, LANES), F32)],
        compiler_params=_cparams(("arbitrary",)),
        name="fox_prep",
    )(pm, pm, ff, fbias, gq.reshape(1, width), gk.reshape(1, width))
    return qa, kta, r.reshape(nb, LANES)


def _fox_kernel(r_ref, q_ref, kt_ref, v_ref, o_ref, v2_ref, s_ref, m_ref, acc_ref, *, tk):
    h = pl.program_id(0)
    qi = pl.program_id(1)

    @pl.when(qi == 0)
    def _():
        v2_ref[:, :LANES] = v_ref[...]
        lane = lax.broadcasted_iota(jnp.int32, v_ref.shape, 1)
        v2_ref[:, LANES:] = jnp.where(lane == 0, 1.0, 0.0).astype(v2_ref.dtype)

    m_ref[...] = jnp.full(m_ref.shape, -jnp.inf, F32)
    acc_ref[...] = jnp.zeros(acc_ref.shape, F32)
    rr = lax.broadcasted_iota(jnp.int32, (tk, tk), 0)
    cc = lax.broadcasted_iota(jnp.int32, (tk, tk), 1)

    def scores(c, j, slot):
        s_ref[slot, c] = jnp.dot(q_ref[c * tk:(c + 1) * tk, :], kt_ref[j], preferred_element_type=F32)

    def fold(c, j, slot, diag):
        s = s_ref[slot, c]
        if diag:
            s = jnp.where(cc <= rr, s, -jnp.inf)
        d = r_ref[2 * qi + c, h] - r_ref[j, h]
        m = m_ref[c]
        m_new = jnp.maximum(m, jnp.max(s, axis=-1, keepdims=True) + d)
        p = jnp.exp2(s - (m_new - d))
        r0 = pl.multiple_of(j * tk, tk)
        pv = jnp.dot(p.astype(BF16), v2_ref[pl.ds(r0, tk), :], preferred_element_type=F32)
        acc_ref[c] = jnp.exp2(m - m_new) * acc_ref[c] + pv
        m_ref[c] = m_new

    scores(0, 0, 0)
    scores(1, 0, 0)

    def body(jp, carry):
        j = 2 * jp
        for slot in range(2):
            scores(0, j + slot + 1, 1 - slot)
            scores(1, j + slot + 1, 1 - slot)
            fold(0, j + slot, slot, False)
            fold(1, j + slot, slot, False)
        return carry

    lax.fori_loop(0, qi, body, 0)
    scores(1, 2 * qi + 1, 1)
    fold(0, 2 * qi, 0, True)
    fold(1, 2 * qi, 0, False)
    fold(1, 2 * qi + 1, 1, True)
    for c in range(2):
        acc = acc_ref[c]
        o_ref[c * tk:(c + 1) * tk, :] = (acc[:, :LANES] / acc[:, LANES:LANES + 1]).astype(o_ref.dtype)


def _fox(qa, kta, r, pm, v_col, tk):
    s = qa.shape[0]
    nh = qa.shape[1] // FOX_KDIM
    nb = s // tk
    tq = 2 * tk
    kern = functools.partial(_fox_kernel, tk=tk)
    return pl.pallas_call(
        kern,
        out_shape=jax.ShapeDtypeStruct((s, nh * HEAD_DIM), BF16),
        grid=(nh, s // tq),
        in_specs=[pl.BlockSpec(memory_space=pltpu.SMEM),
                  pl.BlockSpec((tq, FOX_KDIM), lambda h, i: (i, h)),
                  pl.BlockSpec((nb, FOX_KDIM, tk), lambda h, i: (0, h, 0)),
                  pl.BlockSpec((s, HEAD_DIM), lambda h, i: (0, v_col // HEAD_DIM + h))],
        out_specs=pl.BlockSpec((tq, HEAD_DIM), lambda h, i: (i, h)),
        scratch_shapes=[pltpu.VMEM((s, FOX_KDIM), BF16),
                        pltpu.VMEM((2, 2, tk, tk), F32),
                        pltpu.VMEM((2, tk, 1), F32),
                        pltpu.VMEM((2, tk, FOX_KDIM), F32)],
        compiler_params=_cparams(("parallel", "arbitrary")),
        name="fox_attn",
    )(r, qa, kta, pm)


def _merge_kernel(oa_ref, ob_ref, wa_ref, wb_ref, ga_ref, gb_ref, y_ref):
    ya = jnp.dot(oa_ref[...], wa_ref[...], preferred_element_type=F32)
    yb = jnp.dot(ob_ref[...], wb_ref[...], preferred_element_type=F32)
    y = _sigmoid(ga_ref[...].astype(F32)) * ya + _sigmoid(gb_ref[...].astype(F32)) * yb
    y_ref[...] = y.astype(y_ref.dtype)


def _merge(oa, ob, wa, wb, pg):
    m, ka = oa.shape
    kb = ob.shape[1]
    n = wa.shape[1]
    tm = _tile(m, 1024, SUBLANES)
    tn = _tile(n, 512)
    return pl.pallas_call(
        _merge_kernel,
        out_shape=jax.ShapeDtypeStruct((m, n), BF16),
        grid=(m // tm, n // tn),
        in_specs=[pl.BlockSpec((tm, ka), lambda i, j: (i, 0)),
                  pl.BlockSpec((tm, kb), lambda i, j: (i, 0)),
                  pl.BlockSpec((ka, tn), lambda i, j: (0, j)),
                  pl.BlockSpec((kb, tn), lambda i, j: (0, j)),
                  pl.BlockSpec((tm, tn), lambda i, j: (i, j)),
                  pl.BlockSpec((tm, tn), lambda i, j: (i, n // tn + j))],
        out_specs=pl.BlockSpec((tm, tn), lambda i, j: (i, j)),
        compiler_params=_cparams(("parallel", "arbitrary")),
        name="branch_merge",
    )(oa, ob, wa, wb, pg, pg)


def _resproj_kernel(a_ref, w_ref, x_ref, gt_ref, o_ref):
    y = jnp.dot(a_ref[...], w_ref[...], preferred_element_type=F32)
    o_ref[...] = x_ref[...] + gt_ref[...] * y


def _resproj(a, w, x, mod, gate_idx, tm_pref, tn_pref, name):
    m, k = a.shape
    n = w.shape[1]
    tm = _tile(m, tm_pref, SUBLANES)
    tn = _tile(n, tn_pref)
    return pl.pallas_call(
        _resproj_kernel,
        out_shape=jax.ShapeDtypeStruct((m, n), F32),
        grid=(m // tm, n // tn),
        in_specs=[pl.BlockSpec((tm, k), lambda i, j: (i, 0)),
                  pl.BlockSpec((k, tn), lambda i, j: (0, j)),
                  pl.BlockSpec((tm, tn), lambda i, j: (i, j)),
                  pl.BlockSpec((1, tn), lambda i, j: (0, gate_idx * (n // tn) + j))],
        out_specs=pl.BlockSpec((tm, tn), lambda i, j: (i, j)),
        compiler_params=_cparams(("parallel", "arbitrary")),
        name=name,
    )(a, w, x, mod)


def _swiglu_kernel(h_ref, wg_ref, wu_ref, o_ref):
    h = h_ref[...]
    g = jnp.dot(h, wg_ref[...], preferred_element_type=F32)
    u = jnp.dot(h, wu_ref[...], preferred_element_type=F32)
    o_ref[...] = (g * _sigmoid(g) * u).astype(o_ref.dtype)


def _swiglu(h, w):
    m, k = h.shape
    dff = w.shape[1] // 2
    tm = _tile(m, 1024, SUBLANES)
    tn = _tile(dff, 512)
    nj = dff // tn
    return pl.pallas_call(
        _swiglu_kernel,
        out_shape=jax.ShapeDtypeStruct((m, dff), BF16),
        grid=(m // tm, nj),
        in_specs=[pl.BlockSpec((tm, k), lambda i, j: (i, 0)),
                  pl.BlockSpec((k, tn), lambda i, j: (0, j)),
                  pl.BlockSpec((k, tn), lambda i, j: (0, nj + j))],
        out_specs=pl.BlockSpec((tm, tn), lambda i, j: (i, j)),
        compiler_params=_cparams(("parallel", "arbitrary")),
        name="ffn_swiglu",
    )(h, w, w)


def kernel(x, c, w_ada, b_ada, norm_mix_g, norm_ffn_g, w_in, fox_f_bias, hgrn_lb_table, hgrn_onorm_g,
           fox_q_norm_g, fox_k_norm_g, w_branch_a, w_branch_b, w_out, w_ffn_in, w_ffn_out):
    batch, seq, d = x.shape
    assert batch == 1 and w_ada.shape[0] == 1, "single sequence, single layer"
    hw = hgrn_onorm_g.shape[1] * hgrn_onorm_g.shape[2]
    fw = fox_q_norm_g.shape[1] * fox_q_norm_g.shape[2]
    nfh = fox_q_norm_g.shape[1]
    assert hgrn_onorm_g.shape[2] == HEAD_DIM and fox_q_norm_g.shape[2] == HEAD_DIM
    assert nfh <= LANES and hw == fw and seq % (2 * FOX_TK) == 0
    x2 = x.reshape(seq, d)

    w_bf = w_in[0].astype(BF16)
    f_col = 4 * hw + 3 * fw
    w_gates = w_in[0][:, f_col + nfh:].astype(BF16)
    fbias = jnp.pad(fox_f_bias[0], (0, LANES - nfh)).reshape(1, LANES)

    mod = _ada_mod(c, w_ada[0], b_ada[0])

    h = _normmod(x2, norm_mix_g[0], mod, 0, 1)
    pq = _matmul(h, w_bf, 0, hw, BF16, "in_proj_q")
    hf = _matmul(h, w_bf, hw, hw, F32, "in_proj_f")
    pm = _matmul(h, w_bf, 2 * hw, 2 * hw + 3 * fw, BF16, "in_proj_main")
    ff = _matmul(h, w_bf, f_col, LANES, F32, "in_proj_ff")
    pg = _matmul(h, w_gates, 0, 2 * d, BF16, "in_proj_gates")

    o_a = _hgrn(pq, pm, hf, hgrn_lb_table, hgrn_onorm_g[0], hw, 0, hw)

    qa, kta, r = _foxprep(pm, ff, fbias, fox_q_norm_g[0], fox_k_norm_g[0], fw, 2 * hw, 2 * hw + fw, FOX_TK)
    o_b = _fox(qa, kta, r, pm, 2 * hw + 2 * fw, FOX_TK)

    y = _merge(o_a, o_b, w_branch_a[0].astype(BF16), w_branch_b[0].astype(BF16), pg)
    x1 = _resproj(y, w_out[0].astype(BF16), x2, mod, 2, 1024, 512, "out_proj")

    h2 = _normmod(x1, norm_ffn_g[0], mod, 3, 4)
    act = _swiglu(h2, w_ffn_in[0].astype(BF16))
    x3 = _resproj(act, w_ffn_out[0].astype(BF16), x1, mod, 5, 512, 512, "ffn_out")
    return x3.reshape(batch, seq, d)
```

```python
import functools
import math

import jax
import jax.numpy as jnp
from jax import lax
from jax.experimental import pallas as pl
from jax.experimental.pallas import tpu as pltpu

F32 = jnp.float32
BF16 = jnp.bfloat16

EPS = 1e-6
LOG2E = 1.4426950408889634
LANES = 128
SUBLANES = 8
HEAD_DIM = 128
HGRN_SUB = 16
HGRN_FSUB = 32
HGRN_UNROLL = 2
HGRN_MIN_LB = 2.0 ** -6
FOX_TK = 512
FOX_KDIM = 2 * HEAD_DIM
V7X_VMEM_LIMIT_BYTES = 56 * 1024 * 1024


def _cparams(semantics, vmem_bytes=V7X_VMEM_LIMIT_BYTES):
    return pltpu.CompilerParams(dimension_semantics=semantics, vmem_limit_bytes=vmem_bytes)


def _tile(dim, pref, align=LANES):
    if dim <= pref:
        return dim
    t = (pref // align) * align
    while t >= align:
        if dim % t == 0:
            return t
        t -= align
    raise ValueError(f"no {align}-aligned tile for {dim}")


def _sigmoid(x):
    return 1.0 / (1.0 + jnp.exp(-x))


def _ada_kernel(c_ref, w_ref, b_ref, o_ref):
    c = c_ref[...]
    s = c * _sigmoid(c)
    o_ref[...] = jnp.sum(w_ref[...] * s, axis=0, keepdims=True) + b_ref[...]


def _ada_mod(c, w_ada, b_ada):
    d, n = w_ada.shape
    tn = _tile(n, 512)
    return pl.pallas_call(
        _ada_kernel,
        out_shape=jax.ShapeDtypeStruct((1, n), F32),
        grid=(n // tn,),
        in_specs=[pl.BlockSpec((d, 1), lambda j: (0, 0)),
                  pl.BlockSpec((d, tn), lambda j: (0, j)),
                  pl.BlockSpec((1, tn), lambda j: (0, j))],
        out_specs=pl.BlockSpec((1, tn), lambda j: (0, j)),
        compiler_params=_cparams(("parallel",)),
        name="ada_mod",
    )(c.reshape(d, 1), w_ada, b_ada.reshape(1, n))


def _normmod_kernel(x_ref, g_ref, sh_ref, sc_ref, o_ref):
    x = x_ref[...]
    ms = jnp.mean(x * x, axis=-1, keepdims=True)
    xn = x * lax.rsqrt(ms + EPS)
    o_ref[...] = (xn * g_ref[...] * (1.0 + sc_ref[...]) + sh_ref[...]).astype(o_ref.dtype)


def _normmod(x, gain, mod, shift_idx, scale_idx):
    s, d = x.shape
    tm = _tile(s, 256, SUBLANES)
    return pl.pallas_call(
        _normmod_kernel,
        out_shape=jax.ShapeDtypeStruct((s, d), BF16),
        grid=(s // tm,),
        in_specs=[pl.BlockSpec((tm, d), lambda i: (i, 0)),
                  pl.BlockSpec((1, d), lambda i: (0, 0)),
                  pl.BlockSpec((1, d), lambda i: (0, shift_idx)),
                  pl.BlockSpec((1, d), lambda i: (0, scale_idx))],
        out_specs=pl.BlockSpec((tm, d), lambda i: (i, 0)),
        compiler_params=_cparams(("parallel",)),
        name="normmod",
    )(x, gain.reshape(1, d), mod, mod)


def _mm_kernel(x_ref, w_ref, o_ref):
    o_ref[...] = jnp.dot(x_ref[...], w_ref[...], preferred_element_type=F32).astype(o_ref.dtype)


def _matmul(x, w, col0, ncols, out_dtype, name):
    m, k = x.shape
    tm = _tile(m, 1024, SUBLANES)
    tn = _tile(math.gcd(ncols, col0) if col0 else ncols, 1024)
    return pl.pallas_call(
        _mm_kernel,
        out_shape=jax.ShapeDtypeStruct((m, ncols), out_dtype),
        grid=(m // tm, ncols // tn),
        in_specs=[pl.BlockSpec((tm, k), lambda i, j: (i, 0)),
                  pl.BlockSpec((k, tn), lambda i, j: (0, col0 // tn + j))],
        out_specs=pl.BlockSpec((tm, tn), lambda i, j: (i, j)),
        compiler_params=_cparams(("parallel", "arbitrary")),
        name=name,
    )(x, w)


def _hgrn_kernel(q_ref, i_ref, g_ref, f_ref, lbt_ref, og_ref, o_ref, st_ref, *, hb, tt):
    @pl.when(pl.program_id(1) == 0)
    def _():
        st_ref[...] = jnp.zeros_like(st_ref)

    tbl = lbt_ref[...]
    e = jnp.exp(tbl - jnp.max(tbl, axis=0, keepdims=True))
    lb = e[0:1, :] / jnp.sum(e, axis=0, keepdims=True)
    half = SUBLANES
    nt_dims = (((1,), (1,)), ((), ()))
    tn_dims = (((0,), (0,)), ((), ()))

    def gates(rows, cs, nrows):
        lbh = lb[:, cs]
        f = lbh + (1.0 - lbh) * _sigmoid(f_ref[rows, cs])
        b = jnp.log2(f)
        row = lax.broadcasted_iota(jnp.int32, (nrows, LANES), 0)
        sh = 1
        while sh < nrows:
            b = b + jnp.where(row >= sh, pltpu.roll(b, sh, axis=0), 0.0)
            sh *= 2
        return 1.0 - f, b

    def finish(o, rows, cs):
        ms = jnp.mean(o * o, axis=-1, keepdims=True)
        on = o * lax.rsqrt(ms + EPS) * og_ref[:, cs]
        gate = g_ref[rows, cs].astype(F32)
        o_ref[rows, cs] = (on * (gate * _sigmoid(gate))).astype(o_ref.dtype)

    mid = HGRN_FSUB // 2
    tri_r = lax.broadcasted_iota(jnp.int32, (HGRN_FSUB, HGRN_FSUB), 0)
    tri_c = lax.broadcasted_iota(jnp.int32, (HGRN_FSUB, HGRN_FSUB), 1)

    def body_factored(n, carry):
        pending = []
        for u, h in [(u, h) for u in range(HGRN_UNROLL) for h in range(hb)]:
            r0 = pl.multiple_of((n * HGRN_UNROLL + u) * HGRN_FSUB, HGRN_FSUB)
            rows = pl.ds(r0, HGRN_FSUB)
            cs = slice(h * LANES, (h + 1) * LANES)
            q = q_ref[rows, cs].astype(F32)
            v = i_ref[rows, cs]
            k, b = gates(rows, cs, HGRN_FSUB)
            r = b[mid - 1:mid, :]
            b_last = b[HGRN_FSUB - 1:HGRN_FSUB, :]
            qc = q * jnp.exp2(b - r)
            kc = (k * jnp.exp2(r - b)).astype(BF16)
            a = lax.dot_general(qc.astype(BF16), kc, nt_dims, preferred_element_type=F32)
            st = st_ref[h]
            o = lax.dot_general((qc * jnp.exp2(r)).astype(BF16), st.astype(BF16), nt_dims,
                                preferred_element_type=F32)
            kt = (k * jnp.exp2(b_last - b)).astype(BF16)
            upd = lax.dot_general(v, kt, tn_dims, preferred_element_type=F32)
            st_ref[h] = st * jnp.exp2(b_last) + upd
            pending.append((a, o, v, rows, cs))
        for a, o, v, rows, cs in pending:
            p = jnp.where(tri_c <= tri_r, a, 0.0).astype(BF16)
            finish(o + jnp.dot(p, v, preferred_element_type=F32), rows, cs)
        return carry

    rowc = lax.broadcasted_iota(jnp.int32, (SUBLANES, 1), 0)

    def body_exact(n, carry):
        r0 = pl.multiple_of(n * HGRN_SUB, HGRN_SUB)
        rows = pl.ds(r0, HGRN_SUB)
        for h in range(hb):
            cs = slice(h * LANES, (h + 1) * LANES)
            q = q_ref[rows, cs].astype(F32)
            v = i_ref[rows, cs]
            vf = v.astype(F32)
            k, b = gates(rows, cs, HGRN_SUB)
            b_last = b[HGRN_SUB - 1:HGRN_SUB, :]
            st = st_ref[h]
            qt = (q * jnp.exp2(b)).astype(BF16)
            o = lax.dot_general(qt, st.astype(BF16), nt_dims, preferred_element_type=F32)
            o_lo, o_hi = o[:half], o[half:]
            q_lo, q_hi = q[:half], q[half:]
            b_lo, b_hi = b[:half], b[half:]
            for s in range(HGRN_SUB):
                ks, bs, vs = k[s:s + 1, :], b[s:s + 1, :], vf[s:s + 1, :]
                c_hi = jnp.sum(q_hi * ks * jnp.exp2(b_hi - bs), axis=-1, keepdims=True)
                if s < half:
                    c_lo = jnp.sum(q_lo * ks * jnp.exp2(b_lo - bs), axis=-1, keepdims=True)
                    o_lo = o_lo + jnp.where(rowc >= s, c_lo, 0.0) * vs
                else:
                    c_hi = jnp.where(rowc >= s - half, c_hi, 0.0)
                o_hi = o_hi + c_hi * vs
            kt = (k * jnp.exp2(b_last - b)).astype(BF16)
            upd = lax.dot_general(v, kt, tn_dims, preferred_element_type=F32)
            st_ref[h] = st * jnp.exp2(b_last) + upd
            finish(jnp.concatenate([o_lo, o_hi], axis=0), rows, cs)
        return carry

    factorable = jnp.min(lb) >= HGRN_MIN_LB

    @pl.when(factorable)
    def _():
        lax.fori_loop(0, tt // (HGRN_FSUB * HGRN_UNROLL), body_factored, 0)

    @pl.when(jnp.logical_not(factorable))
    def _():
        lax.fori_loop(0, tt // HGRN_SUB, body_exact, 0)


def _hgrn(pq, pm, hf, lb_table, onorm_g, width, i_col, g_col):
    s = pq.shape[0]
    hb = min(8, width // HEAD_DIM)
    bw = hb * HEAD_DIM
    tt = _tile(s, 512, HGRN_SUB)
    nl = lb_table.shape[0]
    kern = functools.partial(_hgrn_kernel, hb=hb, tt=tt)
    return pl.pallas_call(
        kern,
        out_shape=jax.ShapeDtypeStruct((s, width), BF16),
        grid=(width // bw, s // tt),
        in_specs=[pl.BlockSpec((tt, bw), lambda h, t: (t, h)),
                  pl.BlockSpec((tt, bw), lambda h, t: (t, i_col // bw + h)),
                  pl.BlockSpec((tt, bw), lambda h, t: (t, g_col // bw + h)),
                  pl.BlockSpec((tt, bw), lambda h, t: (t, h)),
                  pl.BlockSpec((nl, bw), lambda h, t: (0, h)),
                  pl.BlockSpec((1, bw), lambda h, t: (0, h))],
        out_specs=pl.BlockSpec((tt, bw), lambda h, t: (t, h)),
        scratch_shapes=[pltpu.VMEM((hb, HEAD_DIM, HEAD_DIM), F32)],
        compiler_params=_cparams(("parallel", "arbitrary")),
        name="hgrn2",
    )(pq, pm, pm, hf, lb_table, onorm_g.reshape(1, width))


def _split3(x):
    hi = x.astype(BF16).astype(F32)
    r = x - hi
    mid = r.astype(BF16).astype(F32)
    lo = (r - mid).astype(BF16).astype(F32)
    return hi, mid, lo


def _foxprep_kernel(q_ref, k_ref, ff_ref, fb_ref, gq_ref, gk_ref, qo_ref, kto_ref, r_ref, carry_ref, *, nh, scale):
    @pl.when(pl.program_id(0) == 0)
    def _():
        carry_ref[...] = jnp.zeros_like(carry_ref)

    tk = q_ref.shape[0]
    x = ff_ref[...] + fb_ref[...]
    ls = (jnp.minimum(x, 0.0) - jnp.log(1.0 + jnp.exp(-jnp.abs(x)))) * LOG2E
    ri = lax.broadcasted_iota(jnp.int32, (tk, tk), 0)
    ci = lax.broadcasted_iota(jnp.int32, (tk, tk), 1)
    lower = (ci <= ri).astype(F32)
    rel = jnp.dot(lower, ls, precision=lax.Precision.HIGHEST, preferred_element_type=F32)
    base = carry_ref[...]
    r_ref[0] = base
    carry_ref[...] = base + rel[tk - 1:tk, :]

    cq = _split3(rel)
    ck = _split3(rel.T)
    lane = lax.broadcasted_iota(jnp.int32, (tk, LANES), 1)
    sub = lax.broadcasted_iota(jnp.int32, (LANES, tk), 0)
    ones_q = jnp.where((lane >= 3) & (lane < 6), 1.0, 0.0)
    ones_k = jnp.where(sub < 3, 1.0, 0.0)
    for h in range(nh):
        cs = slice(h * LANES, (h + 1) * LANES)
        q = q_ref[:, cs].astype(F32)
        qn = q * lax.rsqrt(jnp.mean(q * q, axis=-1, keepdims=True) + EPS) * gq_ref[:, cs]
        k = k_ref[:, cs].astype(F32)
        kn = k * lax.rsqrt(jnp.mean(k * k, axis=-1, keepdims=True) + EPS) * gk_ref[:, cs]
        eq = ones_q
        ek = ones_k
        for t in range(3):
            eq = jnp.where(lane == t, cq[t][:, h:h + 1], eq)
            ek = jnp.where(sub == 3 + t, -ck[t][h:h + 1, :], ek)
        c0 = h * FOX_KDIM
        qo_ref[:, c0:c0 + LANES] = (qn * scale).astype(qo_ref.dtype)
        qo_ref[:, c0 + LANES:c0 + FOX_KDIM] = eq.astype(qo_ref.dtype)
        kto_ref[0, c0:c0 + LANES, :] = kn.T.astype(kto_ref.dtype)
        kto_ref[0, c0 + LANES:c0 + FOX_KDIM, :] = ek.astype(kto_ref.dtype)


def _foxprep(pm, ff, fbias, gq, gk, width, q_col, k_col, tk):
    s = pm.shape[0]
    nh = width // HEAD_DIM
    nb = s // tk
    kern = functools.partial(_foxprep_kernel, nh=nh, scale=LOG2E * float(HEAD_DIM) ** -0.5)
    qa, kta, r = pl.pallas_call(
        kern,
        out_shape=(jax.ShapeDtypeStruct((s, nh * FOX_KDIM), BF16),
                   jax.ShapeDtypeStruct((nb, nh * FOX_KDIM, tk), BF16),
                   jax.ShapeDtypeStruct((nb, 1, LANES), F32)),
        grid=(nb,),
        in_specs=[pl.BlockSpec((tk, width), lambda i: (i, q_col // width)),
                  pl.BlockSpec((tk, width), lambda i: (i, k_col // width)),
                  pl.BlockSpec((tk, LANES), lambda i: (i, 0)),
                  pl.BlockSpec((1, LANES), lambda i: (0, 0)),
                  pl.BlockSpec((1, width), lambda i: (0, 0)),
                  pl.BlockSpec((1, width), lambda i: (0, 0))],
        out_specs=(pl.BlockSpec((tk, nh * FOX_KDIM), lambda i: (i, 0)),
                   pl.BlockSpec((1, nh * FOX_KDIM, tk), lambda i: (i, 0, 0)),
                   pl.BlockSpec((1, 1, LANES), lambda i: (i, 0, 0))),
        scratch_shapes=[pltpu.VMEM((1, LANES), F32)],
        compiler_params=_cparams(("arbitrary",)),
        name="fox_prep",
    )(pm, pm, ff, fbias, gq.reshape(1, width), gk.reshape(1, width))
    return qa, kta, r.reshape(nb, LANES)


def _fox_kernel(r_ref, q_ref, kt_ref, v_ref, o_ref, v2_ref, s_ref, m_ref, acc_ref, *, tk):
    h = pl.program_id(0)
    qi = pl.program_id(1)

    @pl.when(qi == 0)
    def _():
        v2_ref[:, :LANES] = v_ref[...]
        lane = lax.broadcasted_iota(jnp.int32, v_ref.shape, 1)
        v2_ref[:, LANES:] = jnp.where(lane == 0, 1.0, 0.0).astype(v2_ref.dtype)

    m_ref[...] = jnp.full(m_ref.shape, -jnp.inf, F32)
    acc_ref[...] = jnp.zeros(acc_ref.shape, F32)
    rr = lax.broadcasted_iota(jnp.int32, (tk, tk), 0)
    cc = lax.broadcasted_iota(jnp.int32, (tk, tk), 1)

    def scores(c, j, slot):
        s_ref[slot, c] = jnp.dot(q_ref[c * tk:(c + 1) * tk, :], kt_ref[j], preferred_element_type=F32)

    def fold(c, j, slot, diag):
        s = s_ref[slot, c]
        if diag:
            s = jnp.where(cc <= rr, s, -jnp.inf)
        d = r_ref[2 * qi + c, h] - r_ref[j, h]
        m = m_ref[c]
        m_new = jnp.maximum(m, jnp.max(s, axis=-1, keepdims=True) + d)
        p = jnp.exp2(s - (m_new - d))
        r0 = pl.multiple_of(j * tk, tk)
        pv = jnp.dot(p.astype(BF16), v2_ref[pl.ds(r0, tk), :], preferred_element_type=F32)
        acc_ref[c] = jnp.exp2(m - m_new) * acc_ref[c] + pv
        m_ref[c] = m_new

    scores(0, 0, 0)
    scores(1, 0, 0)

    def body(jp, carry):
        j = 2 * jp
        for slot in range(2):
            scores(0, j + slot + 1, 1 - slot)
            scores(1, j + slot + 1, 1 - slot)
            fold(0, j + slot, slot, False)
            fold(1, j + slot, slot, False)
        return carry

    lax.fori_loop(0, qi, body, 0)
    scores(1, 2 * qi + 1, 1)
    fold(0, 2 * qi, 0, True)
    fold(1, 2 * qi, 0, False)
    fold(1, 2 * qi + 1, 1, True)
    for c in range(2):
        acc = acc_ref[c]
        o_ref[c * tk:(c + 1) * tk, :] = (acc[:, :LANES] / acc[:, LANES:LANES + 1]).astype(o_ref.dtype)


def _fox(qa, kta, r, pm, v_col, tk):
    s = qa.shape[0]
    nh = qa.shape[1] // FOX_KDIM
    nb = s // tk
    tq = 2 * tk
    kern = functools.partial(_fox_kernel, tk=tk)
    return pl.pallas_call(
        kern,
        out_shape=jax.ShapeDtypeStruct((s, nh * HEAD_DIM), BF16),
        grid=(nh, s // tq),
        in_specs=[pl.BlockSpec(memory_space=pltpu.SMEM),
                  pl.BlockSpec((tq, FOX_KDIM), lambda h, i: (i, h)),
                  pl.BlockSpec((nb, FOX_KDIM, tk), lambda h, i: (0, h, 0)),
                  pl.BlockSpec((s, HEAD_DIM), lambda h, i: (0, v_col // HEAD_DIM + h))],
        out_specs=pl.BlockSpec((tq, HEAD_DIM), lambda h, i: (i, h)),
        scratch_shapes=[pltpu.VMEM((s, FOX_KDIM), BF16),
                        pltpu.VMEM((2, 2, tk, tk), F32),
                        pltpu.VMEM((2, tk, 1), F32),
                        pltpu.VMEM((2, tk, FOX_KDIM), F32)],
        compiler_params=_cparams(("parallel", "arbitrary")),
        name="fox_attn",
    )(r, qa, kta, pm)


def _merge_kernel(oa_ref, ob_ref, wa_ref, wb_ref, ga_ref, gb_ref, y_ref):
    ya = jnp.dot(oa_ref[...], wa_ref[...], preferred_element_type=F32)
    yb = jnp.dot(ob_ref[...], wb_ref[...], preferred_element_type=F32)
    y = _sigmoid(ga_ref[...].astype(F32)) * ya + _sigmoid(gb_ref[...].astype(F32)) * yb
    y_ref[...] = y.astype(y_ref.dtype)


def _merge(oa, ob, wa, wb, pg):
    m, ka = oa.shape
    kb = ob.shape[1]
    n = wa.shape[1]
    tm = _tile(m, 1024, SUBLANES)
    tn = _tile(n, 512)
    return pl.pallas_call(
        _merge_kernel,
        out_shape=jax.ShapeDtypeStruct((m, n), BF16),
        grid=(m // tm, n // tn),
        in_specs=[pl.BlockSpec((tm, ka), lambda i, j: (i, 0)),
                  pl.BlockSpec((tm, kb), lambda i, j: (i, 0)),
                  pl.BlockSpec((ka, tn), lambda i, j: (0, j)),
                  pl.BlockSpec((kb, tn), lambda i, j: (0, j)),
                  pl.BlockSpec((tm, tn), lambda i, j: (i, j)),
                  pl.BlockSpec((tm, tn), lambda i, j: (i, n // tn + j))],
        out_specs=pl.BlockSpec((tm, tn), lambda i, j: (i, j)),
        compiler_params=_cparams(("parallel", "arbitrary")),
        name="branch_merge",
    )(oa, ob, wa, wb, pg, pg)


def _resproj_kernel(a_ref, w_ref, x_ref, gt_ref, o_ref):
    y = jnp.dot(a_ref[...], w_ref[...], preferred_element_type=F32)
    o_ref[...] = x_ref[...] + gt_ref[...] * y


def _resproj(a, w, x, mod, gate_idx, tm_pref, tn_pref, name):
    m, k = a.shape
    n = w.shape[1]
    tm = _tile(m, tm_pref, SUBLANES)
    tn = _tile(n, tn_pref)
    return pl.pallas_call(
        _resproj_kernel,
        out_shape=jax.ShapeDtypeStruct((m, n), F32),
        grid=(m // tm, n // tn),
        in_specs=[pl.BlockSpec((tm, k), lambda i, j: (i, 0)),
                  pl.BlockSpec((k, tn), lambda i, j: (0, j)),
                  pl.BlockSpec((tm, tn), lambda i, j: (i, j)),
                  pl.BlockSpec((1, tn), lambda i, j: (0, gate_idx * (n // tn) + j))],
        out_specs=pl.BlockSpec((tm, tn), lambda i, j: (i, j)),
        compiler_params=_cparams(("parallel", "arbitrary")),
        name=name,
    )(a, w, x, mod)


def _swiglu_kernel(h_ref, wg_ref, wu_ref, o_ref):
    h = h_ref[...]
    g = jnp.dot(h, wg_ref[...], preferred_element_type=F32)
    u = jnp.dot(h, wu_ref[...], preferred_element_type=F32)
    o_ref[...] = (g * _sigmoid(g) * u).astype(o_ref.dtype)


def _swiglu(h, w):
    m, k = h.shape
    dff = w.shape[1] // 2
    tm = _tile(m, 2048, SUBLANES)
    tn = _tile(dff, 512)
    nj = dff // tn
    return pl.pallas_call(
        _swiglu_kernel,
        out_shape=jax.ShapeDtypeStruct((m, dff), BF16),
        grid=(m // tm, nj),
        in_specs=[pl.BlockSpec((tm, k), lambda i, j: (i, 0)),
                  pl.BlockSpec((k, tn), lambda i, j: (0, j)),
                  pl.BlockSpec((k, tn), lambda i, j: (0, nj + j))],
        out_specs=pl.BlockSpec((tm, tn), lambda i, j: (i, j)),
        compiler_params=_cparams(("parallel", "arbitrary")),
        name="ffn_swiglu",
    )(h, w, w)


def kernel(x, c, w_ada, b_ada, norm_mix_g, norm_ffn_g, w_in, fox_f_bias, hgrn_lb_table, hgrn_onorm_g,
           fox_q_norm_g, fox_k_norm_g, w_branch_a, w_branch_b, w_out, w_ffn_in, w_ffn_out):
    batch, seq, d = x.shape
    assert batch == 1 and w_ada.shape[0] == 1, "single sequence, single layer"
    hw = hgrn_onorm_g.shape[1] * hgrn_onorm_g.shape[2]
    fw = fox_q_norm_g.shape[1] * fox_q_norm_g.shape[2]
    nfh = fox_q_norm_g.shape[1]
    assert hgrn_onorm_g.shape[2] == HEAD_DIM and fox_q_norm_g.shape[2] == HEAD_DIM
    assert nfh <= LANES and hw == fw and seq % (2 * FOX_TK) == 0
    x2 = x.reshape(seq, d)

    f_col = 4 * hw + 3 * fw
    w_bf = w_in[0][:, :f_col + LANES].astype(BF16)
    w_gates = w_in[0][:, f_col + nfh:].astype(BF16)
    fbias = jnp.pad(fox_f_bias[0], (0, LANES - nfh)).reshape(1, LANES)

    mod = _ada_mod(c, w_ada[0], b_ada[0])

    h = _normmod(x2, norm_mix_g[0], mod, 0, 1)
    pq = _matmul(h, w_bf, 0, hw, BF16, "in_proj_q")
    hf = _matmul(h, w_bf, hw, hw, F32, "in_proj_f")
    pm = _matmul(h, w_bf, 2 * hw, 2 * hw + 3 * fw, BF16, "in_proj_main")
    ff = _matmul(h, w_bf, f_col, LANES, F32, "in_proj_ff")
    pg = _matmul(h, w_gates, 0, 2 * d, BF16, "in_proj_gates")

    o_a = _hgrn(pq, pm, hf, hgrn_lb_table, hgrn_onorm_g[0], hw, 0, hw)

    qa, kta, r = _foxprep(pm, ff, fbias, fox_q_norm_g[0], fox_k_norm_g[0], fw, 2 * hw, 2 * hw + fw, FOX_TK)
    o_b = _fox(qa, kta, r, pm, 2 * hw + 2 * fw, FOX_TK)

    y = _merge(o_a, o_b, w_branch_a[0].astype(BF16), w_branch_b[0].astype(BF16), pg)
    x1 = _resproj(y, w_out[0].astype(BF16), x2, mod, 2, 1024, 512, "out_proj")

    h2 = _normmod(x1, norm_ffn_g[0], mod, 3, 4)
    act = _swiglu(h2, w_ffn_in[0].astype(BF16))
    x3 = _resproj(act, w_ffn_out[0].astype(BF16), x1, mod, 5, 512, 512, "ffn_out")
    return x3.reshape(batch, seq, d)
```

```python
import functools
import math

import jax
import jax.numpy as jnp
from jax import lax
from jax.experimental import pallas as pl
from jax.experimental.pallas import tpu as pltpu

F32 = jnp.float32
BF16 = jnp.bfloat16

EPS = 1e-6
LOG2E = 1.4426950408889634
LANES = 128
SUBLANES = 8
HEAD_DIM = 128
HGRN_SUB = 16
HGRN_FSUB = 32
HGRN_UNROLL = 2
HGRN_MIN_LB = 2.0 ** -6
FOX_TK = 512
FOX_KDIM = 2 * HEAD_DIM
V7X_VMEM_LIMIT_BYTES = 56 * 1024 * 1024


def _cparams(semantics, vmem_bytes=V7X_VMEM_LIMIT_BYTES):
    return pltpu.CompilerParams(dimension_semantics=semantics, vmem_limit_bytes=vmem_bytes)


def _tile(dim, pref, align=LANES):
    if dim <= pref:
        return dim
    t = (pref // align) * align
    while t >= align:
        if dim % t == 0:
            return t
        t -= align
    raise ValueError(f"no {align}-aligned tile for {dim}")


def _sigmoid(x):
    return 1.0 / (1.0 + jnp.exp(-x))


def _ada_kernel(c_ref, w_ref, b_ref, o_ref):
    c = c_ref[...]
    s = c * _sigmoid(c)
    o_ref[...] = jnp.sum(w_ref[...] * s, axis=0, keepdims=True) + b_ref[...]


def _ada_mod(c, w_ada, b_ada):
    d, n = w_ada.shape
    tn = _tile(n, 512)
    return pl.pallas_call(
        _ada_kernel,
        out_shape=jax.ShapeDtypeStruct((1, n), F32),
        grid=(n // tn,),
        in_specs=[pl.BlockSpec((d, 1), lambda j: (0, 0)),
                  pl.BlockSpec((d, tn), lambda j: (0, j)),
                  pl.BlockSpec((1, tn), lambda j: (0, j))],
        out_specs=pl.BlockSpec((1, tn), lambda j: (0, j)),
        compiler_params=_cparams(("parallel",)),
        name="ada_mod",
    )(c.reshape(d, 1), w_ada, b_ada.reshape(1, n))


def _normmod_kernel(x_ref, g_ref, sh_ref, sc_ref, o_ref):
    x = x_ref[...]
    ms = jnp.mean(x * x, axis=-1, keepdims=True)
    xn = x * lax.rsqrt(ms + EPS)
    o_ref[...] = (xn * g_ref[...] * (1.0 + sc_ref[...]) + sh_ref[...]).astype(o_ref.dtype)


def _normmod(x, gain, mod, shift_idx, scale_idx):
    s, d = x.shape
    tm = _tile(s, 256, SUBLANES)
    return pl.pallas_call(
        _normmod_kernel,
        out_shape=jax.ShapeDtypeStruct((s, d), BF16),
        grid=(s // tm,),
        in_specs=[pl.BlockSpec((tm, d), lambda i: (i, 0)),
                  pl.BlockSpec((1, d), lambda i: (0, 0)),
                  pl.BlockSpec((1, d), lambda i: (0, shift_idx)),
                  pl.BlockSpec((1, d), lambda i: (0, scale_idx))],
        out_specs=pl.BlockSpec((tm, d), lambda i: (i, 0)),
        compiler_params=_cparams(("parallel",)),
        name="normmod",
    )(x, gain.reshape(1, d), mod, mod)


def _cast_weight(w_ref, wb_ref):
    @pl.when(pl.program_id(1) == 0)
    def _():
        wb_ref[...] = w_ref[...].astype(wb_ref.dtype)


def _mm_kernel(x_ref, w_ref, o_ref, wb_ref):
    _cast_weight(w_ref, wb_ref)
    o_ref[...] = jnp.dot(x_ref[...], wb_ref[...], preferred_element_type=F32).astype(o_ref.dtype)


def _matmul(x, w, col0, ncols, out_dtype, name):
    m, k = x.shape
    tm = _tile(m, 1024, SUBLANES)
    tn = _tile(math.gcd(ncols, col0) if col0 else ncols, 512)
    return pl.pallas_call(
        _mm_kernel,
        out_shape=jax.ShapeDtypeStruct((m, ncols), out_dtype),
        grid=(ncols // tn, m // tm),
        in_specs=[pl.BlockSpec((tm, k), lambda j, i: (i, 0)),
                  pl.BlockSpec((k, tn), lambda j, i: (0, col0 // tn + j))],
        out_specs=pl.BlockSpec((tm, tn), lambda j, i: (i, j)),
        scratch_shapes=[pltpu.VMEM((k, tn), BF16)],
        compiler_params=_cparams(("parallel", "arbitrary")),
        name=name,
    )(x, w)


def _shiftcast_kernel(a_ref, b_ref, o_ref, *, shift):
    w = a_ref.shape[1]
    cat = jnp.concatenate([a_ref[...], b_ref[...]], axis=1)
    o_ref[...] = cat[:, shift:shift + w].astype(o_ref.dtype)


def _shiftcast(w, col0, shift, ncols):
    k = w.shape[0]
    tn = _tile(math.gcd(ncols, col0), 1024)
    tr = _tile(k, 512, SUBLANES)
    kern = functools.partial(_shiftcast_kernel, shift=shift)
    return pl.pallas_call(
        kern,
        out_shape=jax.ShapeDtypeStruct((k, ncols), BF16),
        grid=(k // tr, ncols // tn),
        in_specs=[pl.BlockSpec((tr, tn), lambda r, j: (r, col0 // tn + j)),
                  pl.BlockSpec((tr, tn), lambda r, j: (r, col0 // tn + j + 1))],
        out_specs=pl.BlockSpec((tr, tn), lambda r, j: (r, j)),
        compiler_params=_cparams(("parallel", "parallel")),
        name="gate_weight_shift",
    )(w, w)


def _hgrn_kernel(q_ref, i_ref, g_ref, f_ref, lbt_ref, og_ref, o_ref, st_ref, *, hb, tt):
    @pl.when(pl.program_id(1) == 0)
    def _():
        st_ref[...] = jnp.zeros_like(st_ref)

    tbl = lbt_ref[...]
    e = jnp.exp(tbl - jnp.max(tbl, axis=0, keepdims=True))
    lb = e[0:1, :] / jnp.sum(e, axis=0, keepdims=True)
    half = SUBLANES
    nt_dims = (((1,), (1,)), ((), ()))
    tn_dims = (((0,), (0,)), ((), ()))

    def gates(rows, cs, nrows):
        lbh = lb[:, cs]
        f = lbh + (1.0 - lbh) * _sigmoid(f_ref[rows, cs])
        b = jnp.log2(f)
        row = lax.broadcasted_iota(jnp.int32, (nrows, LANES), 0)
        sh = 1
        while sh < nrows:
            b = b + jnp.where(row >= sh, pltpu.roll(b, sh, axis=0), 0.0)
            sh *= 2
        return 1.0 - f, b

    def finish(o, rows, cs):
        ms = jnp.mean(o * o, axis=-1, keepdims=True)
        on = o * lax.rsqrt(ms + EPS) * og_ref[:, cs]
        gate = g_ref[rows, cs].astype(F32)
        o_ref[rows, cs] = (on * (gate * _sigmoid(gate))).astype(o_ref.dtype)

    mid = HGRN_FSUB // 2
    tri_r = lax.broadcasted_iota(jnp.int32, (HGRN_FSUB, HGRN_FSUB), 0)
    tri_c = lax.broadcasted_iota(jnp.int32, (HGRN_FSUB, HGRN_FSUB), 1)

    def body_factored(n, carry):
        pending = []
        for u, h in [(u, h) for u in range(HGRN_UNROLL) for h in range(hb)]:
            r0 = pl.multiple_of((n * HGRN_UNROLL + u) * HGRN_FSUB, HGRN_FSUB)
            rows = pl.ds(r0, HGRN_FSUB)
            cs = slice(h * LANES, (h + 1) * LANES)
            q = q_ref[rows, cs].astype(F32)
            v = i_ref[rows, cs]
            k, b = gates(rows, cs, HGRN_FSUB)
            r = b[mid - 1:mid, :]
            b_last = b[HGRN_FSUB - 1:HGRN_FSUB, :]
            qc = q * jnp.exp2(b - r)
            kc = (k * jnp.exp2(r - b)).astype(BF16)
            a = lax.dot_general(qc.astype(BF16), kc, nt_dims, preferred_element_type=F32)
            st = st_ref[h]
            o = lax.dot_general((qc * jnp.exp2(r)).astype(BF16), st.astype(BF16), nt_dims,
                                preferred_element_type=F32)
            kt = (k * jnp.exp2(b_last - b)).astype(BF16)
            upd = lax.dot_general(v, kt, tn_dims, preferred_element_type=F32)
            st_ref[h] = st * jnp.exp2(b_last) + upd
            pending.append((a, o, v, rows, cs))
        for a, o, v, rows, cs in pending:
            p = jnp.where(tri_c <= tri_r, a, 0.0).astype(BF16)
            finish(o + jnp.dot(p, v, preferred_element_type=F32), rows, cs)
        return carry

    rowc = lax.broadcasted_iota(jnp.int32, (SUBLANES, 1), 0)

    def body_exact(n, carry):
        r0 = pl.multiple_of(n * HGRN_SUB, HGRN_SUB)
        rows = pl.ds(r0, HGRN_SUB)
        for h in range(hb):
            cs = slice(h * LANES, (h + 1) * LANES)
            q = q_ref[rows, cs].astype(F32)
            v = i_ref[rows, cs]
            vf = v.astype(F32)
            k, b = gates(rows, cs, HGRN_SUB)
            b_last = b[HGRN_SUB - 1:HGRN_SUB, :]
            st = st_ref[h]
            qt = (q * jnp.exp2(b)).astype(BF16)
            o = lax.dot_general(qt, st.astype(BF16), nt_dims, preferred_element_type=F32)
            o_lo, o_hi = o[:half], o[half:]
            q_lo, q_hi = q[:half], q[half:]
            b_lo, b_hi = b[:half], b[half:]
            for s in range(HGRN_SUB):
                ks, bs, vs = k[s:s + 1, :], b[s:s + 1, :], vf[s:s + 1, :]
                c_hi = jnp.sum(q_hi * ks * jnp.exp2(b_hi - bs), axis=-1, keepdims=True)
                if s < half:
                    c_lo = jnp.sum(q_lo * ks * jnp.exp2(b_lo - bs), axis=-1, keepdims=True)
                    o_lo = o_lo + jnp.where(rowc >= s, c_lo, 0.0) * vs
                else:
                    c_hi = jnp.where(rowc >= s - half, c_hi, 0.0)
                o_hi = o_hi + c_hi * vs
            kt = (k * jnp.exp2(b_last - b)).astype(BF16)
            upd = lax.dot_general(v, kt, tn_dims, preferred_element_type=F32)
            st_ref[h] = st * jnp.exp2(b_last) + upd
            finish(jnp.concatenate([o_lo, o_hi], axis=0), rows, cs)
        return carry

    factorable = jnp.min(lb) >= HGRN_MIN_LB

    @pl.when(factorable)
    def _():
        lax.fori_loop(0, tt // (HGRN_FSUB * HGRN_UNROLL), body_factored, 0)

    @pl.when(jnp.logical_not(factorable))
    def _():
        lax.fori_loop(0, tt // HGRN_SUB, body_exact, 0)


def _hgrn(pq, pm, hf, lb_table, onorm_g, width, i_col, g_col):
    s = pq.shape[0]
    hb = min(8, width // HEAD_DIM)
    bw = hb * HEAD_DIM
    tt = _tile(s, 512, HGRN_SUB)
    nl = lb_table.shape[0]
    kern = functools.partial(_hgrn_kernel, hb=hb, tt=tt)
    return pl.pallas_call(
        kern,
        out_shape=jax.ShapeDtypeStruct((s, width), BF16),
        grid=(width // bw, s // tt),
        in_specs=[pl.BlockSpec((tt, bw), lambda h, t: (t, h)),
                  pl.BlockSpec((tt, bw), lambda h, t: (t, i_col // bw + h)),
                  pl.BlockSpec((tt, bw), lambda h, t: (t, g_col // bw + h)),
                  pl.BlockSpec((tt, bw), lambda h, t: (t, h)),
                  pl.BlockSpec((nl, bw), lambda h, t: (0, h)),
                  pl.BlockSpec((1, bw), lambda h, t: (0, h))],
        out_specs=pl.BlockSpec((tt, bw), lambda h, t: (t, h)),
        scratch_shapes=[pltpu.VMEM((hb, HEAD_DIM, HEAD_DIM), F32)],
        compiler_params=_cparams(("parallel", "arbitrary")),
        name="hgrn2",
    )(pq, pm, pm, hf, lb_table, onorm_g.reshape(1, width))


def _split3(x):
    hi = x.astype(BF16).astype(F32)
    r = x - hi
    mid = r.astype(BF16).astype(F32)
    lo = (r - mid).astype(BF16).astype(F32)
    return hi, mid, lo


def _foxprep_kernel(q_ref, k_ref, ff_ref, fb_ref, gq_ref, gk_ref, qo_ref, kto_ref, r_ref, carry_ref, *, nh, scale):
    @pl.when(pl.program_id(0) == 0)
    def _():
        carry_ref[...] = jnp.zeros_like(carry_ref)

    tk = q_ref.shape[0]
    x = ff_ref[...] + fb_ref[...]
    ls = (jnp.minimum(x, 0.0) - jnp.log(1.0 + jnp.exp(-jnp.abs(x)))) * LOG2E
    ri = lax.broadcasted_iota(jnp.int32, (tk, tk), 0)
    ci = lax.broadcasted_iota(jnp.int32, (tk, tk), 1)
    lower = (ci <= ri).astype(F32)
    rel = jnp.dot(lower, ls, precision=lax.Precision.HIGHEST, preferred_element_type=F32)
    base = carry_ref[...]
    r_ref[0] = base
    carry_ref[...] = base + rel[tk - 1:tk, :]

    cq = _split3(rel)
    ck = _split3(rel.T)
    lane = lax.broadcasted_iota(jnp.int32, (tk, LANES), 1)
    sub = lax.broadcasted_iota(jnp.int32, (LANES, tk), 0)
    ones_q = jnp.where((lane >= 3) & (lane < 6), 1.0, 0.0)
    ones_k = jnp.where(sub < 3, 1.0, 0.0)
    for h in range(nh):
        cs = slice(h * LANES, (h + 1) * LANES)
        q = q_ref[:, cs].astype(F32)
        qn = q * lax.rsqrt(jnp.mean(q * q, axis=-1, keepdims=True) + EPS) * gq_ref[:, cs]
        k = k_ref[:, cs].astype(F32)
        kn = k * lax.rsqrt(jnp.mean(k * k, axis=-1, keepdims=True) + EPS) * gk_ref[:, cs]
        eq = ones_q
        ek = ones_k
        for t in range(3):
            eq = jnp.where(lane == t, cq[t][:, h:h + 1], eq)
            ek = jnp.where(sub == 3 + t, -ck[t][h:h + 1, :], ek)
        c0 = h * FOX_KDIM
        qo_ref[:, c0:c0 + LANES] = (qn * scale).astype(qo_ref.dtype)
        qo_ref[:, c0 + LANES:c0 + FOX_KDIM] = eq.astype(qo_ref.dtype)
        kto_ref[0, c0:c0 + LANES, :] = kn.T.astype(kto_ref.dtype)
        kto_ref[0, c0 + LANES:c0 + FOX_KDIM, :] = ek.astype(kto_ref.dtype)


def _foxprep(pm, ff, fbias, gq, gk, width, q_col, k_col, tk):
    s = pm.shape[0]
    nh = width // HEAD_DIM
    nb = s // tk
    kern = functools.partial(_foxprep_kernel, nh=nh, scale=LOG2E * float(HEAD_DIM) ** -0.5)
    qa, kta, r = pl.pallas_call(
        kern,
        out_shape=(jax.ShapeDtypeStruct((s, nh * FOX_KDIM), BF16),
                   jax.ShapeDtypeStruct((nb, nh * FOX_KDIM, tk), BF16),
                   jax.ShapeDtypeStruct((nb, 1, LANES), F32)),
        grid=(nb,),
        in_specs=[pl.BlockSpec((tk, width), lambda i: (i, q_col // width)),
                  pl.BlockSpec((tk, width), lambda i: (i, k_col // width)),
                  pl.BlockSpec((tk, LANES), lambda i: (i, 0)),
                  pl.BlockSpec((1, LANES), lambda i: (0, 0)),
                  pl.BlockSpec((1, width), lambda i: (0, 0)),
                  pl.BlockSpec((1, width), lambda i: (0, 0))],
        out_specs=(pl.BlockSpec((tk, nh * FOX_KDIM), lambda i: (i, 0)),
                   pl.BlockSpec((1, nh * FOX_KDIM, tk), lambda i: (i, 0, 0)),
                   pl.BlockSpec((1, 1, LANES), lambda i: (i, 0, 0))),
        scratch_shapes=[pltpu.VMEM((1, LANES), F32)],
        compiler_params=_cparams(("arbitrary",)),
        name="fox_prep",
    )(pm, pm, ff, fbias, gq.reshape(1, width), gk.reshape(1, width))
    return qa, kta, r.reshape(nb, LANES)


def _fox_kernel(r_ref, q_ref, kt_ref, v_ref, o_ref, v2_ref, s_ref, m_ref, acc_ref, *, tk):
    h = pl.program_id(0)
    qi = pl.program_id(1)

    @pl.when(qi == 0)
    def _():
        v2_ref[:, :LANES] = v_ref[...]
        lane = lax.broadcasted_iota(jnp.int32, v_ref.shape, 1)
        v2_ref[:, LANES:] = jnp.where(lane == 0, 1.0, 0.0).astype(v2_ref.dtype)

    m_ref[...] = jnp.full(m_ref.shape, -jnp.inf, F32)
    acc_ref[...] = jnp.zeros(acc_ref.shape, F32)
    rr = lax.broadcasted_iota(jnp.int32, (tk, tk), 0)
    cc = lax.broadcasted_iota(jnp.int32, (tk, tk), 1)

    def scores(c, j, slot):
        s_ref[slot, c] = jnp.dot(q_ref[c * tk:(c + 1) * tk, :], kt_ref[j], preferred_element_type=F32)

    def fold(c, j, slot, diag):
        s = s_ref[slot, c]
        if diag:
            s = jnp.where(cc <= rr, s, -jnp.inf)
        d = r_ref[2 * qi + c, h] - r_ref[j, h]
        m = m_ref[c]
        m_new = jnp.maximum(m, jnp.max(s, axis=-1, keepdims=True) + d)
        p = jnp.exp2(s - (m_new - d))
        r0 = pl.multiple_of(j * tk, tk)
        pv = jnp.dot(p.astype(BF16), v2_ref[pl.ds(r0, tk), :], preferred_element_type=F32)
        acc_ref[c] = jnp.exp2(m - m_new) * acc_ref[c] + pv
        m_ref[c] = m_new

    scores(0, 0, 0)
    scores(1, 0, 0)

    def pair(j):
        for slot in range(2):
            scores(0, j + slot + 1, 1 - slot)
            scores(1, j + slot + 1, 1 - slot)
            fold(0, j + slot, slot, False)
            fold(1, j + slot, slot, False)

    def body(jp, carry):
        pair(2 * jp)
        return carry

    lax.fori_loop(0, qi, body, 0)
    scores(1, 2 * qi + 1, 1)
    fold(0, 2 * qi, 0, True)
    fold(1, 2 * qi, 0, False)
    fold(1, 2 * qi + 1, 1, True)
    for c in range(2):
        acc = acc_ref[c]
        o_ref[c * tk:(c + 1) * tk, :] = (acc[:, :LANES] / acc[:, LANES:LANES + 1]).astype(o_ref.dtype)


def _fox(qa, kta, r, pm, v_col, tk):
    s = qa.shape[0]
    nh = qa.shape[1] // FOX_KDIM
    nb = s // tk
    tq = 2 * tk
    kern = functools.partial(_fox_kernel, tk=tk)
    return pl.pallas_call(
        kern,
        out_shape=jax.ShapeDtypeStruct((s, nh * HEAD_DIM), BF16),
        grid=(nh, s // tq),
        in_specs=[pl.BlockSpec(memory_space=pltpu.SMEM),
                  pl.BlockSpec((tq, FOX_KDIM), lambda h, i: (i, h)),
                  pl.BlockSpec((nb, FOX_KDIM, tk), lambda h, i: (0, h, 0)),
                  pl.BlockSpec((s, HEAD_DIM), lambda h, i: (0, v_col // HEAD_DIM + h))],
        out_specs=pl.BlockSpec((tq, HEAD_DIM), lambda h, i: (i, h)),
        scratch_shapes=[pltpu.VMEM((s, FOX_KDIM), BF16),
                        pltpu.VMEM((2, 2, tk, tk), F32),
                        pltpu.VMEM((2, tk, 1), F32),
                        pltpu.VMEM((2, tk, FOX_KDIM), F32)],
        compiler_params=_cparams(("parallel", "arbitrary")),
        name="fox_attn",
    )(r, qa, kta, pm)


def _merge_kernel(oa_ref, ob_ref, wa_ref, wb_ref, ga_ref, gb_ref, y_ref, wab_ref, wbb_ref):
    _cast_weight(wa_ref, wab_ref)
    _cast_weight(wb_ref, wbb_ref)
    ya = jnp.dot(oa_ref[...], wab_ref[...], preferred_element_type=F32)
    yb = jnp.dot(ob_ref[...], wbb_ref[...], preferred_element_type=F32)
    y = _sigmoid(ga_ref[...].astype(F32)) * ya + _sigmoid(gb_ref[...].astype(F32)) * yb
    y_ref[...] = y.astype(y_ref.dtype)


def _merge(oa, ob, wa, wb, pg):
    m, ka = oa.shape
    kb = ob.shape[1]
    n = wa.shape[1]
    tm = _tile(m, 1024, SUBLANES)
    tn = _tile(n, 512)
    return pl.pallas_call(
        _merge_kernel,
        out_shape=jax.ShapeDtypeStruct((m, n), BF16),
        grid=(n // tn, m // tm),
        in_specs=[pl.BlockSpec((tm, ka), lambda j, i: (i, 0)),
                  pl.BlockSpec((tm, kb), lambda j, i: (i, 0)),
                  pl.BlockSpec((ka, tn), lambda j, i: (0, j)),
                  pl.BlockSpec((kb, tn), lambda j, i: (0, j)),
                  pl.BlockSpec((tm, tn), lambda j, i: (i, j)),
                  pl.BlockSpec((tm, tn), lambda j, i: (i, n // tn + j))],
        out_specs=pl.BlockSpec((tm, tn), lambda j, i: (i, j)),
        scratch_shapes=[pltpu.VMEM((ka, tn), BF16), pltpu.VMEM((kb, tn), BF16)],
        compiler_params=_cparams(("parallel", "arbitrary")),
        name="branch_merge",
    )(oa, ob, wa, wb, pg, pg)


def _resproj_kernel(a_ref, w_ref, x_ref, gt_ref, o_ref, *scratch):
    if scratch:
        _cast_weight(w_ref, scratch[0])
        w_ref = scratch[0]
    y = jnp.dot(a_ref[...], w_ref[...], preferred_element_type=F32)
    o_ref[...] = x_ref[...] + gt_ref[...] * y


def _resproj(a, w, x, mod, gate_idx, tm_pref, tn_pref, name):
    m, k = a.shape
    n = w.shape[1]
    tm = _tile(m, tm_pref, SUBLANES)
    tn = _tile(n, tn_pref)
    cast = w.dtype != BF16
    ij = (lambda j, i: (i, j)) if cast else (lambda i, j: (i, j))
    im = lambda f: (lambda p, q: f(*ij(p, q)))
    return pl.pallas_call(
        _resproj_kernel,
        out_shape=jax.ShapeDtypeStruct((m, n), F32),
        grid=(n // tn, m // tm) if cast else (m // tm, n // tn),
        in_specs=[pl.BlockSpec((tm, k), im(lambda i, j: (i, 0))),
                  pl.BlockSpec((k, tn), im(lambda i, j: (0, j))),
                  pl.BlockSpec((tm, tn), im(lambda i, j: (i, j))),
                  pl.BlockSpec((1, tn), im(lambda i, j: (0, gate_idx * (n // tn) + j)))],
        out_specs=pl.BlockSpec((tm, tn), im(lambda i, j: (i, j))),
        scratch_shapes=[pltpu.VMEM((k, tn), BF16)] if cast else [],
        compiler_params=_cparams(("parallel", "arbitrary")),
        name=name,
    )(a, w, x, mod)


def _swiglu_kernel(h_ref, wg_ref, wu_ref, o_ref, wgb_ref, wub_ref):
    _cast_weight(wg_ref, wgb_ref)
    _cast_weight(wu_ref, wub_ref)
    h = h_ref[...]
    g = jnp.dot(h, wgb_ref[...], preferred_element_type=F32)
    u = jnp.dot(h, wub_ref[...], preferred_element_type=F32)
    o_ref[...] = (g * _sigmoid(g) * u).astype(o_ref.dtype)


def _swiglu(h, w):
    m, k = h.shape
    dff = w.shape[1] // 2
    tm = _tile(m, 1024, SUBLANES)
    tn = _tile(dff, 512)
    nj = dff // tn
    return pl.pallas_call(
        _swiglu_kernel,
        out_shape=jax.ShapeDtypeStruct((m, dff), BF16),
        grid=(nj, m // tm),
        in_specs=[pl.BlockSpec((tm, k), lambda j, i: (i, 0)),
                  pl.BlockSpec((k, tn), lambda j, i: (0, j)),
                  pl.BlockSpec((k, tn), lambda j, i: (0, nj + j))],
        out_specs=pl.BlockSpec((tm, tn), lambda j, i: (i, j)),
        scratch_shapes=[pltpu.VMEM((k, tn), BF16), pltpu.VMEM((k, tn), BF16)],
        compiler_params=_cparams(("parallel", "arbitrary")),
        name="ffn_swiglu",
    )(h, w, w)


def kernel(x, c, w_ada, b_ada, norm_mix_g, norm_ffn_g, w_in, fox_f_bias, hgrn_lb_table, hgrn_onorm_g,
           fox_q_norm_g, fox_k_norm_g, w_branch_a, w_branch_b, w_out, w_ffn_in, w_ffn_out):
    batch, seq, d = x.shape
    assert batch == 1 and w_ada.shape[0] == 1, "single sequence, single layer"
    hw = hgrn_onorm_g.shape[1] * hgrn_onorm_g.shape[2]
    fw = fox_q_norm_g.shape[1] * fox_q_norm_g.shape[2]
    nfh = fox_q_norm_g.shape[1]
    assert hgrn_onorm_g.shape[2] == HEAD_DIM and fox_q_norm_g.shape[2] == HEAD_DIM
    assert nfh <= LANES and hw == fw and seq % (2 * FOX_TK) == 0
    x2 = x.reshape(seq, d)

    f_col = 4 * hw + 3 * fw
    w_all = w_in[0]
    w_gates = _shiftcast(w_all, f_col, nfh, 2 * d)
    fbias = jnp.pad(fox_f_bias[0], (0, LANES - nfh)).reshape(1, LANES)

    mod = _ada_mod(c, w_ada[0], b_ada[0])

    h = _normmod(x2, norm_mix_g[0], mod, 0, 1)
    pq = _matmul(h, w_all, 0, hw, BF16, "in_proj_q")
    hf = _matmul(h, w_all, hw, hw, F32, "in_proj_f")
    pm = _matmul(h, w_all, 2 * hw, 2 * hw + 3 * fw, BF16, "in_proj_main")
    ff = _matmul(h, w_all, f_col, LANES, F32, "in_proj_ff")
    pg = _matmul(h, w_gates, 0, 2 * d, BF16, "in_proj_gates")

    o_a = _hgrn(pq, pm, hf, hgrn_lb_table, hgrn_onorm_g[0], hw, 0, hw)

    qa, kta, r = _foxprep(pm, ff, fbias, fox_q_norm_g[0], fox_k_norm_g[0], fw, 2 * hw, 2 * hw + fw, FOX_TK)
    o_b = _fox(qa, kta, r, pm, 2 * hw + 2 * fw, FOX_TK)

    y = _merge(o_a, o_b, w_branch_a[0], w_branch_b[0], pg)
    x1 = _resproj(y, w_out[0], x2, mod, 2, 1024, 512, "out_proj")

    h2 = _normmod(x1, norm_ffn_g[0], mod, 3, 4)
    act = _swiglu(h2, w_ffn_in[0])
    x3 = _resproj(act, w_ffn_out[0].astype(BF16), x1, mod, 5, 512, 512, "ffn_out")
    return x3.reshape(batch, seq, d)
```

```python
import functools
import math

import jax
import jax.numpy as jnp
from jax import lax
from jax.experimental import pallas as pl
from jax.experimental.pallas import tpu as pltpu

F32 = jnp.float32
BF16 = jnp.bfloat16

EPS = 1e-6
LOG2E = 1.4426950408889634
LANES = 128
SUBLANES = 8
HEAD_DIM = 128
HGRN_SUB = 16
HGRN_FSUB = 32
HGRN_UNROLL = 2
HGRN_MIN_LB = 2.0 ** -6
FOX_TK = 512
FOX_KDIM = 2 * HEAD_DIM
V7X_VMEM_LIMIT_BYTES = 56 * 1024 * 1024


def _cparams(semantics, vmem_bytes=V7X_VMEM_LIMIT_BYTES):
    return pltpu.CompilerParams(dimension_semantics=semantics, vmem_limit_bytes=vmem_bytes)


def _tile(dim, pref, align=LANES):
    if dim <= pref:
        return dim
    t = (pref // align) * align
    while t >= align:
        if dim % t == 0:
            return t
        t -= align
    raise ValueError(f"no {align}-aligned tile for {dim}")


def _sigmoid(x):
    return 1.0 / (1.0 + jnp.exp(-x))


def _ada_kernel(c_ref, w_ref, b_ref, o_ref):
    c = c_ref[...]
    s = c * _sigmoid(c)
    o_ref[...] = jnp.sum(w_ref[...] * s, axis=0, keepdims=True) + b_ref[...]


def _ada_mod(c, w_ada, b_ada):
    d, n = w_ada.shape
    tn = _tile(n, 512)
    return pl.pallas_call(
        _ada_kernel,
        out_shape=jax.ShapeDtypeStruct((1, n), F32),
        grid=(n // tn,),
        in_specs=[pl.BlockSpec((d, 1), lambda j: (0, 0)),
                  pl.BlockSpec((d, tn), lambda j: (0, j)),
                  pl.BlockSpec((1, tn), lambda j: (0, j))],
        out_specs=pl.BlockSpec((1, tn), lambda j: (0, j)),
        compiler_params=_cparams(("parallel",)),
        name="ada_mod",
    )(c.reshape(d, 1), w_ada, b_ada.reshape(1, n))


def _normmod_kernel(x_ref, g_ref, sh_ref, sc_ref, o_ref):
    x = x_ref[...]
    ms = jnp.mean(x * x, axis=-1, keepdims=True)
    xn = x * lax.rsqrt(ms + EPS)
    o_ref[...] = (xn * g_ref[...] * (1.0 + sc_ref[...]) + sh_ref[...]).astype(o_ref.dtype)


def _normmod(x, gain, mod, shift_idx, scale_idx):
    s, d = x.shape
    tm = _tile(s, 256, SUBLANES)
    return pl.pallas_call(
        _normmod_kernel,
        out_shape=jax.ShapeDtypeStruct((s, d), BF16),
        grid=(s // tm,),
        in_specs=[pl.BlockSpec((tm, d), lambda i: (i, 0)),
                  pl.BlockSpec((1, d), lambda i: (0, 0)),
                  pl.BlockSpec((1, d), lambda i: (0, shift_idx)),
                  pl.BlockSpec((1, d), lambda i: (0, scale_idx))],
        out_specs=pl.BlockSpec((tm, d), lambda i: (i, 0)),
        compiler_params=_cparams(("parallel",)),
        name="normmod",
    )(x, gain.reshape(1, d), mod, mod)


def _mm_kernel(x_ref, wt_ref, o_ref):
    y = lax.dot_general(x_ref[...], wt_ref[...], (((1,), (1,)), ((), ())), preferred_element_type=F32)
    o_ref[...] = y.astype(o_ref.dtype)


def _matmul(x, wt, row0, nrows, out_dtype, name):
    m, k = x.shape
    tm = _tile(m, 1024, SUBLANES)
    tn = _tile(math.gcd(nrows, row0) if row0 else nrows, 1024)
    return pl.pallas_call(
        _mm_kernel,
        out_shape=jax.ShapeDtypeStruct((m, nrows), out_dtype),
        grid=(m // tm, nrows // tn),
        in_specs=[pl.BlockSpec((tm, k), lambda i, j: (i, 0)),
                  pl.BlockSpec((tn, k), lambda i, j: (row0 // tn + j, 0))],
        out_specs=pl.BlockSpec((tm, tn), lambda i, j: (i, j)),
        compiler_params=_cparams(("parallel", "arbitrary")),
        name=name,
    )(x, wt)


def _hgrn_kernel(q_ref, i_ref, g_ref, f_ref, lbt_ref, og_ref, o_ref, st_ref, *, hb, tt):
    @pl.when(pl.program_id(1) == 0)
    def _():
        st_ref[...] = jnp.zeros_like(st_ref)

    tbl = lbt_ref[...]
    e = jnp.exp(tbl - jnp.max(tbl, axis=0, keepdims=True))
    lb = e[0:1, :] / jnp.sum(e, axis=0, keepdims=True)
    half = SUBLANES
    nt_dims = (((1,), (1,)), ((), ()))
    tn_dims = (((0,), (0,)), ((), ()))

    def gates(rows, cs, nrows):
        lbh = lb[:, cs]
        f = lbh + (1.0 - lbh) * _sigmoid(f_ref[rows, cs])
        b = jnp.log2(f)
        row = lax.broadcasted_iota(jnp.int32, (nrows, LANES), 0)
        sh = 1
        while sh < nrows:
            b = b + jnp.where(row >= sh, pltpu.roll(b, sh, axis=0), 0.0)
            sh *= 2
        return 1.0 - f, b

    def finish(o, rows, cs):
        ms = jnp.mean(o * o, axis=-1, keepdims=True)
        on = o * lax.rsqrt(ms + EPS) * og_ref[:, cs]
        gate = g_ref[rows, cs].astype(F32)
        o_ref[rows, cs] = (on * (gate * _sigmoid(gate))).astype(o_ref.dtype)

    mid = HGRN_FSUB // 2
    tri_r = lax.broadcasted_iota(jnp.int32, (HGRN_FSUB, HGRN_FSUB), 0)
    tri_c = lax.broadcasted_iota(jnp.int32, (HGRN_FSUB, HGRN_FSUB), 1)

    def body_factored(n, carry):
        pending = []
        for u, h in [(u, h) for u in range(HGRN_UNROLL) for h in range(hb)]:
            r0 = pl.multiple_of((n * HGRN_UNROLL + u) * HGRN_FSUB, HGRN_FSUB)
            rows = pl.ds(r0, HGRN_FSUB)
            cs = slice(h * LANES, (h + 1) * LANES)
            q = q_ref[rows, cs].astype(F32)
            v = i_ref[rows, cs]
            k, b = gates(rows, cs, HGRN_FSUB)
            r = b[mid - 1:mid, :]
            b_last = b[HGRN_FSUB - 1:HGRN_FSUB, :]
            qc = q * jnp.exp2(b - r)
            kc = (k * jnp.exp2(r - b)).astype(BF16)
            a = lax.dot_general(qc.astype(BF16), kc, nt_dims, preferred_element_type=F32)
            st = st_ref[h]
            o = lax.dot_general((qc * jnp.exp2(r)).astype(BF16), st.astype(BF16), nt_dims,
                                preferred_element_type=F32)
            kt = (k * jnp.exp2(b_last - b)).astype(BF16)
            upd = lax.dot_general(v, kt, tn_dims, preferred_element_type=F32)
            st_ref[h] = st * jnp.exp2(b_last) + upd
            pending.append((a, o, v, rows, cs))
        for a, o, v, rows, cs in pending:
            p = jnp.where(tri_c <= tri_r, a, 0.0).astype(BF16)
            finish(o + jnp.dot(p, v, preferred_element_type=F32), rows, cs)
        return carry

    rowc = lax.broadcasted_iota(jnp.int32, (SUBLANES, 1), 0)

    def body_exact(n, carry):
        r0 = pl.multiple_of(n * HGRN_SUB, HGRN_SUB)
        rows = pl.ds(r0, HGRN_SUB)
        for h in range(hb):
            cs = slice(h * LANES, (h + 1) * LANES)
            q = q_ref[rows, cs].astype(F32)
            v = i_ref[rows, cs]
            vf = v.astype(F32)
            k, b = gates(rows, cs, HGRN_SUB)
            b_last = b[HGRN_SUB - 1:HGRN_SUB, :]
            st = st_ref[h]
            qt = (q * jnp.exp2(b)).astype(BF16)
            o = lax.dot_general(qt, st.astype(BF16), nt_dims, preferred_element_type=F32)
            o_lo, o_hi = o[:half], o[half:]
            q_lo, q_hi = q[:half], q[half:]
            b_lo, b_hi = b[:half], b[half:]
            for s in range(HGRN_SUB):
                ks, bs, vs = k[s:s + 1, :], b[s:s + 1, :], vf[s:s + 1, :]
                c_hi = jnp.sum(q_hi * ks * jnp.exp2(b_hi - bs), axis=-1, keepdims=True)
                if s < half:
                    c_lo = jnp.sum(q_lo * ks * jnp.exp2(b_lo - bs), axis=-1, keepdims=True)
                    o_lo = o_lo + jnp.where(rowc >= s, c_lo, 0.0) * vs
                else:
                    c_hi = jnp.where(rowc >= s - half, c_hi, 0.0)
                o_hi = o_hi + c_hi * vs
            kt = (k * jnp.exp2(b_last - b)).astype(BF16)
            upd = lax.dot_general(v, kt, tn_dims, preferred_element_type=F32)
            st_ref[h] = st * jnp.exp2(b_last) + upd
            finish(jnp.concatenate([o_lo, o_hi], axis=0), rows, cs)
        return carry

    factorable = jnp.min(lb) >= HGRN_MIN_LB

    @pl.when(factorable)
    def _():
        lax.fori_loop(0, tt // (HGRN_FSUB * HGRN_UNROLL), body_factored, 0)

    @pl.when(jnp.logical_not(factorable))
    def _():
        lax.fori_loop(0, tt // HGRN_SUB, body_exact, 0)


def _hgrn(pq, pm, hf, lb_table, onorm_g, width, i_col, g_col):
    s = pq.shape[0]
    hb = min(16, width // HEAD_DIM)
    bw = hb * HEAD_DIM
    tt = _tile(s, 512, HGRN_SUB)
    nl = lb_table.shape[0]
    kern = functools.partial(_hgrn_kernel, hb=hb, tt=tt)
    return pl.pallas_call(
        kern,
        out_shape=jax.ShapeDtypeStruct((s, width), BF16),
        grid=(width // bw, s // tt),
        in_specs=[pl.BlockSpec((tt, bw), lambda h, t: (t, h)),
                  pl.BlockSpec((tt, bw), lambda h, t: (t, i_col // bw + h)),
                  pl.BlockSpec((tt, bw), lambda h, t: (t, g_col // bw + h)),
                  pl.BlockSpec((tt, bw), lambda h, t: (t, h)),
                  pl.BlockSpec((nl, bw), lambda h, t: (0, h)),
                  pl.BlockSpec((1, bw), lambda h, t: (0, h))],
        out_specs=pl.BlockSpec((tt, bw), lambda h, t: (t, h)),
        scratch_shapes=[pltpu.VMEM((hb, HEAD_DIM, HEAD_DIM), F32)],
        compiler_params=_cparams(("parallel", "arbitrary")),
        name="hgrn2",
    )(pq, pm, pm, hf, lb_table, onorm_g.reshape(1, width))


def _split3(x):
    hi = x.astype(BF16).astype(F32)
    r = x - hi
    mid = r.astype(BF16).astype(F32)
    lo = (r - mid).astype(BF16).astype(F32)
    return hi, mid, lo


def _foxprep_kernel(q_ref, k_ref, ff_ref, fb_ref, gq_ref, gk_ref, qo_ref, kto_ref, r_ref, carry_ref, *, nh, scale):
    @pl.when(pl.program_id(0) == 0)
    def _():
        carry_ref[...] = jnp.zeros_like(carry_ref)

    tk = q_ref.shape[0]
    x = ff_ref[...] + fb_ref[...]
    ls = (jnp.minimum(x, 0.0) - jnp.log(1.0 + jnp.exp(-jnp.abs(x)))) * LOG2E
    ri = lax.broadcasted_iota(jnp.int32, (tk, tk), 0)
    ci = lax.broadcasted_iota(jnp.int32, (tk, tk), 1)
    lower = (ci <= ri).astype(F32)
    rel = jnp.dot(lower, ls, precision=lax.Precision.HIGHEST, preferred_element_type=F32)
    base = carry_ref[...]
    r_ref[0] = base
    carry_ref[...] = base + rel[tk - 1:tk, :]

    cq = _split3(rel)
    ck = _split3(rel.T)
    lane = lax.broadcasted_iota(jnp.int32, (tk, LANES), 1)
    sub = lax.broadcasted_iota(jnp.int32, (LANES, tk), 0)
    ones_q = jnp.where((lane >= 3) & (lane < 6), 1.0, 0.0)
    ones_k = jnp.where(sub < 3, 1.0, 0.0)
    for h in range(nh):
        cs = slice(h * LANES, (h + 1) * LANES)
        q = q_ref[:, cs].astype(F32)
        qn = q * lax.rsqrt(jnp.mean(q * q, axis=-1, keepdims=True) + EPS) * gq_ref[:, cs]
        k = k_ref[:, cs].astype(F32)
        kn = k * lax.rsqrt(jnp.mean(k * k, axis=-1, keepdims=True) + EPS) * gk_ref[:, cs]
        eq = ones_q
        ek = ones_k
        for t in range(3):
            eq = jnp.where(lane == t, cq[t][:, h:h + 1], eq)
            ek = jnp.where(sub == 3 + t, -ck[t][h:h + 1, :], ek)
        c0 = h * FOX_KDIM
        qo_ref[:, c0:c0 + LANES] = (qn * scale).astype(qo_ref.dtype)
        qo_ref[:, c0 + LANES:c0 + FOX_KDIM] = eq.astype(qo_ref.dtype)
        kto_ref[0, c0:c0 + LANES, :] = kn.T.astype(kto_ref.dtype)
        kto_ref[0, c0 + LANES:c0 + FOX_KDIM, :] = ek.astype(kto_ref.dtype)


def _foxprep(pm, ff, fbias, gq, gk, width, q_col, k_col, tk):
    s = pm.shape[0]
    nh = width // HEAD_DIM
    nb = s // tk
    kern = functools.partial(_foxprep_kernel, nh=nh, scale=LOG2E * float(HEAD_DIM) ** -0.5)
    qa, kta, r = pl.pallas_call(
        kern,
        out_shape=(jax.ShapeDtypeStruct((s, nh * FOX_KDIM), BF16),
                   jax.ShapeDtypeStruct((nb, nh * FOX_KDIM, tk), BF16),
                   jax.ShapeDtypeStruct((nb, 1, LANES), F32)),
        grid=(nb,),
        in_specs=[pl.BlockSpec((tk, width), lambda i: (i, q_col // width)),
                  pl.BlockSpec((tk, width), lambda i: (i, k_col // width)),
                  pl.BlockSpec((tk, LANES), lambda i: (i, 0)),
                  pl.BlockSpec((1, LANES), lambda i: (0, 0)),
                  pl.BlockSpec((1, width), lambda i: (0, 0)),
                  pl.BlockSpec((1, width), lambda i: (0, 0))],
        out_specs=(pl.BlockSpec((tk, nh * FOX_KDIM), lambda i: (i, 0)),
                   pl.BlockSpec((1, nh * FOX_KDIM, tk), lambda i: (i, 0, 0)),
                   pl.BlockSpec((1, 1, LANES), lambda i: (i, 0, 0))),
        scratch_shapes=[pltpu.VMEM((1, LANES), F32)],
        compiler_params=_cparams(("arbitrary",)),
        name="fox_prep",
    )(pm, pm, ff, fbias, gq.reshape(1, width), gk.reshape(1, width))
    return qa, kta, r.reshape(nb, LANES)


def _fox_kernel(r_ref, q_ref, kt_ref, v_ref, o_ref, v2_ref, s_ref, m_ref, acc_ref, *, tk):
    h = pl.program_id(0)
    qi = pl.program_id(1)

    @pl.when(qi == 0)
    def _():
        v2_ref[:, :LANES] = v_ref[...]
        lane = lax.broadcasted_iota(jnp.int32, v_ref.shape, 1)
        v2_ref[:, LANES:] = jnp.where(lane == 0, 1.0, 0.0).astype(v2_ref.dtype)

    m_ref[...] = jnp.full(m_ref.shape, -jnp.inf, F32)
    acc_ref[...] = jnp.zeros(acc_ref.shape, F32)
    rr = lax.broadcasted_iota(jnp.int32, (tk, tk), 0)
    cc = lax.broadcasted_iota(jnp.int32, (tk, tk), 1)

    def scores(c, j, slot):
        s_ref[slot, c] = jnp.dot(q_ref[c * tk:(c + 1) * tk, :], kt_ref[j], preferred_element_type=F32)

    def fold(c, j, slot, diag):
        s = s_ref[slot, c]
        if diag:
            s = jnp.where(cc <= rr, s, -jnp.inf)
        d = r_ref[2 * qi + c, h] - r_ref[j, h]
        m = m_ref[c]
        m_new = jnp.maximum(m, jnp.max(s, axis=-1, keepdims=True) + d)
        p = jnp.exp2(s - (m_new - d))
        r0 = pl.multiple_of(j * tk, tk)
        pv = jnp.dot(p.astype(BF16), v2_ref[pl.ds(r0, tk), :], preferred_element_type=F32)
        acc_ref[c] = jnp.exp2(m - m_new) * acc_ref[c] + pv
        m_ref[c] = m_new

    scores(0, 0, 0)
    scores(1, 0, 0)

    def pair(j):
        for slot in range(2):
            scores(0, j + slot + 1, 1 - slot)
            scores(1, j + slot + 1, 1 - slot)
            fold(0, j + slot, slot, False)
            fold(1, j + slot, slot, False)

    def body(jp, carry):
        pair(2 * jp)
        return carry

    lax.fori_loop(0, qi, body, 0)
    scores(1, 2 * qi + 1, 1)
    fold(0, 2 * qi, 0, True)
    fold(1, 2 * qi, 0, False)
    fold(1, 2 * qi + 1, 1, True)
    for c in range(2):
        acc = acc_ref[c]
        o_ref[c * tk:(c + 1) * tk, :] = (acc[:, :LANES] / acc[:, LANES:LANES + 1]).astype(o_ref.dtype)


def _fox(qa, kta, r, pm, v_col, tk):
    s = qa.shape[0]
    nh = qa.shape[1] // FOX_KDIM
    nb = s // tk
    tq = 2 * tk
    kern = functools.partial(_fox_kernel, tk=tk)
    return pl.pallas_call(
        kern,
        out_shape=jax.ShapeDtypeStruct((s, nh * HEAD_DIM), BF16),
        grid=(nh, s // tq),
        in_specs=[pl.BlockSpec(memory_space=pltpu.SMEM),
                  pl.BlockSpec((tq, FOX_KDIM), lambda h, i: (i, h)),
                  pl.BlockSpec((nb, FOX_KDIM, tk), lambda h, i: (0, h, 0)),
                  pl.BlockSpec((s, HEAD_DIM), lambda h, i: (0, v_col // HEAD_DIM + h))],
        out_specs=pl.BlockSpec((tq, HEAD_DIM), lambda h, i: (i, h)),
        scratch_shapes=[pltpu.VMEM((s, FOX_KDIM), BF16),
                        pltpu.VMEM((2, 2, tk, tk), F32),
                        pltpu.VMEM((2, tk, 1), F32),
                        pltpu.VMEM((2, tk, FOX_KDIM), F32)],
        compiler_params=_cparams(("parallel", "arbitrary")),
        name="fox_attn",
    )(r, qa, kta, pm)


def _merge_kernel(oa_ref, ob_ref, wa_ref, wb_ref, ga_ref, gb_ref, y_ref):
    sa = _sigmoid(ga_ref[...].astype(F32))
    sb = _sigmoid(gb_ref[...].astype(F32))
    ya = jnp.dot(oa_ref[...], wa_ref[...], preferred_element_type=F32)
    yb = jnp.dot(ob_ref[...], wb_ref[...], preferred_element_type=F32)
    y_ref[...] = (sa * ya + sb * yb).astype(y_ref.dtype)


def _merge(oa, ob, wa, wb, pg):
    m, ka = oa.shape
    kb = ob.shape[1]
    n = wa.shape[1]
    tm = _tile(m, 1024, SUBLANES)
    tn = _tile(n, 512)
    return pl.pallas_call(
        _merge_kernel,
        out_shape=jax.ShapeDtypeStruct((m, n), BF16),
        grid=(m // tm, n // tn),
        in_specs=[pl.BlockSpec((tm, ka), lambda i, j: (i, 0)),
                  pl.BlockSpec((tm, kb), lambda i, j: (i, 0)),
                  pl.BlockSpec((ka, tn), lambda i, j: (0, j)),
                  pl.BlockSpec((kb, tn), lambda i, j: (0, j)),
                  pl.BlockSpec((tm, tn), lambda i, j: (i, j)),
                  pl.BlockSpec((tm, tn), lambda i, j: (i, n // tn + j))],
        out_specs=pl.BlockSpec((tm, tn), lambda i, j: (i, j)),
        compiler_params=_cparams(("parallel", "arbitrary")),
        name="branch_merge",
    )(oa, ob, wa, wb, pg, pg)


def _resproj_kernel(a_ref, w_ref, x_ref, gt_ref, o_ref):
    y = jnp.dot(a_ref[...], w_ref[...], preferred_element_type=F32)
    o_ref[...] = x_ref[...] + gt_ref[...] * y


def _resproj(a, w, x, mod, gate_idx, tm_pref, tn_pref, name):
    m, k = a.shape
    n = w.shape[1]
    tm = _tile(m, tm_pref, SUBLANES)
    tn = _tile(n, tn_pref)
    return pl.pallas_call(
        _resproj_kernel,
        out_shape=jax.ShapeDtypeStruct((m, n), F32),
        grid=(m // tm, n // tn),
        in_specs=[pl.BlockSpec((tm, k), lambda i, j: (i, 0)),
                  pl.BlockSpec((k, tn), lambda i, j: (0, j)),
                  pl.BlockSpec((tm, tn), lambda i, j: (i, j)),
                  pl.BlockSpec((1, tn), lambda i, j: (0, gate_idx * (n // tn) + j))],
        out_specs=pl.BlockSpec((tm, tn), lambda i, j: (i, j)),
        compiler_params=_cparams(("parallel", "arbitrary")),
        name=name,
    )(a, w, x, mod)


def _swiglu_kernel(h_ref, wg_ref, wu_ref, o_ref):
    h = h_ref[...]
    g = jnp.dot(h, wg_ref[...], preferred_element_type=F32)
    u = jnp.dot(h, wu_ref[...], preferred_element_type=F32)
    o_ref[...] = (g * _sigmoid(g) * u).astype(o_ref.dtype)


def _swiglu(h, w):
    m, k = h.shape
    dff = w.shape[1] // 2
    tm = _tile(m, 2048, SUBLANES)
    tn = _tile(dff, 512)
    nj = dff // tn
    return pl.pallas_call(
        _swiglu_kernel,
        out_shape=jax.ShapeDtypeStruct((m, dff), BF16),
        grid=(m // tm, nj),
        in_specs=[pl.BlockSpec((tm, k), lambda i, j: (i, 0)),
                  pl.BlockSpec((k, tn), lambda i, j: (0, j)),
                  pl.BlockSpec((k, tn), lambda i, j: (0, nj + j))],
        out_specs=pl.BlockSpec((tm, tn), lambda i, j: (i, j)),
        compiler_params=_cparams(("parallel", "arbitrary")),
        name="ffn_swiglu",
    )(h, w, w)


def kernel(x, c, w_ada, b_ada, norm_mix_g, norm_ffn_g, w_in, fox_f_bias, hgrn_lb_table, hgrn_onorm_g,
           fox_q_norm_g, fox_k_norm_g, w_branch_a, w_branch_b, w_out, w_ffn_in, w_ffn_out):
    batch, seq, d = x.shape
    assert batch == 1 and w_ada.shape[0] == 1, "single sequence, single layer"
    hw = hgrn_onorm_g.shape[1] * hgrn_onorm_g.shape[2]
    fw = fox_q_norm_g.shape[1] * fox_q_norm_g.shape[2]
    nfh = fox_q_norm_g.shape[1]
    assert hgrn_onorm_g.shape[2] == HEAD_DIM and fox_q_norm_g.shape[2] == HEAD_DIM
    assert nfh <= LANES and hw == fw and seq % (2 * FOX_TK) == 0
    x2 = x.reshape(seq, d)

    f_col = 4 * hw + 3 * fw
    wt = jnp.swapaxes(w_in, 1, 2)[0]
    w_bf = wt[:f_col + LANES].astype(BF16)
    w_gates = wt[f_col + nfh:].astype(BF16)
    fbias = jnp.pad(fox_f_bias[0], (0, LANES - nfh)).reshape(1, LANES)

    mod = _ada_mod(c, w_ada[0], b_ada[0])

    h = _normmod(x2, norm_mix_g[0], mod, 0, 1)
    pq = _matmul(h, w_bf, 0, hw, BF16, "in_proj_q")
    hf = _matmul(h, w_bf, hw, hw, F32, "in_proj_f")
    pm = _matmul(h, w_bf, 2 * hw, 2 * hw + 3 * fw, BF16, "in_proj_main")
    ff = _matmul(h, w_bf, f_col, LANES, F32, "in_proj_ff")
    pg = _matmul(h, w_gates, 0, 2 * d, BF16, "in_proj_gates")

    o_a = _hgrn(pq, pm, hf, hgrn_lb_table, hgrn_onorm_g[0], hw, 0, hw)

    qa, kta, r = _foxprep(pm, ff, fbias, fox_q_norm_g[0], fox_k_norm_g[0], fw, 2 * hw, 2 * hw + fw, FOX_TK)
    o_b = _fox(qa, kta, r, pm, 2 * hw + 2 * fw, FOX_TK)

    y = _merge(o_a, o_b, w_branch_a[0].astype(BF16), w_branch_b[0].astype(BF16), pg)
    x1 = _resproj(y, w_out[0].astype(BF16), x2, mod, 2, 1024, 512, "out_proj")

    h2 = _normmod(x1, norm_ffn_g[0], mod, 3, 4)
    act = _swiglu(h2, w_ffn_in[0].astype(BF16))
    x3 = _resproj(act, w_ffn_out[0].astype(BF16), x1, mod, 5, 512, 512, "ffn_out")
    return x3.reshape(batch, seq, d)
```

```python
import functools
import math

import jax
import jax.numpy as jnp
from jax import lax
from jax.experimental import pallas as pl
from jax.experimental.pallas import tpu as pltpu

F32 = jnp.float32
BF16 = jnp.bfloat16

EPS = 1e-6
LOG2E = 1.4426950408889634
LANES = 128
SUBLANES = 8
BF16_SUBLANES = 16
HEAD_DIM = 128
HGRN_SUB = 16
HGRN_FSUB = 32
HGRN_UNROLL = 4
HGRN_MIN_LB = 2.0 ** -6
FOX_TK = 512
FOX_KDIM = 2 * HEAD_DIM
V7X_VMEM_LIMIT_BYTES = 56 * 1024 * 1024


def _cparams(semantics, vmem_bytes=V7X_VMEM_LIMIT_BYTES):
    return pltpu.CompilerParams(dimension_semantics=semantics, vmem_limit_bytes=vmem_bytes)


def _tile(dim, pref, align=LANES):
    if dim <= pref:
        return dim
    t = (pref // align) * align
    while t >= align:
        if dim % t == 0:
            return t
        t -= align
    raise ValueError(f"no {align}-aligned tile for {dim}")


def _sigmoid(x):
    return 1.0 / (1.0 + jnp.exp(-x))


def _ada_kernel(c_ref, w_ref, b_ref, o_ref):
    c = c_ref[...]
    s = c * _sigmoid(c)
    o_ref[...] = jnp.sum(w_ref[...] * s, axis=0, keepdims=True) + b_ref[...]


def _ada_mod(c, w_ada, b_ada):
    d, n = w_ada.shape
    tn = _tile(n, 512)
    return pl.pallas_call(
        _ada_kernel,
        out_shape=jax.ShapeDtypeStruct((1, n), F32),
        grid=(n // tn,),
        in_specs=[pl.BlockSpec((d, 1), lambda j: (0, 0)),
                  pl.BlockSpec((d, tn), lambda j: (0, j)),
                  pl.BlockSpec((1, tn), lambda j: (0, j))],
        out_specs=pl.BlockSpec((1, tn), lambda j: (0, j)),
        compiler_params=_cparams(("parallel",)),
        name="ada_mod",
    )(c.reshape(d, 1), w_ada, b_ada.reshape(1, n))


def _normmod_kernel(x_ref, g_ref, sh_ref, sc_ref, o_ref):
    x = x_ref[...]
    ms = jnp.mean(x * x, axis=-1, keepdims=True)
    xn = x * lax.rsqrt(ms + EPS)
    o_ref[...] = (xn * g_ref[...] * (1.0 + sc_ref[...]) + sh_ref[...]).astype(o_ref.dtype)


def _normmod(x, gain, mod, shift_idx, scale_idx):
    s, d = x.shape
    tm = _tile(s, 256, SUBLANES)
    return pl.pallas_call(
        _normmod_kernel,
        out_shape=jax.ShapeDtypeStruct((s, d), BF16),
        grid=(s // tm,),
        in_specs=[pl.BlockSpec((tm, d), lambda i: (i, 0)),
                  pl.BlockSpec((1, d), lambda i: (0, 0)),
                  pl.BlockSpec((1, d), lambda i: (0, shift_idx)),
                  pl.BlockSpec((1, d), lambda i: (0, scale_idx))],
        out_specs=pl.BlockSpec((tm, d), lambda i: (i, 0)),
        compiler_params=_cparams(("parallel",)),
        name="normmod",
    )(x, gain.reshape(1, d), mod, mod)


def _mm_kernel(x_ref, wt_ref, o_ref):
    y = lax.dot_general(x_ref[...], wt_ref[...], (((1,), (1,)), ((), ())), preferred_element_type=F32)
    o_ref[...] = y.astype(o_ref.dtype)


def _matmul(x, wt, row0, nrows, out_dtype, name):
    m, k = x.shape
    assert row0 % BF16_SUBLANES == 0
    tm = _tile(m, 1024, SUBLANES)
    tn = _tile(nrows, 1024)
    return pl.pallas_call(
        _mm_kernel,
        out_shape=jax.ShapeDtypeStruct((m, nrows), out_dtype),
        grid=(m // tm, nrows // tn),
        in_specs=[pl.BlockSpec((tm, k), lambda i, j: (i, 0)),
                  pl.BlockSpec((pl.Element(tn), pl.Element(k)),
                               lambda i, j: (pl.multiple_of(row0 + j * tn, BF16_SUBLANES), 0))],
        out_specs=pl.BlockSpec((tm, tn), lambda i, j: (i, j)),
        compiler_params=_cparams(("parallel", "arbitrary")),
        name=name,
    )(x, wt)


def _hgrn_kernel(q_ref, i_ref, g_ref, f_ref, lbt_ref, og_ref, o_ref, st_ref, *, hb, tt):
    @pl.when(pl.program_id(1) == 0)
    def _():
        st_ref[...] = jnp.zeros_like(st_ref)

    tbl = lbt_ref[...]
    e = jnp.exp(tbl - jnp.max(tbl, axis=0, keepdims=True))
    lb = e[0:1, :] / jnp.sum(e, axis=0, keepdims=True)
    half = SUBLANES
    nt_dims = (((1,), (1,)), ((), ()))
    tn_dims = (((0,), (0,)), ((), ()))

    def gates(rows, cs, nrows):
        lbh = lb[:, cs]
        f = lbh + (1.0 - lbh) * _sigmoid(f_ref[rows, cs])
        b = jnp.log2(f)
        row = lax.broadcasted_iota(jnp.int32, (nrows, LANES), 0)
        sh = 1
        while sh < nrows:
            b = b + jnp.where(row >= sh, pltpu.roll(b, sh, axis=0), 0.0)
            sh *= 2
        return 1.0 - f, b

    def finish(o, rows, cs):
        ms = jnp.mean(o * o, axis=-1, keepdims=True)
        on = o * lax.rsqrt(ms + EPS) * og_ref[:, cs]
        gate = g_ref[rows, cs].astype(F32)
        o_ref[rows, cs] = (on * (gate * _sigmoid(gate))).astype(o_ref.dtype)

    mid = HGRN_FSUB // 2
    tri_r = lax.broadcasted_iota(jnp.int32, (HGRN_FSUB, HGRN_FSUB), 0)
    tri_c = lax.broadcasted_iota(jnp.int32, (HGRN_FSUB, HGRN_FSUB), 1)

    def body_factored(n, carry):
        pending = []
        for u, h in [(u, h) for u in range(HGRN_UNROLL) for h in range(hb)]:
            r0 = pl.multiple_of((n * HGRN_UNROLL + u) * HGRN_FSUB, HGRN_FSUB)
            rows = pl.ds(r0, HGRN_FSUB)
            cs = slice(h * LANES, (h + 1) * LANES)
            q = q_ref[rows, cs].astype(F32)
            v = i_ref[rows, cs]
            k, b = gates(rows, cs, HGRN_FSUB)
            r = b[mid - 1:mid, :]
            b_last = b[HGRN_FSUB - 1:HGRN_FSUB, :]
            qc = q * jnp.exp2(b - r)
            kc = (k * jnp.exp2(r - b)).astype(BF16)
            a = lax.dot_general(qc.astype(BF16), kc, nt_dims, preferred_element_type=F32)
            st = st_ref[h]
            o = lax.dot_general((qc * jnp.exp2(r)).astype(BF16), st.astype(BF16), nt_dims,
                                preferred_element_type=F32)
            kt = (k * jnp.exp2(b_last - b)).astype(BF16)
            upd = lax.dot_general(v, kt, tn_dims, preferred_element_type=F32)
            st_ref[h] = st * jnp.exp2(b_last) + upd
            pending.append((a, o, v, rows, cs))
        for a, o, v, rows, cs in pending:
            p = jnp.where(tri_c <= tri_r, a, 0.0).astype(BF16)
            finish(o + jnp.dot(p, v, preferred_element_type=F32), rows, cs)
        return carry

    rowc = lax.broadcasted_iota(jnp.int32, (SUBLANES, 1), 0)

    def body_exact(n, carry):
        r0 = pl.multiple_of(n * HGRN_SUB, HGRN_SUB)
        rows = pl.ds(r0, HGRN_SUB)
        for h in range(hb):
            cs = slice(h * LANES, (h + 1) * LANES)
            q = q_ref[rows, cs].astype(F32)
            v = i_ref[rows, cs]
            vf = v.astype(F32)
            k, b = gates(rows, cs, HGRN_SUB)
            b_last = b[HGRN_SUB - 1:HGRN_SUB, :]
            st = st_ref[h]
            qt = (q * jnp.exp2(b)).astype(BF16)
            o = lax.dot_general(qt, st.astype(BF16), nt_dims, preferred_element_type=F32)
            o_lo, o_hi = o[:half], o[half:]
            q_lo, q_hi = q[:half], q[half:]
            b_lo, b_hi = b[:half], b[half:]
            for s in range(HGRN_SUB):
                ks, bs, vs = k[s:s + 1, :], b[s:s + 1, :], vf[s:s + 1, :]
                c_hi = jnp.sum(q_hi * ks * jnp.exp2(b_hi - bs), axis=-1, keepdims=True)
                if s < half:
                    c_lo = jnp.sum(q_lo * ks * jnp.exp2(b_lo - bs), axis=-1, keepdims=True)
                    o_lo = o_lo + jnp.where(rowc >= s, c_lo, 0.0) * vs
                else:
                    c_hi = jnp.where(rowc >= s - half, c_hi, 0.0)
                o_hi = o_hi + c_hi * vs
            kt = (k * jnp.exp2(b_last - b)).astype(BF16)
            upd = lax.dot_general(v, kt, tn_dims, preferred_element_type=F32)
            st_ref[h] = st * jnp.exp2(b_last) + upd
            finish(jnp.concatenate([o_lo, o_hi], axis=0), rows, cs)
        return carry

    factorable = jnp.min(lb) >= HGRN_MIN_LB

    @pl.when(factorable)
    def _():
        lax.fori_loop(0, tt // (HGRN_FSUB * HGRN_UNROLL), body_factored, 0)

    @pl.when(jnp.logical_not(factorable))
    def _():
        lax.fori_loop(0, tt // HGRN_SUB, body_exact, 0)


def _hgrn(pq, pm, hf, lb_table, onorm_g, width, i_col, g_col):
    s = pq.shape[0]
    hb = min(16, width // HEAD_DIM)
    bw = hb * HEAD_DIM
    tt = _tile(s, 512, HGRN_SUB)
    nl = lb_table.shape[0]
    kern = functools.partial(_hgrn_kernel, hb=hb, tt=tt)
    return pl.pallas_call(
        kern,
        out_shape=jax.ShapeDtypeStruct((s, width), BF16),
        grid=(width // bw, s // tt),
        in_specs=[pl.BlockSpec((tt, bw), lambda h, t: (t, h)),
                  pl.BlockSpec((tt, bw), lambda h, t: (t, i_col // bw + h)),
                  pl.BlockSpec((tt, bw), lambda h, t: (t, g_col // bw + h)),
                  pl.BlockSpec((tt, bw), lambda h, t: (t, h)),
                  pl.BlockSpec((nl, bw), lambda h, t: (0, h)),
                  pl.BlockSpec((1, bw), lambda h, t: (0, h))],
        out_specs=pl.BlockSpec((tt, bw), lambda h, t: (t, h)),
        scratch_shapes=[pltpu.VMEM((hb, HEAD_DIM, HEAD_DIM), F32)],
        compiler_params=_cparams(("parallel", "arbitrary")),
        name="hgrn2",
    )(pq, pm, pm, hf, lb_table, onorm_g.reshape(1, width))


def _split3(x):
    hi = x.astype(BF16).astype(F32)
    r = x - hi
    mid = r.astype(BF16).astype(F32)
    lo = (r - mid).astype(BF16).astype(F32)
    return hi, mid, lo


def _foxprep_kernel(q_ref, k_ref, ff_ref, fb_ref, gq_ref, gk_ref, qo_ref, kto_ref, r_ref, carry_ref, *, nh, scale):
    @pl.when(pl.program_id(0) == 0)
    def _():
        carry_ref[...] = jnp.zeros_like(carry_ref)

    tk = q_ref.shape[0]
    x = ff_ref[...] + fb_ref[...]
    ls = (jnp.minimum(x, 0.0) - jnp.log(1.0 + jnp.exp(-jnp.abs(x)))) * LOG2E
    ri = lax.broadcasted_iota(jnp.int32, (tk, tk), 0)
    ci = lax.broadcasted_iota(jnp.int32, (tk, tk), 1)
    lower = (ci <= ri).astype(F32)
    rel = jnp.dot(lower, ls, precision=lax.Precision.HIGHEST, preferred_element_type=F32)
    base = carry_ref[...]
    r_ref[0] = base
    carry_ref[...] = base + rel[tk - 1:tk, :]

    cq = _split3(rel)
    ck = _split3(rel.T)
    lane = lax.broadcasted_iota(jnp.int32, (tk, LANES), 1)
    sub = lax.broadcasted_iota(jnp.int32, (LANES, tk), 0)
    ones_q = jnp.where((lane >= 3) & (lane < 6), 1.0, 0.0)
    ones_k = jnp.where(sub < 3, 1.0, 0.0)
    for h in range(nh):
        cs = slice(h * LANES, (h + 1) * LANES)
        q = q_ref[:, cs].astype(F32)
        qn = q * lax.rsqrt(jnp.mean(q * q, axis=-1, keepdims=True) + EPS) * gq_ref[:, cs]
        k = k_ref[:, cs].astype(F32)
        kn = k * lax.rsqrt(jnp.mean(k * k, axis=-1, keepdims=True) + EPS) * gk_ref[:, cs]
        eq = ones_q
        ek = ones_k
        for t in range(3):
            eq = jnp.where(lane == t, cq[t][:, h:h + 1], eq)
            ek = jnp.where(sub == 3 + t, -ck[t][h:h + 1, :], ek)
        c0 = h * FOX_KDIM
        qo_ref[:, c0:c0 + LANES] = (qn * scale).astype(qo_ref.dtype)
        qo_ref[:, c0 + LANES:c0 + FOX_KDIM] = eq.astype(qo_ref.dtype)
        kto_ref[0, c0:c0 + LANES, :] = kn.T.astype(kto_ref.dtype)
        kto_ref[0, c0 + LANES:c0 + FOX_KDIM, :] = ek.astype(kto_ref.dtype)


def _foxprep(pm, ff, fbias, gq, gk, width, q_col, k_col, tk):
    s = pm.shape[0]
    nh = width // HEAD_DIM
    nb = s // tk
    kern = functools.partial(_foxprep_kernel, nh=nh, scale=LOG2E * float(HEAD_DIM) ** -0.5)
    qa, kta, r = pl.pallas_call(
        kern,
        out_shape=(jax.ShapeDtypeStruct((s, nh * FOX_KDIM), BF16),
                   jax.ShapeDtypeStruct((nb, nh * FOX_KDIM, tk), BF16),
                   jax.ShapeDtypeStruct((nb, 1, LANES), F32)),
        grid=(nb,),
        in_specs=[pl.BlockSpec((tk, width), lambda i: (i, q_col // width)),
                  pl.BlockSpec((tk, width), lambda i: (i, k_col // width)),
                  pl.BlockSpec((tk, LANES), lambda i: (i, 0)),
                  pl.BlockSpec((1, LANES), lambda i: (0, 0)),
                  pl.BlockSpec((1, width), lambda i: (0, 0)),
                  pl.BlockSpec((1, width), lambda i: (0, 0))],
        out_specs=(pl.BlockSpec((tk, nh * FOX_KDIM), lambda i: (i, 0)),
                   pl.BlockSpec((1, nh * FOX_KDIM, tk), lambda i: (i, 0, 0)),
                   pl.BlockSpec((1, 1, LANES), lambda i: (i, 0, 0))),
        scratch_shapes=[pltpu.VMEM((1, LANES), F32)],
        compiler_params=_cparams(("arbitrary",)),
        name="fox_prep",
    )(pm, pm, ff, fbias, gq.reshape(1, width), gk.reshape(1, width))
    return qa, kta, r.reshape(nb, LANES)


def _fox_kernel(r_ref, q_ref, kt_ref, v_ref, o_ref, v2_ref, s_ref, m_ref, acc_ref, *, tk):
    h = pl.program_id(0)
    qi = pl.program_id(1)

    @pl.when(qi == 0)
    def _():
        v2_ref[:, :LANES] = v_ref[...]
        lane = lax.broadcasted_iota(jnp.int32, v_ref.shape, 1)
        v2_ref[:, LANES:] = jnp.where(lane == 0, 1.0, 0.0).astype(v2_ref.dtype)

    m_ref[...] = jnp.full(m_ref.shape, -jnp.inf, F32)
    acc_ref[...] = jnp.zeros(acc_ref.shape, F32)
    rr = lax.broadcasted_iota(jnp.int32, (tk, tk), 0)
    cc = lax.broadcasted_iota(jnp.int32, (tk, tk), 1)

    def scores(c, j, slot):
        s_ref[slot, c] = jnp.dot(q_ref[c * tk:(c + 1) * tk, :], kt_ref[j], preferred_element_type=F32)

    def fold(c, j, slot, diag):
        s = s_ref[slot, c]
        if diag:
            s = jnp.where(cc <= rr, s, -jnp.inf)
        d = r_ref[2 * qi + c, h] - r_ref[j, h]
        m = m_ref[c]
        m_new = jnp.maximum(m, jnp.max(s, axis=-1, keepdims=True) + d)
        p = jnp.exp2(s - (m_new - d))
        r0 = pl.multiple_of(j * tk, tk)
        pv = jnp.dot(p.astype(BF16), v2_ref[pl.ds(r0, tk), :], preferred_element_type=F32)
        acc_ref[c] = jnp.exp2(m - m_new) * acc_ref[c] + pv
        m_ref[c] = m_new

    scores(0, 0, 0)
    scores(1, 0, 0)

    def pair(j):
        for slot in range(2):
            scores(0, j + slot + 1, 1 - slot)
            scores(1, j + slot + 1, 1 - slot)
            fold(0, j + slot, slot, False)
            fold(1, j + slot, slot, False)

    def body(jp, carry):
        pair(2 * jp)
        return carry

    lax.fori_loop(0, qi, body, 0)
    scores(1, 2 * qi + 1, 1)
    fold(0, 2 * qi, 0, True)
    fold(1, 2 * qi, 0, False)
    fold(1, 2 * qi + 1, 1, True)
    for c in range(2):
        acc = acc_ref[c]
        o_ref[c * tk:(c + 1) * tk, :] = (acc[:, :LANES] / acc[:, LANES:LANES + 1]).astype(o_ref.dtype)


def _fox(qa, kta, r, pm, v_col, tk):
    s = qa.shape[0]
    nh = qa.shape[1] // FOX_KDIM
    nb = s // tk
    tq = 2 * tk
    kern = functools.partial(_fox_kernel, tk=tk)
    return pl.pallas_call(
        kern,
        out_shape=jax.ShapeDtypeStruct((s, nh * HEAD_DIM), BF16),
        grid=(nh, s // tq),
        in_specs=[pl.BlockSpec(memory_space=pltpu.SMEM),
                  pl.BlockSpec((tq, FOX_KDIM), lambda h, i: (i, h)),
                  pl.BlockSpec((nb, FOX_KDIM, tk), lambda h, i: (0, h, 0)),
                  pl.BlockSpec((s, HEAD_DIM), lambda h, i: (0, v_col // HEAD_DIM + h))],
        out_specs=pl.BlockSpec((tq, HEAD_DIM), lambda h, i: (i, h)),
        scratch_shapes=[pltpu.VMEM((s, FOX_KDIM), BF16),
                        pltpu.VMEM((2, 2, tk, tk), F32),
                        pltpu.VMEM((2, tk, 1), F32),
                        pltpu.VMEM((2, tk, FOX_KDIM), F32)],
        compiler_params=_cparams(("parallel", "arbitrary")),
        name="fox_attn",
    )(r, qa, kta, pm)


def _merge_kernel(oa_ref, ob_ref, wa_ref, wb_ref, ga_ref, gb_ref, y_ref):
    sa = _sigmoid(ga_ref[...].astype(F32))
    sb = _sigmoid(gb_ref[...].astype(F32))
    ya = jnp.dot(oa_ref[...], wa_ref[...], preferred_element_type=F32)
    yb = jnp.dot(ob_ref[...], wb_ref[...], preferred_element_type=F32)
    y_ref[...] = (sa * ya + sb * yb).astype(y_ref.dtype)


def _merge(oa, ob, wa, wb, pg):
    m, ka = oa.shape
    kb = ob.shape[1]
    n = wa.shape[1]
    tm = _tile(m, 1024, SUBLANES)
    tn = _tile(n, 512)
    return pl.pallas_call(
        _merge_kernel,
        out_shape=jax.ShapeDtypeStruct((m, n), BF16),
        grid=(m // tm, n // tn),
        in_specs=[pl.BlockSpec((tm, ka), lambda i, j: (i, 0)),
                  pl.BlockSpec((tm, kb), lambda i, j: (i, 0)),
                  pl.BlockSpec((ka, tn), lambda i, j: (0, j)),
                  pl.BlockSpec((kb, tn), lambda i, j: (0, j)),
                  pl.BlockSpec((tm, tn), lambda i, j: (i, j)),
                  pl.BlockSpec((tm, tn), lambda i, j: (i, n // tn + j))],
        out_specs=pl.BlockSpec((tm, tn), lambda i, j: (i, j)),
        compiler_params=_cparams(("parallel", "arbitrary")),
        name="branch_merge",
    )(oa, ob, wa, wb, pg, pg)


def _resproj_kernel(a_ref, w_ref, x_ref, gt_ref, o_ref):
    y = jnp.dot(a_ref[...], w_ref[...], preferred_element_type=F32)
    o_ref[...] = x_ref[...] + gt_ref[...] * y


def _resproj(a, w, x, mod, gate_idx, tm_pref, tn_pref, name):
    m, k = a.shape
    n = w.shape[1]
    tm = _tile(m, tm_pref, SUBLANES)
    tn = _tile(n, tn_pref)
    return pl.pallas_call(
        _resproj_kernel,
        out_shape=jax.ShapeDtypeStruct((m, n), F32),
        grid=(m // tm, n // tn),
        in_specs=[pl.BlockSpec((tm, k), lambda i, j: (i, 0)),
                  pl.BlockSpec((k, tn), lambda i, j: (0, j)),
                  pl.BlockSpec((tm, tn), lambda i, j: (i, j)),
                  pl.BlockSpec((1, tn), lambda i, j: (0, gate_idx * (n // tn) + j))],
        out_specs=pl.BlockSpec((tm, tn), lambda i, j: (i, j)),
        compiler_params=_cparams(("parallel", "arbitrary")),
        name=name,
    )(a, w, x, mod)


def _swiglu_kernel(h_ref, wg_ref, wu_ref, o_ref):
    h = h_ref[...]
    g = jnp.dot(h, wg_ref[...], preferred_element_type=F32)
    u = jnp.dot(h, wu_ref[...], preferred_element_type=F32)
    o_ref[...] = (g * _sigmoid(g) * u).astype(o_ref.dtype)


def _swiglu(h, w):
    m, k = h.shape
    dff = w.shape[1] // 2
    tm = _tile(m, 2048, SUBLANES)
    tn = _tile(dff, 512)
    nj = dff // tn
    return pl.pallas_call(
        _swiglu_kernel,
        out_shape=jax.ShapeDtypeStruct((m, dff), BF16),
        grid=(m // tm, nj),
        in_specs=[pl.BlockSpec((tm, k), lambda i, j: (i, 0)),
                  pl.BlockSpec((k, tn), lambda i, j: (0, j)),
                  pl.BlockSpec((k, tn), lambda i, j: (0, nj + j))],
        out_specs=pl.BlockSpec((tm, tn), lambda i, j: (i, j)),
        compiler_params=_cparams(("parallel", "arbitrary")),
        name="ffn_swiglu",
    )(h, w, w)


def kernel(x, c, w_ada, b_ada, norm_mix_g, norm_ffn_g, w_in, fox_f_bias, hgrn_lb_table, hgrn_onorm_g,
           fox_q_norm_g, fox_k_norm_g, w_branch_a, w_branch_b, w_out, w_ffn_in, w_ffn_out):
    batch, seq, d = x.shape
    assert batch == 1 and w_ada.shape[0] == 1, "single sequence, single layer"
    hw = hgrn_onorm_g.shape[1] * hgrn_onorm_g.shape[2]
    fw = fox_q_norm_g.shape[1] * fox_q_norm_g.shape[2]
    nfh = fox_q_norm_g.shape[1]
    assert hgrn_onorm_g.shape[2] == HEAD_DIM and fox_q_norm_g.shape[2] == HEAD_DIM
    assert nfh <= LANES and hw == fw and seq % (2 * FOX_TK) == 0
    x2 = x.reshape(seq, d)

    f_col = 4 * hw + 3 * fw
    w_bf = jnp.swapaxes(w_in, 1, 2)[0].astype(BF16)
    fbias = jnp.pad(fox_f_bias[0], (0, LANES - nfh)).reshape(1, LANES)

    mod = _ada_mod(c, w_ada[0], b_ada[0])

    h = _normmod(x2, norm_mix_g[0], mod, 0, 1)
    pq = _matmul(h, w_bf, 0, hw, BF16, "in_proj_q")
    hf = _matmul(h, w_bf, hw, hw, F32, "in_proj_f")
    pm = _matmul(h, w_bf, 2 * hw, 2 * hw + 3 * fw, BF16, "in_proj_main")
    ff = _matmul(h, w_bf, f_col, LANES, F32, "in_proj_ff")
    pg = _matmul(h, w_bf, f_col + nfh, 2 * d, BF16, "in_proj_gates")

    o_a = _hgrn(pq, pm, hf, hgrn_lb_table, hgrn_onorm_g[0], hw, 0, hw)

    qa, kta, r = _foxprep(pm, ff, fbias, fox_q_norm_g[0], fox_k_norm_g[0], fw, 2 * hw, 2 * hw + fw, FOX_TK)
    o_b = _fox(qa, kta, r, pm, 2 * hw + 2 * fw, FOX_TK)

    y = _merge(o_a, o_b, w_branch_a[0].astype(BF16), w_branch_b[0].astype(BF16), pg)
    x1 = _resproj(y, w_out[0].astype(BF16), x2, mod, 2, 1024, 512, "out_proj")

    h2 = _normmod(x1, norm_ffn_g[0], mod, 3, 4)
    act = _swiglu(h2, w_ffn_in[0].astype(BF16))
    x3 = _resproj(act, w_ffn_out[0].astype(BF16), x1, mod, 5, 512, 512, "ffn_out")
    return x3.reshape(batch, seq, d)
```

```python
import functools
import math

import jax
import jax.numpy as jnp
from jax import lax
from jax.experimental import pallas as pl
from jax.experimental.pallas import tpu as pltpu

F32 = jnp.float32
BF16 = jnp.bfloat16

EPS = 1e-6
LOG2E = 1.4426950408889634
LANES = 128
SUBLANES = 8
BF16_SUBLANES = 16
HEAD_DIM = 128
HGRN_SUB = 16
HGRN_FSUB = 32
HGRN_UNROLL = 4
HGRN_MIN_LB = 2.0 ** -6
FOX_TK = 512
FOX_KDIM = 2 * HEAD_DIM
V7X_VMEM_LIMIT_BYTES = 56 * 1024 * 1024


def _cparams(semantics, vmem_bytes=V7X_VMEM_LIMIT_BYTES):
    return pltpu.CompilerParams(dimension_semantics=semantics, vmem_limit_bytes=vmem_bytes)


def _tile(dim, pref, align=LANES):
    if dim <= pref:
        return dim
    t = (pref // align) * align
    while t >= align:
        if dim % t == 0:
            return t
        t -= align
    raise ValueError(f"no {align}-aligned tile for {dim}")


def _sigmoid(x):
    return 1.0 / (1.0 + jnp.exp(-x))


def _ada_kernel(c_ref, w_ref, b_ref, o_ref):
    c = c_ref[...]
    s = c * _sigmoid(c)
    o_ref[...] = jnp.sum(w_ref[...] * s, axis=0, keepdims=True) + b_ref[...]


def _ada_mod(c, w_ada, b_ada):
    d, n = w_ada.shape
    tn = _tile(n, 512)
    return pl.pallas_call(
        _ada_kernel,
        out_shape=jax.ShapeDtypeStruct((1, n), F32),
        grid=(n // tn,),
        in_specs=[pl.BlockSpec((d, 1), lambda j: (0, 0)),
                  pl.BlockSpec((d, tn), lambda j: (0, j)),
                  pl.BlockSpec((1, tn), lambda j: (0, j))],
        out_specs=pl.BlockSpec((1, tn), lambda j: (0, j)),
        compiler_params=_cparams(("parallel",)),
        name="ada_mod",
    )(c.reshape(d, 1), w_ada, b_ada.reshape(1, n))


def _normmod_kernel(x_ref, g_ref, sh_ref, sc_ref, o_ref):
    x = x_ref[...]
    ms = jnp.mean(x * x, axis=-1, keepdims=True)
    xn = x * lax.rsqrt(ms + EPS)
    o_ref[...] = (xn * g_ref[...] * (1.0 + sc_ref[...]) + sh_ref[...]).astype(o_ref.dtype)


def _normmod(x, gain, mod, shift_idx, scale_idx):
    s, d = x.shape
    tm = _tile(s, 256, SUBLANES)
    return pl.pallas_call(
        _normmod_kernel,
        out_shape=jax.ShapeDtypeStruct((s, d), BF16),
        grid=(s // tm,),
        in_specs=[pl.BlockSpec((tm, d), lambda i: (i, 0)),
                  pl.BlockSpec((1, d), lambda i: (0, 0)),
                  pl.BlockSpec((1, d), lambda i: (0, shift_idx)),
                  pl.BlockSpec((1, d), lambda i: (0, scale_idx))],
        out_specs=pl.BlockSpec((tm, d), lambda i: (i, 0)),
        compiler_params=_cparams(("parallel",)),
        name="normmod",
    )(x, gain.reshape(1, d), mod, mod)


def _side_cast_specs(side, n_i, n_j):
    in_specs, out_specs, shapes = [], [], []
    for w in side:
        rows, cols = w.shape
        slab = rows // (n_i * n_j)
        assert slab * n_i * n_j == rows and slab % BF16_SUBLANES == 0, (w.shape, n_i, n_j)
        in_specs.append(pl.BlockSpec((slab, cols), lambda i, j: (i * n_j + j, 0)))
        out_specs.append(pl.BlockSpec((slab, cols), lambda i, j: (i * n_j + j, 0)))
        shapes.append(jax.ShapeDtypeStruct((rows, cols), BF16))
    return in_specs, out_specs, shapes


def _side_cast(side_refs):
    n = len(side_refs) // 2
    for src, dst in zip(side_refs[:n], side_refs[n:]):
        dst[...] = src[...].astype(dst.dtype)


def _mm_kernel(x_ref, wt_ref, *refs):
    n_side = (len(refs) - 1) // 2
    o_ref = refs[n_side]
    y = lax.dot_general(x_ref[...], wt_ref[...], (((1,), (1,)), ((), ())), preferred_element_type=F32)
    o_ref[...] = y.astype(o_ref.dtype)
    _side_cast(refs[:n_side] + refs[n_side + 1:])


def _matmul(x, wt, row0, nrows, out_dtype, name, side=()):
    m, k = x.shape
    assert row0 % BF16_SUBLANES == 0
    tm = _tile(m, 1024, SUBLANES)
    tn = _tile(nrows, 1024)
    n_i, n_j = m // tm, nrows // tn
    side_in, side_out, side_shapes = _side_cast_specs(side, n_i, n_j)
    out = pl.pallas_call(
        _mm_kernel,
        out_shape=[jax.ShapeDtypeStruct((m, nrows), out_dtype)] + side_shapes,
        grid=(n_i, n_j),
        in_specs=[pl.BlockSpec((tm, k), lambda i, j: (i, 0)),
                  pl.BlockSpec((pl.Element(tn), pl.Element(k)),
                               lambda i, j: (pl.multiple_of(row0 + j * tn, BF16_SUBLANES), 0))] + side_in,
        out_specs=[pl.BlockSpec((tm, tn), lambda i, j: (i, j))] + side_out,
        compiler_params=_cparams(("parallel", "arbitrary")),
        name=name,
    )(x, wt, *side)
    return out if side else out[0]


def _hgrn_kernel(q_ref, i_ref, g_ref, f_ref, lbt_ref, og_ref, o_ref, st_ref, *, hb, tt):
    @pl.when(pl.program_id(1) == 0)
    def _():
        st_ref[...] = jnp.zeros_like(st_ref)

    tbl = lbt_ref[...]
    e = jnp.exp(tbl - jnp.max(tbl, axis=0, keepdims=True))
    lb = e[0:1, :] / jnp.sum(e, axis=0, keepdims=True)
    half = SUBLANES
    nt_dims = (((1,), (1,)), ((), ()))
    tn_dims = (((0,), (0,)), ((), ()))

    def gates(rows, cs, nrows):
        lbh = lb[:, cs]
        f = lbh + (1.0 - lbh) * _sigmoid(f_ref[rows, cs])
        b = jnp.log2(f)
        row = lax.broadcasted_iota(jnp.int32, (nrows, LANES), 0)
        sh = 1
        while sh < nrows:
            b = b + jnp.where(row >= sh, pltpu.roll(b, sh, axis=0), 0.0)
            sh *= 2
        return 1.0 - f, b

    def finish(o, rows, cs):
        ms = jnp.mean(o * o, axis=-1, keepdims=True)
        on = o * lax.rsqrt(ms + EPS) * og_ref[:, cs]
        gate = g_ref[rows, cs].astype(F32)
        o_ref[rows, cs] = (on * (gate * _sigmoid(gate))).astype(o_ref.dtype)

    mid = HGRN_FSUB // 2
    tri_r = lax.broadcasted_iota(jnp.int32, (HGRN_FSUB, HGRN_FSUB), 0)
    tri_c = lax.broadcasted_iota(jnp.int32, (HGRN_FSUB, HGRN_FSUB), 1)

    def body_factored(n, carry):
        pending = []
        for u, h in [(u, h) for u in range(HGRN_UNROLL) for h in range(hb)]:
            r0 = pl.multiple_of((n * HGRN_UNROLL + u) * HGRN_FSUB, HGRN_FSUB)
            rows = pl.ds(r0, HGRN_FSUB)
            cs = slice(h * LANES, (h + 1) * LANES)
            q = q_ref[rows, cs].astype(F32)
            v = i_ref[rows, cs]
            k, b = gates(rows, cs, HGRN_FSUB)
            r = b[mid - 1:mid, :]
            b_last = b[HGRN_FSUB - 1:HGRN_FSUB, :]
            qc = q * jnp.exp2(b - r)
            kc = (k * jnp.exp2(r - b)).astype(BF16)
            a = lax.dot_general(qc.astype(BF16), kc, nt_dims, preferred_element_type=F32)
            st = st_ref[h]
            o = lax.dot_general((qc * jnp.exp2(r)).astype(BF16), st.astype(BF16), nt_dims,
                                preferred_element_type=F32)
            kt = (k * jnp.exp2(b_last - b)).astype(BF16)
            upd = lax.dot_general(v, kt, tn_dims, preferred_element_type=F32)
            st_ref[h] = st * jnp.exp2(b_last) + upd
            pending.append((a, o, v, rows, cs))
        for a, o, v, rows, cs in pending:
            p = jnp.where(tri_c <= tri_r, a, 0.0).astype(BF16)
            finish(o + jnp.dot(p, v, preferred_element_type=F32), rows, cs)
        return carry

    rowc = lax.broadcasted_iota(jnp.int32, (SUBLANES, 1), 0)

    def body_exact(n, carry):
        r0 = pl.multiple_of(n * HGRN_SUB, HGRN_SUB)
        rows = pl.ds(r0, HGRN_SUB)
        for h in range(hb):
            cs = slice(h * LANES, (h + 1) * LANES)
            q = q_ref[rows, cs].astype(F32)
            v = i_ref[rows, cs]
            vf = v.astype(F32)
            k, b = gates(rows, cs, HGRN_SUB)
            b_last = b[HGRN_SUB - 1:HGRN_SUB, :]
            st = st_ref[h]
            qt = (q * jnp.exp2(b)).astype(BF16)
            o = lax.dot_general(qt, st.astype(BF16), nt_dims, preferred_element_type=F32)
            o_lo, o_hi = o[:half], o[half:]
            q_lo, q_hi = q[:half], q[half:]
            b_lo, b_hi = b[:half], b[half:]
            for s in range(HGRN_SUB):
                ks, bs, vs = k[s:s + 1, :], b[s:s + 1, :], vf[s:s + 1, :]
                c_hi = jnp.sum(q_hi * ks * jnp.exp2(b_hi - bs), axis=-1, keepdims=True)
                if s < half:
                    c_lo = jnp.sum(q_lo * ks * jnp.exp2(b_lo - bs), axis=-1, keepdims=True)
                    o_lo = o_lo + jnp.where(rowc >= s, c_lo, 0.0) * vs
                else:
                    c_hi = jnp.where(rowc >= s - half, c_hi, 0.0)
                o_hi = o_hi + c_hi * vs
            kt = (k * jnp.exp2(b_last - b)).astype(BF16)
            upd = lax.dot_general(v, kt, tn_dims, preferred_element_type=F32)
            st_ref[h] = st * jnp.exp2(b_last) + upd
            finish(jnp.concatenate([o_lo, o_hi], axis=0), rows, cs)
        return carry

    factorable = jnp.min(lb) >= HGRN_MIN_LB

    @pl.when(factorable)
    def _():
        lax.fori_loop(0, tt // (HGRN_FSUB * HGRN_UNROLL), body_factored, 0)

    @pl.when(jnp.logical_not(factorable))
    def _():
        lax.fori_loop(0, tt // HGRN_SUB, body_exact, 0)


def _hgrn(pq, pm, hf, lb_table, onorm_g, width, i_col, g_col):
    s = pq.shape[0]
    hb = min(16, width // HEAD_DIM)
    bw = hb * HEAD_DIM
    tt = _tile(s, 512, HGRN_SUB)
    nl = lb_table.shape[0]
    kern = functools.partial(_hgrn_kernel, hb=hb, tt=tt)
    return pl.pallas_call(
        kern,
        out_shape=jax.ShapeDtypeStruct((s, width), BF16),
        grid=(width // bw, s // tt),
        in_specs=[pl.BlockSpec((tt, bw), lambda h, t: (t, h)),
                  pl.BlockSpec((tt, bw), lambda h, t: (t, i_col // bw + h)),
                  pl.BlockSpec((tt, bw), lambda h, t: (t, g_col // bw + h)),
                  pl.BlockSpec((tt, bw), lambda h, t: (t, h)),
                  pl.BlockSpec((nl, bw), lambda h, t: (0, h)),
                  pl.BlockSpec((1, bw), lambda h, t: (0, h))],
        out_specs=pl.BlockSpec((tt, bw), lambda h, t: (t, h)),
        scratch_shapes=[pltpu.VMEM((hb, HEAD_DIM, HEAD_DIM), F32)],
        compiler_params=_cparams(("parallel", "arbitrary")),
        name="hgrn2",
    )(pq, pm, pm, hf, lb_table, onorm_g.reshape(1, width))


def _split3(x):
    hi = x.astype(BF16).astype(F32)
    r = x - hi
    mid = r.astype(BF16).astype(F32)
    lo = (r - mid).astype(BF16).astype(F32)
    return hi, mid, lo


def _foxprep_kernel(q_ref, k_ref, ff_ref, fb_ref, gq_ref, gk_ref, qo_ref, kto_ref, r_ref, carry_ref, *, nh, scale):
    @pl.when(pl.program_id(0) == 0)
    def _():
        carry_ref[...] = jnp.zeros_like(carry_ref)

    tk = q_ref.shape[0]
    x = ff_ref[...] + fb_ref[...]
    ls = (jnp.minimum(x, 0.0) - jnp.log(1.0 + jnp.exp(-jnp.abs(x)))) * LOG2E
    ri = lax.broadcasted_iota(jnp.int32, (tk, tk), 0)
    ci = lax.broadcasted_iota(jnp.int32, (tk, tk), 1)
    lower = (ci <= ri).astype(F32)
    rel = jnp.dot(lower, ls, precision=lax.Precision.HIGHEST, preferred_element_type=F32)
    base = carry_ref[...]
    r_ref[0] = base
    carry_ref[...] = base + rel[tk - 1:tk, :]

    cq = _split3(rel)
    ck = _split3(rel.T)
    lane = lax.broadcasted_iota(jnp.int32, (tk, LANES), 1)
    sub = lax.broadcasted_iota(jnp.int32, (LANES, tk), 0)
    ones_q = jnp.where((lane >= 3) & (lane < 6), 1.0, 0.0)
    ones_k = jnp.where(sub < 3, 1.0, 0.0)
    for h in range(nh):
        cs = slice(h * LANES, (h + 1) * LANES)
        q = q_ref[:, cs].astype(F32)
        qn = q * lax.rsqrt(jnp.mean(q * q, axis=-1, keepdims=True) + EPS) * gq_ref[:, cs]
        k = k_ref[:, cs].astype(F32)
        kn = k * lax.rsqrt(jnp.mean(k * k, axis=-1, keepdims=True) + EPS) * gk_ref[:, cs]
        eq = ones_q
        ek = ones_k
        for t in range(3):
            eq = jnp.where(lane == t, cq[t][:, h:h + 1], eq)
            ek = jnp.where(sub == 3 + t, -ck[t][h:h + 1, :], ek)
        c0 = h * FOX_KDIM
        qo_ref[:, c0:c0 + LANES] = (qn * scale).astype(qo_ref.dtype)
        qo_ref[:, c0 + LANES:c0 + FOX_KDIM] = eq.astype(qo_ref.dtype)
        kto_ref[0, c0:c0 + LANES, :] = kn.T.astype(kto_ref.dtype)
        kto_ref[0, c0 + LANES:c0 + FOX_KDIM, :] = ek.astype(kto_ref.dtype)


def _foxprep(pm, ff, fbias, gq, gk, width, q_col, k_col, tk):
    s = pm.shape[0]
    nh = width // HEAD_DIM
    nb = s // tk
    kern = functools.partial(_foxprep_kernel, nh=nh, scale=LOG2E * float(HEAD_DIM) ** -0.5)
    qa, kta, r = pl.pallas_call(
        kern,
        out_shape=(jax.ShapeDtypeStruct((s, nh * FOX_KDIM), BF16),
                   jax.ShapeDtypeStruct((nb, nh * FOX_KDIM, tk), BF16),
                   jax.ShapeDtypeStruct((nb, 1, LANES), F32)),
        grid=(nb,),
        in_specs=[pl.BlockSpec((tk, width), lambda i: (i, q_col // width)),
                  pl.BlockSpec((tk, width), lambda i: (i, k_col // width)),
                  pl.BlockSpec((tk, LANES), lambda i: (i, 0)),
                  pl.BlockSpec((1, LANES), lambda i: (0, 0)),
                  pl.BlockSpec((1, width), lambda i: (0, 0)),
                  pl.BlockSpec((1, width), lambda i: (0, 0))],
        out_specs=(pl.BlockSpec((tk, nh * FOX_KDIM), lambda i: (i, 0)),
                   pl.BlockSpec((1, nh * FOX_KDIM, tk), lambda i: (i, 0, 0)),
                   pl.BlockSpec((1, 1, LANES), lambda i: (i, 0, 0))),
        scratch_shapes=[pltpu.VMEM((1, LANES), F32)],
        compiler_params=_cparams(("arbitrary",)),
        name="fox_prep",
    )(pm, pm, ff, fbias, gq.reshape(1, width), gk.reshape(1, width))
    return qa, kta, r.reshape(nb, LANES)


def _fox_kernel(r_ref, q_ref, kt_ref, v_ref, o_ref, v2_ref, s_ref, m_ref, acc_ref, *, tk):
    h = pl.program_id(0)
    qi = pl.program_id(1)

    @pl.when(qi == 0)
    def _():
        v2_ref[:, :LANES] = v_ref[...]
        lane = lax.broadcasted_iota(jnp.int32, v_ref.shape, 1)
        v2_ref[:, LANES:] = jnp.where(lane == 0, 1.0, 0.0).astype(v2_ref.dtype)

    m_ref[...] = jnp.full(m_ref.shape, -jnp.inf, F32)
    acc_ref[...] = jnp.zeros(acc_ref.shape, F32)
    rr = lax.broadcasted_iota(jnp.int32, (tk, tk), 0)
    cc = lax.broadcasted_iota(jnp.int32, (tk, tk), 1)

    def scores(c, j, slot):
        s_ref[slot, c] = jnp.dot(q_ref[c * tk:(c + 1) * tk, :], kt_ref[j], preferred_element_type=F32)

    def fold(c, j, slot, diag):
        s = s_ref[slot, c]
        if diag:
            s = jnp.where(cc <= rr, s, -jnp.inf)
        d = r_ref[2 * qi + c, h] - r_ref[j, h]
        m = m_ref[c]
        m_new = jnp.maximum(m, jnp.max(s, axis=-1, keepdims=True) + d)
        p = jnp.exp2(s - (m_new - d))
        r0 = pl.multiple_of(j * tk, tk)
        pv = jnp.dot(p.astype(BF16), v2_ref[pl.ds(r0, tk), :], preferred_element_type=F32)
        acc_ref[c] = jnp.exp2(m - m_new) * acc_ref[c] + pv
        m_ref[c] = m_new

    scores(0, 0, 0)
    scores(1, 0, 0)

    def pair(j):
        for slot in range(2):
            scores(0, j + slot + 1, 1 - slot)
            scores(1, j + slot + 1, 1 - slot)
            fold(0, j + slot, slot, False)
            fold(1, j + slot, slot, False)

    def body(jp, carry):
        pair(2 * jp)
        return carry

    lax.fori_loop(0, qi, body, 0)
    scores(1, 2 * qi + 1, 1)
    fold(0, 2 * qi, 0, True)
    fold(1, 2 * qi, 0, False)
    fold(1, 2 * qi + 1, 1, True)
    for c in range(2):
        acc = acc_ref[c]
        o_ref[c * tk:(c + 1) * tk, :] = (acc[:, :LANES] / acc[:, LANES:LANES + 1]).astype(o_ref.dtype)


def _fox(qa, kta, r, pm, v_col, tk):
    s = qa.shape[0]
    nh = qa.shape[1] // FOX_KDIM
    nb = s // tk
    tq = 2 * tk
    kern = functools.partial(_fox_kernel, tk=tk)
    return pl.pallas_call(
        kern,
        out_shape=jax.ShapeDtypeStruct((s, nh * HEAD_DIM), BF16),
        grid=(nh, s // tq),
        in_specs=[pl.BlockSpec(memory_space=pltpu.SMEM),
                  pl.BlockSpec((tq, FOX_KDIM), lambda h, i: (i, h)),
                  pl.BlockSpec((nb, FOX_KDIM, tk), lambda h, i: (0, h, 0)),
                  pl.BlockSpec((s, HEAD_DIM), lambda h, i: (0, v_col // HEAD_DIM + h))],
        out_specs=pl.BlockSpec((tq, HEAD_DIM), lambda h, i: (i, h)),
        scratch_shapes=[pltpu.VMEM((s, FOX_KDIM), BF16),
                        pltpu.VMEM((2, 2, tk, tk), F32),
                        pltpu.VMEM((2, tk, 1), F32),
                        pltpu.VMEM((2, tk, FOX_KDIM), F32)],
        compiler_params=_cparams(("parallel", "arbitrary")),
        name="fox_attn",
    )(r, qa, kta, pm)


def _merge_kernel(oa_ref, ob_ref, wa_ref, wb_ref, ga_ref, gb_ref, y_ref):
    sa = _sigmoid(ga_ref[...].astype(F32))
    sb = _sigmoid(gb_ref[...].astype(F32))
    ya = jnp.dot(oa_ref[...], wa_ref[...], preferred_element_type=F32)
    yb = jnp.dot(ob_ref[...], wb_ref[...], preferred_element_type=F32)
    y_ref[...] = (sa * ya + sb * yb).astype(y_ref.dtype)


def _merge(oa, ob, wa, wb, pg):
    m, ka = oa.shape
    kb = ob.shape[1]
    n = wa.shape[1]
    tm = _tile(m, 1024, SUBLANES)
    tn = _tile(n, 512)
    return pl.pallas_call(
        _merge_kernel,
        out_shape=jax.ShapeDtypeStruct((m, n), BF16),
        grid=(m // tm, n // tn),
        in_specs=[pl.BlockSpec((tm, ka), lambda i, j: (i, 0)),
                  pl.BlockSpec((tm, kb), lambda i, j: (i, 0)),
                  pl.BlockSpec((ka, tn), lambda i, j: (0, j)),
                  pl.BlockSpec((kb, tn), lambda i, j: (0, j)),
                  pl.BlockSpec((tm, tn), lambda i, j: (i, j)),
                  pl.BlockSpec((tm, tn), lambda i, j: (i, n // tn + j))],
        out_specs=pl.BlockSpec((tm, tn), lambda i, j: (i, j)),
        compiler_params=_cparams(("parallel", "arbitrary")),
        name="branch_merge",
    )(oa, ob, wa, wb, pg, pg)


def _resproj_kernel(a_ref, w_ref, x_ref, gt_ref, o_ref):
    y = jnp.dot(a_ref[...], w_ref[...], preferred_element_type=F32)
    o_ref[...] = x_ref[...] + gt_ref[...] * y


def _resproj(a, w, x, mod, gate_idx, tm_pref, tn_pref, name):
    m, k = a.shape
    n = w.shape[1]
    tm = _tile(m, tm_pref, SUBLANES)
    tn = _tile(n, tn_pref)
    return pl.pallas_call(
        _resproj_kernel,
        out_shape=jax.ShapeDtypeStruct((m, n), F32),
        grid=(m // tm, n // tn),
        in_specs=[pl.BlockSpec((tm, k), lambda i, j: (i, 0)),
                  pl.BlockSpec((k, tn), lambda i, j: (0, j)),
                  pl.BlockSpec((tm, tn), lambda i, j: (i, j)),
                  pl.BlockSpec((1, tn), lambda i, j: (0, gate_idx * (n // tn) + j))],
        out_specs=pl.BlockSpec((tm, tn), lambda i, j: (i, j)),
        compiler_params=_cparams(("parallel", "arbitrary")),
        name=name,
    )(a, w, x, mod)


def _swiglu_kernel(h_ref, wg_ref, wu_ref, *refs):
    n_side = (len(refs) - 1) // 2
    o_ref = refs[n_side]
    h = h_ref[...]
    g = jnp.dot(h, wg_ref[...], preferred_element_type=F32)
    u = jnp.dot(h, wu_ref[...], preferred_element_type=F32)
    o_ref[...] = (g * _sigmoid(g) * u).astype(o_ref.dtype)
    _side_cast(refs[:n_side] + refs[n_side + 1:])


def _swiglu(h, w, side=()):
    m, k = h.shape
    dff = w.shape[1] // 2
    tm = _tile(m, 2048, SUBLANES)
    tn = _tile(dff, 512)
    nj = dff // tn
    side_in, side_out, side_shapes = _side_cast_specs(side, m // tm, nj)
    out = pl.pallas_call(
        _swiglu_kernel,
        out_shape=[jax.ShapeDtypeStruct((m, dff), BF16)] + side_shapes,
        grid=(m // tm, nj),
        in_specs=[pl.BlockSpec((tm, k), lambda i, j: (i, 0)),
                  pl.BlockSpec((k, tn), lambda i, j: (0, j)),
                  pl.BlockSpec((k, tn), lambda i, j: (0, nj + j))] + side_in,
        out_specs=[pl.BlockSpec((tm, tn), lambda i, j: (i, j))] + side_out,
        compiler_params=_cparams(("parallel", "arbitrary")),
        name="ffn_swiglu",
    )(h, w, w, *side)
    return out if side else out[0]


def kernel(x, c, w_ada, b_ada, norm_mix_g, norm_ffn_g, w_in, fox_f_bias, hgrn_lb_table, hgrn_onorm_g,
           fox_q_norm_g, fox_k_norm_g, w_branch_a, w_branch_b, w_out, w_ffn_in, w_ffn_out):
    batch, seq, d = x.shape
    assert batch == 1 and w_ada.shape[0] == 1, "single sequence, single layer"
    hw = hgrn_onorm_g.shape[1] * hgrn_onorm_g.shape[2]
    fw = fox_q_norm_g.shape[1] * fox_q_norm_g.shape[2]
    nfh = fox_q_norm_g.shape[1]
    assert hgrn_onorm_g.shape[2] == HEAD_DIM and fox_q_norm_g.shape[2] == HEAD_DIM
    assert nfh <= LANES and hw == fw and seq % (2 * FOX_TK) == 0
    x2 = x.reshape(seq, d)

    f_col = 4 * hw + 3 * fw
    w_bf = jnp.swapaxes(w_in, 1, 2)[0].astype(BF16)
    fbias = jnp.pad(fox_f_bias[0], (0, LANES - nfh)).reshape(1, LANES)

    mod = _ada_mod(c, w_ada[0], b_ada[0])

    h = _normmod(x2, norm_mix_g[0], mod, 0, 1)
    pq, w_out_bf = _matmul(h, w_bf, 0, hw, BF16, "in_proj_q", side=(w_out[0],))
    hf, w_a_bf, w_b_bf = _matmul(h, w_bf, hw, hw, F32, "in_proj_f",
                                 side=(w_branch_a[0], w_branch_b[0]))
    pm = _matmul(h, w_bf, 2 * hw, 2 * hw + 3 * fw, BF16, "in_proj_main")
    ff = _matmul(h, w_bf, f_col, LANES, F32, "in_proj_ff")
    pg, w_ffn_in_bf = _matmul(h, w_bf, f_col + nfh, 2 * d, BF16, "in_proj_gates",
                              side=(w_ffn_in[0],))

    o_a = _hgrn(pq, pm, hf, hgrn_lb_table, hgrn_onorm_g[0], hw, 0, hw)

    qa, kta, r = _foxprep(pm, ff, fbias, fox_q_norm_g[0], fox_k_norm_g[0], fw, 2 * hw, 2 * hw + fw, FOX_TK)
    o_b = _fox(qa, kta, r, pm, 2 * hw + 2 * fw, FOX_TK)

    y = _merge(o_a, o_b, w_a_bf, w_b_bf, pg)
    x1 = _resproj(y, w_out_bf, x2, mod, 2, 1024, 512, "out_proj")

    h2 = _normmod(x1, norm_ffn_g[0], mod, 3, 4)
    act, w_ffn_out_bf = _swiglu(h2, w_ffn_in_bf, side=(w_ffn_out[0],))
    x3 = _resproj(act, w_ffn_out_bf, x1, mod, 5, 512, 512, "ffn_out")
    return x3.reshape(batch, seq, d)
```

```python
import functools
import math

import jax
import jax.numpy as jnp
from jax import lax
from jax.experimental import pallas as pl
from jax.experimental.pallas import tpu as pltpu

F32 = jnp.float32
BF16 = jnp.bfloat16

EPS = 1e-6
LOG2E = 1.4426950408889634
LANES = 128
SUBLANES = 8
BF16_SUBLANES = 16
HEAD_DIM = 128
HGRN_SUB = 16
HGRN_FSUB = 32
HGRN_UNROLL = 4
HGRN_MIN_LB = 2.0 ** -6
FOX_TK = 512
FOX_KDIM = 2 * HEAD_DIM
V7X_VMEM_LIMIT_BYTES = 56 * 1024 * 1024


def _cparams(semantics, vmem_bytes=V7X_VMEM_LIMIT_BYTES):
    return pltpu.CompilerParams(dimension_semantics=semantics, vmem_limit_bytes=vmem_bytes)


def _tile(dim, pref, align=LANES):
    if dim <= pref:
        return dim
    t = (pref // align) * align
    while t >= align:
        if dim % t == 0:
            return t
        t -= align
    raise ValueError(f"no {align}-aligned tile for {dim}")


def _sigmoid(x):
    return 1.0 / (1.0 + jnp.exp(-x))


def _ada_kernel(c_ref, w_ref, b_ref, o_ref):
    c = c_ref[...]
    s = c * _sigmoid(c)
    o_ref[...] = jnp.sum(w_ref[...] * s, axis=0, keepdims=True) + b_ref[...]


def _ada_mod(c, w_ada, b_ada):
    d, n = w_ada.shape
    tn = _tile(n, 512)
    return pl.pallas_call(
        _ada_kernel,
        out_shape=jax.ShapeDtypeStruct((1, n), F32),
        grid=(n // tn,),
        in_specs=[pl.BlockSpec((d, 1), lambda j: (0, 0)),
                  pl.BlockSpec((d, tn), lambda j: (0, j)),
                  pl.BlockSpec((1, tn), lambda j: (0, j))],
        out_specs=pl.BlockSpec((1, tn), lambda j: (0, j)),
        compiler_params=_cparams(("parallel",)),
        name="ada_mod",
    )(c.reshape(d, 1), w_ada, b_ada.reshape(1, n))


def _normmod_kernel(x_ref, g_ref, sh_ref, sc_ref, o_ref):
    x = x_ref[...]
    ms = jnp.mean(x * x, axis=-1, keepdims=True)
    xn = x * lax.rsqrt(ms + EPS)
    o_ref[...] = (xn * g_ref[...] * (1.0 + sc_ref[...]) + sh_ref[...]).astype(o_ref.dtype)


def _normmod(x, gain, mod, shift_idx, scale_idx):
    s, d = x.shape
    tm = _tile(s, 256, SUBLANES)
    return pl.pallas_call(
        _normmod_kernel,
        out_shape=jax.ShapeDtypeStruct((s, d), BF16),
        grid=(s // tm,),
        in_specs=[pl.BlockSpec((tm, d), lambda i: (i, 0)),
                  pl.BlockSpec((1, d), lambda i: (0, 0)),
                  pl.BlockSpec((1, d), lambda i: (0, shift_idx)),
                  pl.BlockSpec((1, d), lambda i: (0, scale_idx))],
        out_specs=pl.BlockSpec((tm, d), lambda i: (i, 0)),
        compiler_params=_cparams(("parallel",)),
        name="normmod",
    )(x, gain.reshape(1, d), mod, mod)


def _side_cast_specs(side, n_i, n_j):
    in_specs, out_specs, shapes = [], [], []
    for item in side:
        w, row0, rows = item if isinstance(item, tuple) else (item, 0, item.shape[0])
        cols = w.shape[1]
        assert rows % BF16_SUBLANES == 0 and row0 % BF16_SUBLANES == 0, (row0, rows)
        units = rows // BF16_SUBLANES
        n_slabs = max(dv for dv in range(1, min(units, n_i * n_j) + 1) if units % dv == 0)
        slab = rows // n_slabs

        def slab_index(i, j, n_slabs=n_slabs):
            return jnp.minimum(i * n_j + j, n_slabs - 1)

        in_specs.append(pl.BlockSpec(
            (pl.Element(slab), pl.Element(cols)),
            lambda i, j, f=slab_index, row0=row0, slab=slab:
                (pl.multiple_of(row0 + f(i, j) * slab, BF16_SUBLANES), 0)))
        out_specs.append(pl.BlockSpec((slab, cols), lambda i, j, f=slab_index: (f(i, j), 0)))
        shapes.append(jax.ShapeDtypeStruct((rows, cols), BF16))
    return in_specs, out_specs, shapes


def _side_cast(side_refs):
    n = len(side_refs) // 2
    for src, dst in zip(side_refs[:n], side_refs[n:]):
        dst[...] = src[...].astype(dst.dtype)


def _mm_kernel(x_ref, wt_ref, *refs):
    n_side = (len(refs) - 1) // 2
    o_ref = refs[n_side]
    y = lax.dot_general(x_ref[...], wt_ref[...].astype(BF16), (((1,), (1,)), ((), ())),
                        preferred_element_type=F32)
    o_ref[...] = y.astype(o_ref.dtype)
    _side_cast(refs[:n_side] + refs[n_side + 1:])


def _side_arrays(side):
    return [item[0] if isinstance(item, tuple) else item for item in side]


def _matmul(x, wt, row0, nrows, out_dtype, name, side=(), tn_pref=1024):
    m, k = x.shape
    assert row0 % BF16_SUBLANES == 0
    tm = _tile(m, 1024, SUBLANES)
    tn = _tile(nrows, tn_pref)
    n_i, n_j = m // tm, nrows // tn
    side_in, side_out, side_shapes = _side_cast_specs(side, n_i, n_j)
    out = pl.pallas_call(
        _mm_kernel,
        out_shape=[jax.ShapeDtypeStruct((m, nrows), out_dtype)] + side_shapes,
        grid=(n_i, n_j),
        in_specs=[pl.BlockSpec((tm, k), lambda i, j: (i, 0)),
                  pl.BlockSpec((pl.Element(tn), pl.Element(k)),
                               lambda i, j: (pl.multiple_of(row0 + j * tn, BF16_SUBLANES), 0))] + side_in,
        out_specs=[pl.BlockSpec((tm, tn), lambda i, j: (i, j))] + side_out,
        compiler_params=_cparams(("arbitrary" if side else "parallel", "arbitrary")),
        name=name,
    )(x, wt, *_side_arrays(side))
    return out if side else out[0]


def _hgrn_kernel(q_ref, i_ref, g_ref, f_ref, lbt_ref, og_ref, o_ref, st_ref, *, hb, tt):
    @pl.when(pl.program_id(1) == 0)
    def _():
        st_ref[...] = jnp.zeros_like(st_ref)

    tbl = lbt_ref[...]
    e = jnp.exp(tbl - jnp.max(tbl, axis=0, keepdims=True))
    lb = e[0:1, :] / jnp.sum(e, axis=0, keepdims=True)
    half = SUBLANES
    nt_dims = (((1,), (1,)), ((), ()))
    tn_dims = (((0,), (0,)), ((), ()))

    def gates(rows, cs, nrows):
        lbh = lb[:, cs]
        f = lbh + (1.0 - lbh) * _sigmoid(f_ref[rows, cs])
        b = jnp.log2(f)
        row = lax.broadcasted_iota(jnp.int32, (nrows, LANES), 0)
        sh = 1
        while sh < nrows:
            b = b + jnp.where(row >= sh, pltpu.roll(b, sh, axis=0), 0.0)
            sh *= 2
        return 1.0 - f, b

    def finish(o, rows, cs):
        ms = jnp.mean(o * o, axis=-1, keepdims=True)
        on = o * lax.rsqrt(ms + EPS) * og_ref[:, cs]
        gate = g_ref[rows, cs].astype(F32)
        o_ref[rows, cs] = (on * (gate * _sigmoid(gate))).astype(o_ref.dtype)

    mid = HGRN_FSUB // 2
    tri_r = lax.broadcasted_iota(jnp.int32, (HGRN_FSUB, HGRN_FSUB), 0)
    tri_c = lax.broadcasted_iota(jnp.int32, (HGRN_FSUB, HGRN_FSUB), 1)

    def body_factored(n, carry):
        pending = []
        for u, h in [(u, h) for u in range(HGRN_UNROLL) for h in range(hb)]:
            r0 = pl.multiple_of((n * HGRN_UNROLL + u) * HGRN_FSUB, HGRN_FSUB)
            rows = pl.ds(r0, HGRN_FSUB)
            cs = slice(h * LANES, (h + 1) * LANES)
            q = q_ref[rows, cs].astype(F32)
            v = i_ref[rows, cs]
            k, b = gates(rows, cs, HGRN_FSUB)
            r = b[mid - 1:mid, :]
            b_last = b[HGRN_FSUB - 1:HGRN_FSUB, :]
            qc = q * jnp.exp2(b - r)
            kc = (k * jnp.exp2(r - b)).astype(BF16)
            a = lax.dot_general(qc.astype(BF16), kc, nt_dims, preferred_element_type=F32)
            st = st_ref[h]
            o = lax.dot_general((qc * jnp.exp2(r)).astype(BF16), st.astype(BF16), nt_dims,
                                preferred_element_type=F32)
            kt = (k * jnp.exp2(b_last - b)).astype(BF16)
            upd = lax.dot_general(v, kt, tn_dims, preferred_element_type=F32)
            st_ref[h] = st * jnp.exp2(b_last) + upd
            pending.append((a, o, v, rows, cs))
        for a, o, v, rows, cs in pending:
            p = jnp.where(tri_c <= tri_r, a, 0.0).astype(BF16)
            finish(o + jnp.dot(p, v, preferred_element_type=F32), rows, cs)
        return carry

    rowc = lax.broadcasted_iota(jnp.int32, (SUBLANES, 1), 0)

    def body_exact(n, carry):
        r0 = pl.multiple_of(n * HGRN_SUB, HGRN_SUB)
        rows = pl.ds(r0, HGRN_SUB)
        for h in range(hb):
            cs = slice(h * LANES, (h + 1) * LANES)
            q = q_ref[rows, cs].astype(F32)
            v = i_ref[rows, cs]
            vf = v.astype(F32)
            k, b = gates(rows, cs, HGRN_SUB)
            b_last = b[HGRN_SUB - 1:HGRN_SUB, :]
            st = st_ref[h]
            qt = (q * jnp.exp2(b)).astype(BF16)
            o = lax.dot_general(qt, st.astype(BF16), nt_dims, preferred_element_type=F32)
            o_lo, o_hi = o[:half], o[half:]
            q_lo, q_hi = q[:half], q[half:]
            b_lo, b_hi = b[:half], b[half:]
            for s in range(HGRN_SUB):
                ks, bs, vs = k[s:s + 1, :], b[s:s + 1, :], vf[s:s + 1, :]
                c_hi = jnp.sum(q_hi * ks * jnp.exp2(b_hi - bs), axis=-1, keepdims=True)
                if s < half:
                    c_lo = jnp.sum(q_lo * ks * jnp.exp2(b_lo - bs), axis=-1, keepdims=True)
                    o_lo = o_lo + jnp.where(rowc >= s, c_lo, 0.0) * vs
                else:
                    c_hi = jnp.where(rowc >= s - half, c_hi, 0.0)
                o_hi = o_hi + c_hi * vs
            kt = (k * jnp.exp2(b_last - b)).astype(BF16)
            upd = lax.dot_general(v, kt, tn_dims, preferred_element_type=F32)
            st_ref[h] = st * jnp.exp2(b_last) + upd
            finish(jnp.concatenate([o_lo, o_hi], axis=0), rows, cs)
        return carry

    factorable = jnp.min(lb) >= HGRN_MIN_LB

    @pl.when(factorable)
    def _():
        lax.fori_loop(0, tt // (HGRN_FSUB * HGRN_UNROLL), body_factored, 0)

    @pl.when(jnp.logical_not(factorable))
    def _():
        lax.fori_loop(0, tt // HGRN_SUB, body_exact, 0)


def _hgrn(pq, pm, hf, lb_table, onorm_g, width, i_col, g_col):
    s = pq.shape[0]
    hb = min(16, width // HEAD_DIM)
    bw = hb * HEAD_DIM
    tt = _tile(s, 512, HGRN_SUB)
    nl = lb_table.shape[0]
    kern = functools.partial(_hgrn_kernel, hb=hb, tt=tt)
    return pl.pallas_call(
        kern,
        out_shape=jax.ShapeDtypeStruct((s, width), BF16),
        grid=(width // bw, s // tt),
        in_specs=[pl.BlockSpec((tt, bw), lambda h, t: (t, h)),
                  pl.BlockSpec((tt, bw), lambda h, t: (t, i_col // bw + h)),
                  pl.BlockSpec((tt, bw), lambda h, t: (t, g_col // bw + h)),
                  pl.BlockSpec((tt, bw), lambda h, t: (t, h)),
                  pl.BlockSpec((nl, bw), lambda h, t: (0, h)),
                  pl.BlockSpec((1, bw), lambda h, t: (0, h))],
        out_specs=pl.BlockSpec((tt, bw), lambda h, t: (t, h)),
        scratch_shapes=[pltpu.VMEM((hb, HEAD_DIM, HEAD_DIM), F32)],
        compiler_params=_cparams(("parallel", "arbitrary")),
        name="hgrn2",
    )(pq, pm, pm, hf, lb_table, onorm_g.reshape(1, width))


def _split3(x):
    hi = x.astype(BF16).astype(F32)
    r = x - hi
    mid = r.astype(BF16).astype(F32)
    lo = (r - mid).astype(BF16).astype(F32)
    return hi, mid, lo


def _foxprep_kernel(q_ref, k_ref, ff_ref, fb_ref, gq_ref, gk_ref, qo_ref, kto_ref, r_ref, carry_ref, *, nh, scale):
    @pl.when(pl.program_id(0) == 0)
    def _():
        carry_ref[...] = jnp.zeros_like(carry_ref)

    tk = q_ref.shape[0]
    x = ff_ref[...] + fb_ref[...]
    ls = (jnp.minimum(x, 0.0) - jnp.log(1.0 + jnp.exp(-jnp.abs(x)))) * LOG2E
    ri = lax.broadcasted_iota(jnp.int32, (tk, tk), 0)
    ci = lax.broadcasted_iota(jnp.int32, (tk, tk), 1)
    lower = (ci <= ri).astype(F32)
    rel = jnp.dot(lower, ls, precision=lax.Precision.HIGHEST, preferred_element_type=F32)
    base = carry_ref[...]
    r_ref[0] = base
    carry_ref[...] = base + rel[tk - 1:tk, :]

    cq = _split3(rel)
    ck = _split3(rel.T)
    lane = lax.broadcasted_iota(jnp.int32, (tk, LANES), 1)
    sub = lax.broadcasted_iota(jnp.int32, (LANES, tk), 0)
    ones_q = jnp.where((lane >= 3) & (lane < 6), 1.0, 0.0)
    ones_k = jnp.where(sub < 3, 1.0, 0.0)
    for h in range(nh):
        cs = slice(h * LANES, (h + 1) * LANES)
        q = q_ref[:, cs].astype(F32)
        qn = q * lax.rsqrt(jnp.mean(q * q, axis=-1, keepdims=True) + EPS) * gq_ref[:, cs]
        k = k_ref[:, cs].astype(F32)
        kn = k * lax.rsqrt(jnp.mean(k * k, axis=-1, keepdims=True) + EPS) * gk_ref[:, cs]
        eq = ones_q
        ek = ones_k
        for t in range(3):
            eq = jnp.where(lane == t, cq[t][:, h:h + 1], eq)
            ek = jnp.where(sub == 3 + t, -ck[t][h:h + 1, :], ek)
        c0 = h * FOX_KDIM
        qo_ref[:, c0:c0 + LANES] = (qn * scale).astype(qo_ref.dtype)
        qo_ref[:, c0 + LANES:c0 + FOX_KDIM] = eq.astype(qo_ref.dtype)
        kto_ref[0, c0:c0 + LANES, :] = kn.T.astype(kto_ref.dtype)
        kto_ref[0, c0 + LANES:c0 + FOX_KDIM, :] = ek.astype(kto_ref.dtype)


def _foxprep(pm, ff, fbias, gq, gk, width, q_col, k_col, tk):
    s = pm.shape[0]
    nh = width // HEAD_DIM
    nb = s // tk
    kern = functools.partial(_foxprep_kernel, nh=nh, scale=LOG2E * float(HEAD_DIM) ** -0.5)
    qa, kta, r = pl.pallas_call(
        kern,
        out_shape=(jax.ShapeDtypeStruct((s, nh * FOX_KDIM), BF16),
                   jax.ShapeDtypeStruct((nb, nh * FOX_KDIM, tk), BF16),
                   jax.ShapeDtypeStruct((nb, 1, LANES), F32)),
        grid=(nb,),
        in_specs=[pl.BlockSpec((tk, width), lambda i: (i, q_col // width)),
                  pl.BlockSpec((tk, width), lambda i: (i, k_col // width)),
                  pl.BlockSpec((tk, LANES), lambda i: (i, 0)),
                  pl.BlockSpec((1, LANES), lambda i: (0, 0)),
                  pl.BlockSpec((1, width), lambda i: (0, 0)),
                  pl.BlockSpec((1, width), lambda i: (0, 0))],
        out_specs=(pl.BlockSpec((tk, nh * FOX_KDIM), lambda i: (i, 0)),
                   pl.BlockSpec((1, nh * FOX_KDIM, tk), lambda i: (i, 0, 0)),
                   pl.BlockSpec((1, 1, LANES), lambda i: (i, 0, 0))),
        scratch_shapes=[pltpu.VMEM((1, LANES), F32)],
        compiler_params=_cparams(("arbitrary",)),
        name="fox_prep",
    )(pm, pm, ff, fbias, gq.reshape(1, width), gk.reshape(1, width))
    return qa, kta, r.reshape(nb, LANES)


def _fox_kernel(r_ref, q_ref, kt_ref, v_ref, o_ref, v2_ref, s_ref, m_ref, acc_ref, *, tk):
    h = pl.program_id(0)
    qi = pl.program_id(1)

    @pl.when(qi == 0)
    def _():
        v2_ref[:, :LANES] = v_ref[...]
        lane = lax.broadcasted_iota(jnp.int32, v_ref.shape, 1)
        v2_ref[:, LANES:] = jnp.where(lane == 0, 1.0, 0.0).astype(v2_ref.dtype)

    m_ref[...] = jnp.full(m_ref.shape, -jnp.inf, F32)
    acc_ref[...] = jnp.zeros(acc_ref.shape, F32)
    rr = lax.broadcasted_iota(jnp.int32, (tk, tk), 0)
    cc = lax.broadcasted_iota(jnp.int32, (tk, tk), 1)

    def scores(c, j, slot):
        s_ref[slot, c] = jnp.dot(q_ref[c * tk:(c + 1) * tk, :], kt_ref[j], preferred_element_type=F32)

    def fold(c, j, slot, diag):
        s = s_ref[slot, c]
        if diag:
            s = jnp.where(cc <= rr, s, -jnp.inf)
        d = r_ref[2 * qi + c, h] - r_ref[j, h]
        m = m_ref[c]
        m_new = jnp.maximum(m, jnp.max(s, axis=-1, keepdims=True) + d)
        p = jnp.exp2(s - (m_new - d))
        r0 = pl.multiple_of(j * tk, tk)
        pv = jnp.dot(p.astype(BF16), v2_ref[pl.ds(r0, tk), :], preferred_element_type=F32)
        acc_ref[c] = jnp.exp2(m - m_new) * acc_ref[c] + pv
        m_ref[c] = m_new

    scores(0, 0, 0)
    scores(1, 0, 0)

    def pair(j):
        for slot in range(2):
            scores(0, j + slot + 1, 1 - slot)
            scores(1, j + slot + 1, 1 - slot)
            fold(0, j + slot, slot, False)
            fold(1, j + slot, slot, False)

    def body(jp, carry):
        pair(2 * jp)
        return carry

    lax.fori_loop(0, qi, body, 0)
    scores(1, 2 * qi + 1, 1)
    fold(0, 2 * qi, 0, True)
    fold(1, 2 * qi, 0, False)
    fold(1, 2 * qi + 1, 1, True)
    for c in range(2):
        acc = acc_ref[c]
        o_ref[c * tk:(c + 1) * tk, :] = (acc[:, :LANES] / acc[:, LANES:LANES + 1]).astype(o_ref.dtype)


def _fox(qa, kta, r, pm, v_col, tk):
    s = qa.shape[0]
    nh = qa.shape[1] // FOX_KDIM
    nb = s // tk
    tq = 2 * tk
    kern = functools.partial(_fox_kernel, tk=tk)
    return pl.pallas_call(
        kern,
        out_shape=jax.ShapeDtypeStruct((s, nh * HEAD_DIM), BF16),
        grid=(nh, s // tq),
        in_specs=[pl.BlockSpec(memory_space=pltpu.SMEM),
                  pl.BlockSpec((tq, FOX_KDIM), lambda h, i: (i, h)),
                  pl.BlockSpec((nb, FOX_KDIM, tk), lambda h, i: (0, h, 0)),
                  pl.BlockSpec((s, HEAD_DIM), lambda h, i: (0, v_col // HEAD_DIM + h))],
        out_specs=pl.BlockSpec((tq, HEAD_DIM), lambda h, i: (i, h)),
        scratch_shapes=[pltpu.VMEM((s, FOX_KDIM), BF16),
                        pltpu.VMEM((2, 2, tk, tk), F32),
                        pltpu.VMEM((2, tk, 1), F32),
                        pltpu.VMEM((2, tk, FOX_KDIM), F32)],
        compiler_params=_cparams(("parallel", "arbitrary")),
        name="fox_attn",
    )(r, qa, kta, pm)


def _merge_kernel(oa_ref, ob_ref, wa_ref, wb_ref, ga_ref, gb_ref, y_ref):
    sa = _sigmoid(ga_ref[...].astype(F32))
    sb = _sigmoid(gb_ref[...].astype(F32))
    ya = jnp.dot(oa_ref[...], wa_ref[...], preferred_element_type=F32)
    yb = jnp.dot(ob_ref[...], wb_ref[...], preferred_element_type=F32)
    y_ref[...] = (sa * ya + sb * yb).astype(y_ref.dtype)


def _merge(oa, ob, wa, wb, pg):
    m, ka = oa.shape
    kb = ob.shape[1]
    n = wa.shape[1]
    tm = _tile(m, 1024, SUBLANES)
    tn = _tile(n, 512)
    return pl.pallas_call(
        _merge_kernel,
        out_shape=jax.ShapeDtypeStruct((m, n), BF16),
        grid=(m // tm, n // tn),
        in_specs=[pl.BlockSpec((tm, ka), lambda i, j: (i, 0)),
                  pl.BlockSpec((tm, kb), lambda i, j: (i, 0)),
                  pl.BlockSpec((ka, tn), lambda i, j: (0, j)),
                  pl.BlockSpec((kb, tn), lambda i, j: (0, j)),
                  pl.BlockSpec((tm, tn), lambda i, j: (i, j)),
                  pl.BlockSpec((tm, tn), lambda i, j: (i, n // tn + j))],
        out_specs=pl.BlockSpec((tm, tn), lambda i, j: (i, j)),
        compiler_params=_cparams(("parallel", "arbitrary")),
        name="branch_merge",
    )(oa, ob, wa, wb, pg, pg)


def _resproj_kernel(a_ref, w_ref, x_ref, gt_ref, o_ref):
    y = jnp.dot(a_ref[...], w_ref[...], preferred_element_type=F32)
    o_ref[...] = x_ref[...] + gt_ref[...] * y


def _resproj(a, w, x, mod, gate_idx, tm_pref, tn_pref, name):
    m, k = a.shape
    n = w.shape[1]
    tm = _tile(m, tm_pref, SUBLANES)
    tn = _tile(n, tn_pref)
    return pl.pallas_call(
        _resproj_kernel,
        out_shape=jax.ShapeDtypeStruct((m, n), F32),
        grid=(m // tm, n // tn),
        in_specs=[pl.BlockSpec((tm, k), lambda i, j: (i, 0)),
                  pl.BlockSpec((k, tn), lambda i, j: (0, j)),
                  pl.BlockSpec((tm, tn), lambda i, j: (i, j)),
                  pl.BlockSpec((1, tn), lambda i, j: (0, gate_idx * (n // tn) + j))],
        out_specs=pl.BlockSpec((tm, tn), lambda i, j: (i, j)),
        compiler_params=_cparams(("parallel", "arbitrary")),
        name=name,
    )(a, w, x, mod)


def _swiglu_kernel(h_ref, wg_ref, wu_ref, *refs):
    n_side = (len(refs) - 1) // 2
    o_ref = refs[n_side]
    h = h_ref[...]
    g = jnp.dot(h, wg_ref[...], preferred_element_type=F32)
    u = jnp.dot(h, wu_ref[...], preferred_element_type=F32)
    o_ref[...] = (g * _sigmoid(g) * u).astype(o_ref.dtype)
    _side_cast(refs[:n_side] + refs[n_side + 1:])


def _swiglu(h, w, side=()):
    m, k = h.shape
    dff = w.shape[1] // 2
    tm = _tile(m, 2048, SUBLANES)
    tn = _tile(dff, 512)
    nj = dff // tn
    side_in, side_out, side_shapes = _side_cast_specs(side, m // tm, nj)
    out = pl.pallas_call(
        _swiglu_kernel,
        out_shape=[jax.ShapeDtypeStruct((m, dff), BF16)] + side_shapes,
        grid=(m // tm, nj),
        in_specs=[pl.BlockSpec((tm, k), lambda i, j: (i, 0)),
                  pl.BlockSpec((k, tn), lambda i, j: (0, j)),
                  pl.BlockSpec((k, tn), lambda i, j: (0, nj + j))] + side_in,
        out_specs=[pl.BlockSpec((tm, tn), lambda i, j: (i, j))] + side_out,
        compiler_params=_cparams(("arbitrary" if side else "parallel", "arbitrary")),
        name="ffn_swiglu",
    )(h, w, w, *_side_arrays(side))
    return out if side else out[0]


def kernel(x, c, w_ada, b_ada, norm_mix_g, norm_ffn_g, w_in, fox_f_bias, hgrn_lb_table, hgrn_onorm_g,
           fox_q_norm_g, fox_k_norm_g, w_branch_a, w_branch_b, w_out, w_ffn_in, w_ffn_out):
    batch, seq, d = x.shape
    assert batch == 1 and w_ada.shape[0] == 1, "single sequence, single layer"
    hw = hgrn_onorm_g.shape[1] * hgrn_onorm_g.shape[2]
    fw = fox_q_norm_g.shape[1] * fox_q_norm_g.shape[2]
    nfh = fox_q_norm_g.shape[1]
    assert hgrn_onorm_g.shape[2] == HEAD_DIM and fox_q_norm_g.shape[2] == HEAD_DIM
    assert nfh <= LANES and hw == fw and seq % (2 * FOX_TK) == 0
    x2 = x.reshape(seq, d)

    f_col = 4 * hw + 3 * fw
    wt = jnp.swapaxes(w_in, 1, 2)[0]
    fbias = jnp.pad(fox_f_bias[0], (0, LANES - nfh)).reshape(1, LANES)

    mod = _ada_mod(c, w_ada[0], b_ada[0])

    h = _normmod(x2, norm_mix_g[0], mod, 0, 1)
    n_main = 2 * hw + 3 * fw
    ff, w_qf = _matmul(h, wt, f_col, LANES, F32, "in_proj_ff",
                       side=((wt, 0, 2 * hw),))
    pq, w_main = _matmul(h, w_qf, 0, hw, BF16, "in_proj_q", tn_pref=512,
                         side=((wt, 2 * hw, n_main),))
    hf, w_gates = _matmul(h, w_qf, hw, hw, F32, "in_proj_f", tn_pref=512,
                          side=((wt, f_col, nfh + 2 * d),))
    pm, w_out_bf, w_a_bf, w_b_bf = _matmul(h, w_main, 0, n_main, BF16, "in_proj_main",
                                           side=(w_out[0], w_branch_a[0], w_branch_b[0]))
    pg, w_ffn_in_bf = _matmul(h, w_gates, nfh, 2 * d, BF16, "in_proj_gates",
                              side=(w_ffn_in[0],))

    o_a = _hgrn(pq, pm, hf, hgrn_lb_table, hgrn_onorm_g[0], hw, 0, hw)

    qa, kta, r = _foxprep(pm, ff, fbias, fox_q_norm_g[0], fox_k_norm_g[0], fw, 2 * hw, 2 * hw + fw, FOX_TK)
    o_b = _fox(qa, kta, r, pm, 2 * hw + 2 * fw, FOX_TK)

    y = _merge(o_a, o_b, w_a_bf, w_b_bf, pg)
    x1 = _resproj(y, w_out_bf, x2, mod, 2, 1024, 512, "out_proj")

    h2 = _normmod(x1, norm_ffn_g[0], mod, 3, 4)
    act, w_ffn_out_bf = _swiglu(h2, w_ffn_in_bf, side=(w_ffn_out[0],))
    x3 = _resproj(act, w_ffn_out_bf, x1, mod, 5, 512, 512, "ffn_out")
    return x3.reshape(batch, seq, d)
```

```python
import functools
import math

import jax
import jax.numpy as jnp
from jax import lax
from jax.experimental import pallas as pl
from jax.experimental.pallas import tpu as pltpu

F32 = jnp.float32
BF16 = jnp.bfloat16

EPS = 1e-6
LOG2E = 1.4426950408889634
LANES = 128
SUBLANES = 8
BF16_SUBLANES = 16
HEAD_DIM = 128
HGRN_SUB = 16
HGRN_FSUB = 32
HGRN_UNROLL = 4
HGRN_MIN_LB = 2.0 ** -6
FOX_TK = 512
FOX_KDIM = 2 * HEAD_DIM
V7X_VMEM_LIMIT_BYTES = 56 * 1024 * 1024


def _cparams(semantics, vmem_bytes=V7X_VMEM_LIMIT_BYTES):
    return pltpu.CompilerParams(dimension_semantics=semantics, vmem_limit_bytes=vmem_bytes)


def _tile(dim, pref, align=LANES):
    if dim <= pref:
        return dim
    t = (pref // align) * align
    while t >= align:
        if dim % t == 0:
            return t
        t -= align
    raise ValueError(f"no {align}-aligned tile for {dim}")


def _sigmoid(x):
    return 1.0 / (1.0 + jnp.exp(-x))


def _ada_kernel(c_ref, w_ref, b_ref, o_ref):
    c = c_ref[...]
    s = c * _sigmoid(c)
    o_ref[...] = jnp.sum(w_ref[...] * s, axis=0, keepdims=True) + b_ref[...]


def _ada_mod(c_col, w_ada, b_ada, ncols):
    d = w_ada.shape[0]
    tn = _tile(ncols, 512)
    return pl.pallas_call(
        _ada_kernel,
        out_shape=jax.ShapeDtypeStruct((1, ncols), F32),
        grid=(ncols // tn,),
        in_specs=[pl.BlockSpec((d, 1), lambda j: (0, 0)),
                  pl.BlockSpec((d, tn), lambda j: (0, j)),
                  pl.BlockSpec((1, tn), lambda j: (0, j))],
        out_specs=pl.BlockSpec((1, tn), lambda j: (0, j)),
        compiler_params=_cparams(("parallel",)),
        name="ada_mod",
    )(c_col, w_ada, b_ada)


def _ada_side_specs(ada, n_i, n_j):
    c_col, w_ada, b_ada, col0, ncols = ada
    d = w_ada.shape[0]
    units = ncols // LANES
    n_blk = max(dv for dv in range(1, min(units, n_i * n_j) + 1) if units % dv == 0)
    cb = ncols // n_blk
    assert col0 % cb == 0
    blk = lambda i, j: jnp.minimum(i * n_j + j, n_blk - 1)
    in_specs = [pl.BlockSpec((d, 1), lambda i, j: (0, 0)),
                pl.BlockSpec((d, cb), lambda i, j: (0, col0 // cb + blk(i, j))),
                pl.BlockSpec((1, cb), lambda i, j: (0, col0 // cb + blk(i, j)))]
    out_spec = pl.BlockSpec((1, cb), lambda i, j: (0, blk(i, j)))
    return in_specs, out_spec, jax.ShapeDtypeStruct((1, ncols), F32)


def _normmod_kernel(x_ref, g_ref, sh_ref, sc_ref, o_ref):
    x = x_ref[...]
    ms = jnp.mean(x * x, axis=-1, keepdims=True)
    xn = x * lax.rsqrt(ms + EPS)
    o_ref[...] = (xn * g_ref[...] * (1.0 + sc_ref[...]) + sh_ref[...]).astype(o_ref.dtype)


def _normmod(x, gain, mod, shift_idx, scale_idx):
    s, d = x.shape
    tm = _tile(s, 256, SUBLANES)
    return pl.pallas_call(
        _normmod_kernel,
        out_shape=jax.ShapeDtypeStruct((s, d), BF16),
        grid=(s // tm,),
        in_specs=[pl.BlockSpec((tm, d), lambda i: (i, 0)),
                  pl.BlockSpec((1, d), lambda i: (0, 0)),
                  pl.BlockSpec((1, d), lambda i: (0, shift_idx)),
                  pl.BlockSpec((1, d), lambda i: (0, scale_idx))],
        out_specs=pl.BlockSpec((tm, d), lambda i: (i, 0)),
        compiler_params=_cparams(("parallel",)),
        name="normmod",
    )(x, gain.reshape(1, d), mod, mod)


def _normmod_proj_kernel(x_ref, g_ref, sh_ref, sc_ref, wt_ref, *refs):
    n_side = (len(refs) - 2) // 2
    h_ref, o_ref = refs[n_side], refs[n_side + 1]
    _normmod_kernel(x_ref, g_ref, sh_ref, sc_ref, h_ref)
    o_ref[...] = lax.dot_general(h_ref[...], wt_ref[...].astype(BF16), (((1,), (1,)), ((), ())),
                                 preferred_element_type=F32)
    _side_cast(refs[:n_side] + refs[n_side + 2:])


def _normmod_proj(x, gain, mod, shift_idx, scale_idx, wt, row0, nrows, side):
    s, d = x.shape
    tm = _tile(s, 512, SUBLANES)
    n_i = s // tm
    side_in, side_out, side_shapes = _side_cast_specs(side, n_i, 1)
    return pl.pallas_call(
        _normmod_proj_kernel,
        out_shape=[jax.ShapeDtypeStruct((s, d), BF16), jax.ShapeDtypeStruct((s, nrows), F32)] + side_shapes,
        grid=(n_i, 1),
        in_specs=[pl.BlockSpec((tm, d), lambda i, j: (i, 0)),
                  pl.BlockSpec((1, d), lambda i, j: (0, 0)),
                  pl.BlockSpec((1, d), lambda i, j: (0, shift_idx)),
                  pl.BlockSpec((1, d), lambda i, j: (0, scale_idx)),
                  pl.BlockSpec((pl.Element(nrows), pl.Element(d)), lambda i, j: (row0, 0))] + side_in,
        out_specs=[pl.BlockSpec((tm, d), lambda i, j: (i, 0)),
                   pl.BlockSpec((tm, nrows), lambda i, j: (i, 0))] + side_out,
        compiler_params=_cparams(("arbitrary", "arbitrary")),
        name="normmod_proj_ff",
    )(x, gain.reshape(1, d), mod, mod, wt, *_side_arrays(side))


def _side_cast_specs(side, n_i, n_j):
    in_specs, out_specs, shapes = [], [], []
    for item in side:
        w, row0, rows = item if isinstance(item, tuple) else (item, 0, item.shape[0])
        cols = w.shape[1]
        assert rows % BF16_SUBLANES == 0 and row0 % BF16_SUBLANES == 0, (row0, rows)
        units = rows // BF16_SUBLANES
        n_slabs = max(dv for dv in range(1, min(units, n_i * n_j) + 1) if units % dv == 0)
        slab = rows // n_slabs

        def slab_index(i, j, n_slabs=n_slabs):
            return jnp.minimum(i * n_j + j, n_slabs - 1)

        in_specs.append(pl.BlockSpec(
            (pl.Element(slab), pl.Element(cols)),
            lambda i, j, f=slab_index, row0=row0, slab=slab:
                (pl.multiple_of(row0 + f(i, j) * slab, BF16_SUBLANES), 0)))
        out_specs.append(pl.BlockSpec((slab, cols), lambda i, j, f=slab_index: (f(i, j), 0)))
        shapes.append(jax.ShapeDtypeStruct((rows, cols), BF16))
    return in_specs, out_specs, shapes


def _side_cast(side_refs):
    n = len(side_refs) // 2
    for src, dst in zip(side_refs[:n], side_refs[n:]):
        dst[...] = src[...].astype(dst.dtype)


def _mm_kernel(x_ref, wt_ref, *refs, with_ada):
    n_ada = 3 if with_ada else 0
    n_side = (len(refs) - n_ada - 1 - (1 if with_ada else 0)) // 2
    ada_in, side_in = refs[:n_ada], refs[n_ada:n_ada + n_side]
    outs = refs[n_ada + n_side:]
    o_ref = outs[0]
    y = lax.dot_general(x_ref[...], wt_ref[...].astype(BF16), (((1,), (1,)), ((), ())),
                        preferred_element_type=F32)
    o_ref[...] = y.astype(o_ref.dtype)
    if with_ada:
        _ada_kernel(*ada_in, outs[1])
    _side_cast(side_in + outs[1 + (1 if with_ada else 0):])


def _side_arrays(side):
    return [item[0] if isinstance(item, tuple) else item for item in side]


def _matmul(x, wt, row0, nrows, out_dtype, name, side=(), tn_pref=1024, ada=None):
    m, k = x.shape
    assert row0 % BF16_SUBLANES == 0
    tm = _tile(m, 1024, SUBLANES)
    tn = _tile(nrows, tn_pref)
    n_i, n_j = m // tm, nrows // tn
    side_in, side_out, side_shapes = _side_cast_specs(side, n_i, n_j)
    ada_in, ada_out, ada_shape, ada_args = [], [], [], []
    if ada is not None:
        ada_in, out_spec, shape = _ada_side_specs(ada, n_i, n_j)
        ada_out, ada_shape, ada_args = [out_spec], [shape], list(ada[:3])
    has_jobs = bool(side) or ada is not None
    out = pl.pallas_call(
        functools.partial(_mm_kernel, with_ada=ada is not None),
        out_shape=[jax.ShapeDtypeStruct((m, nrows), out_dtype)] + ada_shape + side_shapes,
        grid=(n_i, n_j),
        in_specs=[pl.BlockSpec((tm, k), lambda i, j: (i, 0)),
                  pl.BlockSpec((pl.Element(tn), pl.Element(k)),
                               lambda i, j: (pl.multiple_of(row0 + j * tn, BF16_SUBLANES), 0))]
                 + ada_in + side_in,
        out_specs=[pl.BlockSpec((tm, tn), lambda i, j: (i, j))] + ada_out + side_out,
        compiler_params=_cparams(("arbitrary" if has_jobs else "parallel", "arbitrary")),
        name=name,
    )(x, wt, *ada_args, *_side_arrays(side))
    return out if has_jobs else out[0]


def _hgrn_kernel(q_ref, i_ref, g_ref, f_ref, lbt_ref, og_ref, o_ref, st_ref, *, hb, tt):
    @pl.when(pl.program_id(1) == 0)
    def _():
        st_ref[...] = jnp.zeros_like(st_ref)

    tbl = lbt_ref[...]
    e = jnp.exp(tbl - jnp.max(tbl, axis=0, keepdims=True))
    lb = e[0:1, :] / jnp.sum(e, axis=0, keepdims=True)
    half = SUBLANES
    nt_dims = (((1,), (1,)), ((), ()))
    tn_dims = (((0,), (0,)), ((), ()))

    def gates(rows, cs, nrows):
        lbh = lb[:, cs]
        f = lbh + (1.0 - lbh) * _sigmoid(f_ref[rows, cs])
        b = jnp.log2(f)
        row = lax.broadcasted_iota(jnp.int32, (nrows, LANES), 0)
        sh = 1
        while sh < nrows:
            b = b + jnp.where(row >= sh, pltpu.roll(b, sh, axis=0), 0.0)
            sh *= 2
        return 1.0 - f, b

    def finish(o, rows, cs):
        ms = jnp.mean(o * o, axis=-1, keepdims=True)
        on = o * lax.rsqrt(ms + EPS) * og_ref[:, cs]
        gate = g_ref[rows, cs].astype(F32)
        o_ref[rows, cs] = (on * (gate * _sigmoid(gate))).astype(o_ref.dtype)

    mid = HGRN_FSUB // 2
    tri_r = lax.broadcasted_iota(jnp.int32, (HGRN_FSUB, HGRN_FSUB), 0)
    tri_c = lax.broadcasted_iota(jnp.int32, (HGRN_FSUB, HGRN_FSUB), 1)

    def body_factored(n, carry):
        pending = []
        for u, h in [(u, h) for u in range(HGRN_UNROLL) for h in range(hb)]:
            r0 = pl.multiple_of((n * HGRN_UNROLL + u) * HGRN_FSUB, HGRN_FSUB)
            rows = pl.ds(r0, HGRN_FSUB)
            cs = slice(h * LANES, (h + 1) * LANES)
            q = q_ref[rows, cs].astype(F32)
            v = i_ref[rows, cs]
            k, b = gates(rows, cs, HGRN_FSUB)
            r = b[mid - 1:mid, :]
            b_last = b[HGRN_FSUB - 1:HGRN_FSUB, :]
            qc = q * jnp.exp2(b - r)
            kc = (k * jnp.exp2(r - b)).astype(BF16)
            a = lax.dot_general(qc.astype(BF16), kc, nt_dims, preferred_element_type=F32)
            st = st_ref[h]
            o = lax.dot_general((qc * jnp.exp2(r)).astype(BF16), st.astype(BF16), nt_dims,
                                preferred_element_type=F32)
            kt = (k * jnp.exp2(b_last - b)).astype(BF16)
            upd = lax.dot_general(v, kt, tn_dims, preferred_element_type=F32)
            st_ref[h] = st * jnp.exp2(b_last) + upd
            pending.append((a, o, v, rows, cs))
        for a, o, v, rows, cs in pending:
            p = jnp.where(tri_c <= tri_r, a, 0.0).astype(BF16)
            finish(o + jnp.dot(p, v, preferred_element_type=F32), rows, cs)
        return carry

    rowc = lax.broadcasted_iota(jnp.int32, (SUBLANES, 1), 0)

    def body_exact(n, carry):
        r0 = pl.multiple_of(n * HGRN_SUB, HGRN_SUB)
        rows = pl.ds(r0, HGRN_SUB)
        for h in range(hb):
            cs = slice(h * LANES, (h + 1) * LANES)
            q = q_ref[rows, cs].astype(F32)
            v = i_ref[rows, cs]
            vf = v.astype(F32)
            k, b = gates(rows, cs, HGRN_SUB)
            b_last = b[HGRN_SUB - 1:HGRN_SUB, :]
            st = st_ref[h]
            qt = (q * jnp.exp2(b)).astype(BF16)
            o = lax.dot_general(qt, st.astype(BF16), nt_dims, preferred_element_type=F32)
            o_lo, o_hi = o[:half], o[half:]
            q_lo, q_hi = q[:half], q[half:]
            b_lo, b_hi = b[:half], b[half:]
            for s in range(HGRN_SUB):
                ks, bs, vs = k[s:s + 1, :], b[s:s + 1, :], vf[s:s + 1, :]
                c_hi = jnp.sum(q_hi * ks * jnp.exp2(b_hi - bs), axis=-1, keepdims=True)
                if s < half:
                    c_lo = jnp.sum(q_lo * ks * jnp.exp2(b_lo - bs), axis=-1, keepdims=True)
                    o_lo = o_lo + jnp.where(rowc >= s, c_lo, 0.0) * vs
                else:
                    c_hi = jnp.where(rowc >= s - half, c_hi, 0.0)
                o_hi = o_hi + c_hi * vs
            kt = (k * jnp.exp2(b_last - b)).astype(BF16)
            upd = lax.dot_general(v, kt, tn_dims, preferred_element_type=F32)
            st_ref[h] = st * jnp.exp2(b_last) + upd
            finish(jnp.concatenate([o_lo, o_hi], axis=0), rows, cs)
        return carry

    factorable = jnp.min(lb) >= HGRN_MIN_LB

    @pl.when(factorable)
    def _():
        lax.fori_loop(0, tt // (HGRN_FSUB * HGRN_UNROLL), body_factored, 0)

    @pl.when(jnp.logical_not(factorable))
    def _():
        lax.fori_loop(0, tt // HGRN_SUB, body_exact, 0)


def _hgrn(pq, pm, hf, lb_table, onorm_g, width, i_col, g_col):
    s = pq.shape[0]
    hb = min(16, width // HEAD_DIM)
    bw = hb * HEAD_DIM
    tt = _tile(s, 512, HGRN_SUB)
    nl = lb_table.shape[0]
    kern = functools.partial(_hgrn_kernel, hb=hb, tt=tt)
    return pl.pallas_call(
        kern,
        out_shape=jax.ShapeDtypeStruct((s, width), BF16),
        grid=(width // bw, s // tt),
        in_specs=[pl.BlockSpec((tt, bw), lambda h, t: (t, h)),
                  pl.BlockSpec((tt, bw), lambda h, t: (t, i_col // bw + h)),
                  pl.BlockSpec((tt, bw), lambda h, t: (t, g_col // bw + h)),
                  pl.BlockSpec((tt, bw), lambda h, t: (t, h)),
                  pl.BlockSpec((nl, bw), lambda h, t: (0, h)),
                  pl.BlockSpec((1, bw), lambda h, t: (0, h))],
        out_specs=pl.BlockSpec((tt, bw), lambda h, t: (t, h)),
        scratch_shapes=[pltpu.VMEM((hb, HEAD_DIM, HEAD_DIM), F32)],
        compiler_params=_cparams(("parallel", "arbitrary")),
        name="hgrn2",
    )(pq, pm, pm, hf, lb_table, onorm_g.reshape(1, width))


def _split3(x):
    hi = x.astype(BF16).astype(F32)
    r = x - hi
    mid = r.astype(BF16).astype(F32)
    lo = (r - mid).astype(BF16).astype(F32)
    return hi, mid, lo


def _foxprep_kernel(q_ref, k_ref, ff_ref, fb_ref, gq_ref, gk_ref, qo_ref, kto_ref, r_ref, carry_ref, *, nh, scale):
    @pl.when(pl.program_id(0) == 0)
    def _():
        carry_ref[...] = jnp.zeros_like(carry_ref)

    tk = q_ref.shape[0]
    x = ff_ref[...] + fb_ref[...]
    ls = (jnp.minimum(x, 0.0) - jnp.log(1.0 + jnp.exp(-jnp.abs(x)))) * LOG2E
    ri = lax.broadcasted_iota(jnp.int32, (tk, tk), 0)
    ci = lax.broadcasted_iota(jnp.int32, (tk, tk), 1)
    lower = (ci <= ri).astype(F32)
    rel = jnp.dot(lower, ls, precision=lax.Precision.HIGHEST, preferred_element_type=F32)
    base = carry_ref[...]
    r_ref[0] = base
    carry_ref[...] = base + rel[tk - 1:tk, :]

    cq = _split3(rel)
    ck = _split3(rel.T)
    lane = lax.broadcasted_iota(jnp.int32, (tk, LANES), 1)
    sub = lax.broadcasted_iota(jnp.int32, (LANES, tk), 0)
    ones_q = jnp.where((lane >= 3) & (lane < 6), 1.0, 0.0)
    ones_k = jnp.where(sub < 3, 1.0, 0.0)
    for h in range(nh):
        cs = slice(h * LANES, (h + 1) * LANES)
        q = q_ref[:, cs].astype(F32)
        qn = q * lax.rsqrt(jnp.mean(q * q, axis=-1, keepdims=True) + EPS) * gq_ref[:, cs]
        k = k_ref[:, cs].astype(F32)
        kn = k * lax.rsqrt(jnp.mean(k * k, axis=-1, keepdims=True) + EPS) * gk_ref[:, cs]
        eq = ones_q
        ek = ones_k
        for t in range(3):
            eq = jnp.where(lane == t, cq[t][:, h:h + 1], eq)
            ek = jnp.where(sub == 3 + t, -ck[t][h:h + 1, :], ek)
        c0 = h * FOX_KDIM
        qo_ref[:, c0:c0 + LANES] = (qn * scale).astype(qo_ref.dtype)
        qo_ref[:, c0 + LANES:c0 + FOX_KDIM] = eq.astype(qo_ref.dtype)
        kto_ref[0, c0:c0 + LANES, :] = kn.T.astype(kto_ref.dtype)
        kto_ref[0, c0 + LANES:c0 + FOX_KDIM, :] = ek.astype(kto_ref.dtype)


def _foxprep(pm, ff, fbias, gq, gk, width, q_col, k_col, tk):
    s = pm.shape[0]
    nh = width // HEAD_DIM
    nb = s // tk
    kern = functools.partial(_foxprep_kernel, nh=nh, scale=LOG2E * float(HEAD_DIM) ** -0.5)
    qa, kta, r = pl.pallas_call(
        kern,
        out_shape=(jax.ShapeDtypeStruct((s, nh * FOX_KDIM), BF16),
                   jax.ShapeDtypeStruct((nb, nh * FOX_KDIM, tk), BF16),
                   jax.ShapeDtypeStruct((nb, 1, LANES), F32)),
        grid=(nb,),
        in_specs=[pl.BlockSpec((tk, width), lambda i: (i, q_col // width)),
                  pl.BlockSpec((tk, width), lambda i: (i, k_col // width)),
                  pl.BlockSpec((tk, LANES), lambda i: (i, 0)),
                  pl.BlockSpec((1, LANES), lambda i: (0, 0)),
                  pl.BlockSpec((1, width), lambda i: (0, 0)),
                  pl.BlockSpec((1, width), lambda i: (0, 0))],
        out_specs=(pl.BlockSpec((tk, nh * FOX_KDIM), lambda i: (i, 0)),
                   pl.BlockSpec((1, nh * FOX_KDIM, tk), lambda i: (i, 0, 0)),
                   pl.BlockSpec((1, 1, LANES), lambda i: (i, 0, 0))),
        scratch_shapes=[pltpu.VMEM((1, LANES), F32)],
        compiler_params=_cparams(("arbitrary",)),
        name="fox_prep",
    )(pm, pm, ff, fbias, gq.reshape(1, width), gk.reshape(1, width))
    return qa, kta, r.reshape(nb, LANES)


def _fox_kernel(r_ref, q_ref, kt_ref, v_ref, o_ref, v2_ref, s_ref, m_ref, acc_ref, *, tk):
    h = pl.program_id(0)
    qi = pl.program_id(1)

    @pl.when(qi == 0)
    def _():
        v2_ref[:, :LANES] = v_ref[...]
        lane = lax.broadcasted_iota(jnp.int32, v_ref.shape, 1)
        v2_ref[:, LANES:] = jnp.where(lane == 0, 1.0, 0.0).astype(v2_ref.dtype)

    m_ref[...] = jnp.full(m_ref.shape, -jnp.inf, F32)
    acc_ref[...] = jnp.zeros(acc_ref.shape, F32)
    rr = lax.broadcasted_iota(jnp.int32, (tk, tk), 0)
    cc = lax.broadcasted_iota(jnp.int32, (tk, tk), 1)

    def scores(c, j, slot):
        s_ref[slot, c] = jnp.dot(q_ref[c * tk:(c + 1) * tk, :], kt_ref[j], preferred_element_type=F32)

    def fold(c, j, slot, diag):
        s = s_ref[slot, c]
        if diag:
            s = jnp.where(cc <= rr, s, -jnp.inf)
        d = r_ref[2 * qi + c, h] - r_ref[j, h]
        m = m_ref[c]
        m_new = jnp.maximum(m, jnp.max(s, axis=-1, keepdims=True) + d)
        p = jnp.exp2(s - (m_new - d))
        r0 = pl.multiple_of(j * tk, tk)
        pv = jnp.dot(p.astype(BF16), v2_ref[pl.ds(r0, tk), :], preferred_element_type=F32)
        acc_ref[c] = jnp.exp2(m - m_new) * acc_ref[c] + pv
        m_ref[c] = m_new

    scores(0, 0, 0)
    scores(1, 0, 0)

    def pair(j):
        for slot in range(2):
            scores(0, j + slot + 1, 1 - slot)
            scores(1, j + slot + 1, 1 - slot)
            fold(0, j + slot, slot, False)
            fold(1, j + slot, slot, False)

    def body(jp, carry):
        pair(2 * jp)
        return carry

    lax.fori_loop(0, qi, body, 0)
    scores(1, 2 * qi + 1, 1)
    fold(0, 2 * qi, 0, True)
    fold(1, 2 * qi, 0, False)
    fold(1, 2 * qi + 1, 1, True)
    for c in range(2):
        acc = acc_ref[c]
        o_ref[c * tk:(c + 1) * tk, :] = (acc[:, :LANES] / acc[:, LANES:LANES + 1]).astype(o_ref.dtype)


def _fox(qa, kta, r, pm, v_col, tk):
    s = qa.shape[0]
    nh = qa.shape[1] // FOX_KDIM
    nb = s // tk
    tq = 2 * tk
    kern = functools.partial(_fox_kernel, tk=tk)
    return pl.pallas_call(
        kern,
        out_shape=jax.ShapeDtypeStruct((s, nh * HEAD_DIM), BF16),
        grid=(nh, s // tq),
        in_specs=[pl.BlockSpec(memory_space=pltpu.SMEM),
                  pl.BlockSpec((tq, FOX_KDIM), lambda h, i: (i, h)),
                  pl.BlockSpec((nb, FOX_KDIM, tk), lambda h, i: (0, h, 0)),
                  pl.BlockSpec((s, HEAD_DIM), lambda h, i: (0, v_col // HEAD_DIM + h))],
        out_specs=pl.BlockSpec((tq, HEAD_DIM), lambda h, i: (i, h)),
        scratch_shapes=[pltpu.VMEM((s, FOX_KDIM), BF16),
                        pltpu.VMEM((2, 2, tk, tk), F32),
                        pltpu.VMEM((2, tk, 1), F32),
                        pltpu.VMEM((2, tk, FOX_KDIM), F32)],
        compiler_params=_cparams(("parallel", "arbitrary")),
        name="fox_attn",
    )(r, qa, kta, pm)


def _merge_kernel(oa_ref, ob_ref, wa_ref, wb_ref, ga_ref, gb_ref, y_ref):
    sa = _sigmoid(ga_ref[...].astype(F32))
    sb = _sigmoid(gb_ref[...].astype(F32))
    ya = jnp.dot(oa_ref[...], wa_ref[...], preferred_element_type=F32)
    yb = jnp.dot(ob_ref[...], wb_ref[...], preferred_element_type=F32)
    y_ref[...] = (sa * ya + sb * yb).astype(y_ref.dtype)


def _merge(oa, ob, wa, wb, pg):
    m, ka = oa.shape
    kb = ob.shape[1]
    n = wa.shape[1]
    tm = _tile(m, 1024, SUBLANES)
    tn = _tile(n, 512)
    return pl.pallas_call(
        _merge_kernel,
        out_shape=jax.ShapeDtypeStruct((m, n), BF16),
        grid=(m // tm, n // tn),
        in_specs=[pl.BlockSpec((tm, ka), lambda i, j: (i, 0)),
                  pl.BlockSpec((tm, kb), lambda i, j: (i, 0)),
                  pl.BlockSpec((ka, tn), lambda i, j: (0, j)),
                  pl.BlockSpec((kb, tn), lambda i, j: (0, j)),
                  pl.BlockSpec((tm, tn), lambda i, j: (i, j)),
                  pl.BlockSpec((tm, tn), lambda i, j: (i, n // tn + j))],
        out_specs=pl.BlockSpec((tm, tn), lambda i, j: (i, j)),
        compiler_params=_cparams(("parallel", "arbitrary")),
        name="branch_merge",
    )(oa, ob, wa, wb, pg, pg)


def _resproj_kernel(a_ref, w_ref, x_ref, gt_ref, o_ref):
    y = jnp.dot(a_ref[...], w_ref[...], preferred_element_type=F32)
    o_ref[...] = x_ref[...] + gt_ref[...] * y


def _resproj(a, w, x, mod, gate_idx, tm_pref, tn_pref, name):
    m, k = a.shape
    n = w.shape[1]
    tm = _tile(m, tm_pref, SUBLANES)
    tn = _tile(n, tn_pref)
    return pl.pallas_call(
        _resproj_kernel,
        out_shape=jax.ShapeDtypeStruct((m, n), F32),
        grid=(m // tm, n // tn),
        in_specs=[pl.BlockSpec((tm, k), lambda i, j: (i, 0)),
                  pl.BlockSpec((k, tn), lambda i, j: (0, j)),
                  pl.BlockSpec((tm, tn), lambda i, j: (i, j)),
                  pl.BlockSpec((1, tn), lambda i, j: (0, gate_idx * (n // tn) + j))],
        out_specs=pl.BlockSpec((tm, tn), lambda i, j: (i, j)),
        compiler_params=_cparams(("parallel", "arbitrary")),
        name=name,
    )(a, w, x, mod)


def _swiglu_kernel(h_ref, wg_ref, wu_ref, *refs):
    n_side = (len(refs) - 1) // 2
    o_ref = refs[n_side]
    h = h_ref[...]
    g = jnp.dot(h, wg_ref[...], preferred_element_type=F32)
    u = jnp.dot(h, wu_ref[...], preferred_element_type=F32)
    o_ref[...] = (g * _sigmoid(g) * u).astype(o_ref.dtype)
    _side_cast(refs[:n_side] + refs[n_side + 1:])


def _swiglu(h, w, side=()):
    m, k = h.shape
    dff = w.shape[1] // 2
    tm = _tile(m, 2048, SUBLANES)
    tn = _tile(dff, 512)
    nj = dff // tn
    side_in, side_out, side_shapes = _side_cast_specs(side, m // tm, nj)
    out = pl.pallas_call(
        _swiglu_kernel,
        out_shape=[jax.ShapeDtypeStruct((m, dff), BF16)] + side_shapes,
        grid=(m // tm, nj),
        in_specs=[pl.BlockSpec((tm, k), lambda i, j: (i, 0)),
                  pl.BlockSpec((k, tn), lambda i, j: (0, j)),
                  pl.BlockSpec((k, tn), lambda i, j: (0, nj + j))] + side_in,
        out_specs=[pl.BlockSpec((tm, tn), lambda i, j: (i, j))] + side_out,
        compiler_params=_cparams(("arbitrary" if side else "parallel", "arbitrary")),
        name="ffn_swiglu",
    )(h, w, w, *_side_arrays(side))
    return out if side else out[0]


def kernel(x, c, w_ada, b_ada, norm_mix_g, norm_ffn_g, w_in, fox_f_bias, hgrn_lb_table, hgrn_onorm_g,
           fox_q_norm_g, fox_k_norm_g, w_branch_a, w_branch_b, w_out, w_ffn_in, w_ffn_out):
    batch, seq, d = x.shape
    assert batch == 1 and w_ada.shape[0] == 1, "single sequence, single layer"
    hw = hgrn_onorm_g.shape[1] * hgrn_onorm_g.shape[2]
    fw = fox_q_norm_g.shape[1] * fox_q_norm_g.shape[2]
    nfh = fox_q_norm_g.shape[1]
    assert hgrn_onorm_g.shape[2] == HEAD_DIM and fox_q_norm_g.shape[2] == HEAD_DIM
    assert nfh <= LANES and hw == fw and seq % (2 * FOX_TK) == 0
    x2 = x.reshape(seq, d)

    f_col = 4 * hw + 3 * fw
    wt = jnp.swapaxes(w_in, 1, 2)[0]
    fbias = jnp.pad(fox_f_bias[0], (0, LANES - nfh)).reshape(1, LANES)

    c_col, w_ada2, b_ada2 = c.reshape(d, 1), w_ada[0], b_ada.reshape(1, b_ada.shape[-1])
    mod_a = _ada_mod(c_col, w_ada2, b_ada2, 2 * d)

    n_main = 2 * hw + 3 * fw
    h, ff, w_qf = _normmod_proj(x2, norm_mix_g[0], mod_a, 0, 1, wt, f_col, LANES,
                                side=((wt, 0, 2 * hw),))
    pq, w_main = _matmul(h, w_qf, 0, hw, BF16, "in_proj_q", tn_pref=512,
                         side=((wt, 2 * hw, n_main),))
    hf, w_gates = _matmul(h, w_qf, hw, hw, F32, "in_proj_f", tn_pref=512,
                          side=((wt, f_col, nfh + 2 * d),))
    pm, mod_b, w_out_bf, w_a_bf, w_b_bf = _matmul(
        h, w_main, 0, n_main, BF16, "in_proj_main", ada=(c_col, w_ada2, b_ada2, 2 * d, 4 * d),
        side=(w_out[0], w_branch_a[0], w_branch_b[0]))
    pg, w_ffn_in_bf = _matmul(h, w_gates, nfh, 2 * d, BF16, "in_proj_gates",
                              side=(w_ffn_in[0],))

    o_a = _hgrn(pq, pm, hf, hgrn_lb_table, hgrn_onorm_g[0], hw, 0, hw)

    qa, kta, r = _foxprep(pm, ff, fbias, fox_q_norm_g[0], fox_k_norm_g[0], fw, 2 * hw, 2 * hw + fw, FOX_TK)
    o_b = _fox(qa, kta, r, pm, 2 * hw + 2 * fw, FOX_TK)

    y = _merge(o_a, o_b, w_a_bf, w_b_bf, pg)
    x1 = _resproj(y, w_out_bf, x2, mod_b, 0, 1024, 512, "out_proj")

    h2 = _normmod(x1, norm_ffn_g[0], mod_b, 1, 2)
    act, w_ffn_out_bf = _swiglu(h2, w_ffn_in_bf, side=(w_ffn_out[0],))
    x3 = _resproj(act, w_ffn_out_bf, x1, mod_b, 3, 512, 512, "ffn_out")
    return x3.reshape(batch, seq, d)
```

```python
import functools
import math

import jax
import jax.numpy as jnp
from jax import lax
from jax.experimental import pallas as pl
from jax.experimental.pallas import tpu as pltpu

F32 = jnp.float32
BF16 = jnp.bfloat16

EPS = 1e-6
LOG2E = 1.4426950408889634
LANES = 128
SUBLANES = 8
BF16_SUBLANES = 16
HEAD_DIM = 128
HGRN_SUB = 16
HGRN_FSUB = 32
HGRN_UNROLL = 4
HGRN_MIN_LB = 2.0 ** -6
FOX_TK = 512
FOX_KDIM = 2 * HEAD_DIM
V7X_VMEM_LIMIT_BYTES = 56 * 1024 * 1024


def _cparams(semantics, vmem_bytes=V7X_VMEM_LIMIT_BYTES):
    return pltpu.CompilerParams(dimension_semantics=semantics, vmem_limit_bytes=vmem_bytes)


def _tile(dim, pref, align=LANES):
    if dim <= pref:
        return dim
    t = (pref // align) * align
    while t >= align:
        if dim % t == 0:
            return t
        t -= align
    raise ValueError(f"no {align}-aligned tile for {dim}")


def _sigmoid(x):
    return 1.0 / (1.0 + jnp.exp(-x))


def _ada_kernel(c_ref, w_ref, b_ref, o_ref):
    c = c_ref[...]
    s = c * _sigmoid(c)
    o_ref[...] = jnp.sum(w_ref[...] * s, axis=0, keepdims=True) + b_ref[...]


def _ada_mod(c_col, w_ada, b_ada, ncols):
    d = w_ada.shape[0]
    tn = _tile(ncols, 512)
    return pl.pallas_call(
        _ada_kernel,
        out_shape=jax.ShapeDtypeStruct((1, ncols), F32),
        grid=(ncols // tn,),
        in_specs=[pl.BlockSpec((d, 1), lambda j: (0, 0)),
                  pl.BlockSpec((d, tn), lambda j: (0, j)),
                  pl.BlockSpec((1, tn), lambda j: (0, j))],
        out_specs=pl.BlockSpec((1, tn), lambda j: (0, j)),
        compiler_params=_cparams(("parallel",)),
        name="ada_mod",
    )(c_col, w_ada, b_ada)


def _ada_side_specs(ada, n_i, n_j):
    c_col, w_ada, b_ada, col0, ncols = ada
    d = w_ada.shape[0]
    units = ncols // LANES
    n_blk = max(dv for dv in range(1, min(units, n_i * n_j) + 1) if units % dv == 0)
    cb = ncols // n_blk
    assert col0 % cb == 0
    blk = lambda i, j: jnp.minimum(i * n_j + j, n_blk - 1)
    in_specs = [pl.BlockSpec((d, 1), lambda i, j: (0, 0)),
                pl.BlockSpec((d, cb), lambda i, j: (0, col0 // cb + blk(i, j))),
                pl.BlockSpec((1, cb), lambda i, j: (0, col0 // cb + blk(i, j)))]
    out_spec = pl.BlockSpec((1, cb), lambda i, j: (0, blk(i, j)))
    return in_specs, out_spec, jax.ShapeDtypeStruct((1, ncols), F32)


def _normmod_kernel(x_ref, g_ref, sh_ref, sc_ref, o_ref):
    x = x_ref[...]
    ms = jnp.mean(x * x, axis=-1, keepdims=True)
    xn = x * lax.rsqrt(ms + EPS)
    o_ref[...] = (xn * g_ref[...] * (1.0 + sc_ref[...]) + sh_ref[...]).astype(o_ref.dtype)


def _normmod(x, gain, mod, shift_idx, scale_idx):
    s, d = x.shape
    tm = _tile(s, 512, SUBLANES)
    return pl.pallas_call(
        _normmod_kernel,
        out_shape=jax.ShapeDtypeStruct((s, d), BF16),
        grid=(s // tm,),
        in_specs=[pl.BlockSpec((tm, d), lambda i: (i, 0)),
                  pl.BlockSpec((1, d), lambda i: (0, 0)),
                  pl.BlockSpec((1, d), lambda i: (0, shift_idx)),
                  pl.BlockSpec((1, d), lambda i: (0, scale_idx))],
        out_specs=pl.BlockSpec((tm, d), lambda i: (i, 0)),
        compiler_params=_cparams(("parallel",)),
        name="normmod",
    )(x, gain.reshape(1, d), mod, mod)


def _normmod_proj_kernel(x_ref, g_ref, sh_ref, sc_ref, wt_ref, *refs):
    n_side = (len(refs) - 2) // 2
    h_ref, o_ref = refs[n_side], refs[n_side + 1]
    _normmod_kernel(x_ref, g_ref, sh_ref, sc_ref, h_ref)
    o_ref[...] = lax.dot_general(h_ref[...], wt_ref[...].astype(BF16), (((1,), (1,)), ((), ())),
                                 preferred_element_type=F32)
    _side_cast(refs[:n_side] + refs[n_side + 2:])


def _normmod_proj(x, gain, mod, shift_idx, scale_idx, wt, row0, nrows, side):
    s, d = x.shape
    tm = _tile(s, 512, SUBLANES)
    n_i = s // tm
    side_in, side_out, side_shapes = _side_cast_specs(side, n_i, 1)
    return pl.pallas_call(
        _normmod_proj_kernel,
        out_shape=[jax.ShapeDtypeStruct((s, d), BF16), jax.ShapeDtypeStruct((s, nrows), F32)] + side_shapes,
        grid=(n_i, 1),
        in_specs=[pl.BlockSpec((tm, d), lambda i, j: (i, 0)),
                  pl.BlockSpec((1, d), lambda i, j: (0, 0)),
                  pl.BlockSpec((1, d), lambda i, j: (0, shift_idx)),
                  pl.BlockSpec((1, d), lambda i, j: (0, scale_idx)),
                  pl.BlockSpec((pl.Element(nrows), pl.Element(d)), lambda i, j: (row0, 0))] + side_in,
        out_specs=[pl.BlockSpec((tm, d), lambda i, j: (i, 0)),
                   pl.BlockSpec((tm, nrows), lambda i, j: (i, 0))] + side_out,
        compiler_params=_cparams(("arbitrary", "arbitrary")),
        name="normmod_proj_ff",
    )(x, gain.reshape(1, d), mod, mod, wt, *_side_arrays(side))


def _side_cast_specs(side, n_i, n_j):
    in_specs, out_specs, shapes = [], [], []
    for item in side:
        w, row0, rows = item if isinstance(item, tuple) else (item, 0, item.shape[0])
        cols = w.shape[1]
        assert rows % BF16_SUBLANES == 0 and row0 % BF16_SUBLANES == 0, (row0, rows)
        units = rows // BF16_SUBLANES
        n_slabs = max(dv for dv in range(1, min(units, n_i * n_j) + 1) if units % dv == 0)
        slab = rows // n_slabs

        def slab_index(i, j, n_slabs=n_slabs):
            return jnp.minimum(i * n_j + j, n_slabs - 1)

        in_specs.append(pl.BlockSpec(
            (pl.Element(slab), pl.Element(cols)),
            lambda i, j, f=slab_index, row0=row0, slab=slab:
                (pl.multiple_of(row0 + f(i, j) * slab, BF16_SUBLANES), 0)))
        out_specs.append(pl.BlockSpec((slab, cols), lambda i, j, f=slab_index: (f(i, j), 0)))
        shapes.append(jax.ShapeDtypeStruct((rows, cols), BF16))
    return in_specs, out_specs, shapes


def _side_cast(side_refs):
    n = len(side_refs) // 2
    for src, dst in zip(side_refs[:n], side_refs[n:]):
        dst[...] = src[...].astype(dst.dtype)


def _mm_kernel(x_ref, wt_ref, *refs, with_ada):
    n_ada = 3 if with_ada else 0
    n_side = (len(refs) - n_ada - 1 - (1 if with_ada else 0)) // 2
    ada_in, side_in = refs[:n_ada], refs[n_ada:n_ada + n_side]
    outs = refs[n_ada + n_side:]
    o_ref = outs[0]
    y = lax.dot_general(x_ref[...], wt_ref[...].astype(BF16), (((1,), (1,)), ((), ())),
                        preferred_element_type=F32)
    o_ref[...] = y.astype(o_ref.dtype)
    if with_ada:
        _ada_kernel(*ada_in, outs[1])
    _side_cast(side_in + outs[1 + (1 if with_ada else 0):])


def _side_arrays(side):
    return [item[0] if isinstance(item, tuple) else item for item in side]


def _matmul(x, wt, row0, nrows, out_dtype, name, side=(), tn_pref=1024, ada=None):
    m, k = x.shape
    assert row0 % BF16_SUBLANES == 0
    tm = _tile(m, 1024, SUBLANES)
    tn = _tile(nrows, tn_pref)
    n_i, n_j = m // tm, nrows // tn
    side_in, side_out, side_shapes = _side_cast_specs(side, n_i, n_j)
    ada_in, ada_out, ada_shape, ada_args = [], [], [], []
    if ada is not None:
        ada_in, out_spec, shape = _ada_side_specs(ada, n_i, n_j)
        ada_out, ada_shape, ada_args = [out_spec], [shape], list(ada[:3])
    has_jobs = bool(side) or ada is not None
    out = pl.pallas_call(
        functools.partial(_mm_kernel, with_ada=ada is not None),
        out_shape=[jax.ShapeDtypeStruct((m, nrows), out_dtype)] + ada_shape + side_shapes,
        grid=(n_i, n_j),
        in_specs=[pl.BlockSpec((tm, k), lambda i, j: (i, 0)),
                  pl.BlockSpec((pl.Element(tn), pl.Element(k)),
                               lambda i, j: (pl.multiple_of(row0 + j * tn, BF16_SUBLANES), 0))]
                 + ada_in + side_in,
        out_specs=[pl.BlockSpec((tm, tn), lambda i, j: (i, j))] + ada_out + side_out,
        compiler_params=_cparams(("arbitrary" if has_jobs else "parallel", "arbitrary")),
        name=name,
    )(x, wt, *ada_args, *_side_arrays(side))
    return out if has_jobs else out[0]


def _hgrn_kernel(q_ref, i_ref, g_ref, f_ref, lbt_ref, og_ref, o_ref, st_ref, *, hb, tt):
    @pl.when(pl.program_id(1) == 0)
    def _():
        st_ref[...] = jnp.zeros_like(st_ref)

    tbl = lbt_ref[...]
    e = jnp.exp(tbl - jnp.max(tbl, axis=0, keepdims=True))
    lb = e[0:1, :] / jnp.sum(e, axis=0, keepdims=True)
    half = SUBLANES
    nt_dims = (((1,), (1,)), ((), ()))
    tn_dims = (((0,), (0,)), ((), ()))

    def gates(rows, cs, nrows):
        lbh = lb[:, cs]
        f = lbh + (1.0 - lbh) * _sigmoid(f_ref[rows, cs])
        b = jnp.log2(f)
        row = lax.broadcasted_iota(jnp.int32, (nrows, LANES), 0)
        sh = 1
        while sh < nrows:
            b = b + jnp.where(row >= sh, pltpu.roll(b, sh, axis=0), 0.0)
            sh *= 2
        return 1.0 - f, b

    def finish(o, rows, cs):
        ms = jnp.mean(o * o, axis=-1, keepdims=True)
        on = o * lax.rsqrt(ms + EPS) * og_ref[:, cs]
        gate = g_ref[rows, cs].astype(F32)
        o_ref[rows, cs] = (on * (gate * _sigmoid(gate))).astype(o_ref.dtype)

    mid = HGRN_FSUB // 2
    tri_r = lax.broadcasted_iota(jnp.int32, (HGRN_FSUB, HGRN_FSUB), 0)
    tri_c = lax.broadcasted_iota(jnp.int32, (HGRN_FSUB, HGRN_FSUB), 1)

    def body_factored(n, carry):
        pending = []
        for u, h in [(u, h) for u in range(HGRN_UNROLL) for h in range(hb)]:
            r0 = pl.multiple_of((n * HGRN_UNROLL + u) * HGRN_FSUB, HGRN_FSUB)
            rows = pl.ds(r0, HGRN_FSUB)
            cs = slice(h * LANES, (h + 1) * LANES)
            q = q_ref[rows, cs].astype(F32)
            v = i_ref[rows, cs]
            k, b = gates(rows, cs, HGRN_FSUB)
            r = b[mid - 1:mid, :]
            b_last = b[HGRN_FSUB - 1:HGRN_FSUB, :]
            qc = q * jnp.exp2(b - r)
            kc = (k * jnp.exp2(r - b)).astype(BF16)
            a = lax.dot_general(qc.astype(BF16), kc, nt_dims, preferred_element_type=F32)
            st = st_ref[h]
            o = lax.dot_general((qc * jnp.exp2(r)).astype(BF16), st.astype(BF16), nt_dims,
                                preferred_element_type=F32)
            kt = (k * jnp.exp2(b_last - b)).astype(BF16)
            upd = lax.dot_general(v, kt, tn_dims, preferred_element_type=F32)
            st_ref[h] = st * jnp.exp2(b_last) + upd
            pending.append((a, o, v, rows, cs))
        for a, o, v, rows, cs in pending:
            p = jnp.where(tri_c <= tri_r, a, 0.0).astype(BF16)
            finish(o + jnp.dot(p, v, preferred_element_type=F32), rows, cs)
        return carry

    rowc = lax.broadcasted_iota(jnp.int32, (SUBLANES, 1), 0)

    def body_exact(n, carry):
        r0 = pl.multiple_of(n * HGRN_SUB, HGRN_SUB)
        rows = pl.ds(r0, HGRN_SUB)
        for h in range(hb):
            cs = slice(h * LANES, (h + 1) * LANES)
            q = q_ref[rows, cs].astype(F32)
            v = i_ref[rows, cs]
            vf = v.astype(F32)
            k, b = gates(rows, cs, HGRN_SUB)
            b_last = b[HGRN_SUB - 1:HGRN_SUB, :]
            st = st_ref[h]
            qt = (q * jnp.exp2(b)).astype(BF16)
            o = lax.dot_general(qt, st.astype(BF16), nt_dims, preferred_element_type=F32)
            o_lo, o_hi = o[:half], o[half:]
            q_lo, q_hi = q[:half], q[half:]
            b_lo, b_hi = b[:half], b[half:]
            for s in range(HGRN_SUB):
                ks, bs, vs = k[s:s + 1, :], b[s:s + 1, :], vf[s:s + 1, :]
                c_hi = jnp.sum(q_hi * ks * jnp.exp2(b_hi - bs), axis=-1, keepdims=True)
                if s < half:
                    c_lo = jnp.sum(q_lo * ks * jnp.exp2(b_lo - bs), axis=-1, keepdims=True)
                    o_lo = o_lo + jnp.where(rowc >= s, c_lo, 0.0) * vs
                else:
                    c_hi = jnp.where(rowc >= s - half, c_hi, 0.0)
                o_hi = o_hi + c_hi * vs
            kt = (k * jnp.exp2(b_last - b)).astype(BF16)
            upd = lax.dot_general(v, kt, tn_dims, preferred_element_type=F32)
            st_ref[h] = st * jnp.exp2(b_last) + upd
            finish(jnp.concatenate([o_lo, o_hi], axis=0), rows, cs)
        return carry

    factorable = jnp.min(lb) >= HGRN_MIN_LB

    @pl.when(factorable)
    def _():
        lax.fori_loop(0, tt // (HGRN_FSUB * HGRN_UNROLL), body_factored, 0)

    @pl.when(jnp.logical_not(factorable))
    def _():
        lax.fori_loop(0, tt // HGRN_SUB, body_exact, 0)


def _hgrn(pq, pm, hf, lb_table, onorm_g, width, i_col, g_col):
    s = pq.shape[0]
    hb = min(16, width // HEAD_DIM)
    bw = hb * HEAD_DIM
    tt = _tile(s, 512, HGRN_SUB)
    nl = lb_table.shape[0]
    kern = functools.partial(_hgrn_kernel, hb=hb, tt=tt)
    return pl.pallas_call(
        kern,
        out_shape=jax.ShapeDtypeStruct((s, width), BF16),
        grid=(width // bw, s // tt),
        in_specs=[pl.BlockSpec((tt, bw), lambda h, t: (t, h)),
                  pl.BlockSpec((tt, bw), lambda h, t: (t, i_col // bw + h)),
                  pl.BlockSpec((tt, bw), lambda h, t: (t, g_col // bw + h)),
                  pl.BlockSpec((tt, bw), lambda h, t: (t, h)),
                  pl.BlockSpec((nl, bw), lambda h, t: (0, h)),
                  pl.BlockSpec((1, bw), lambda h, t: (0, h))],
        out_specs=pl.BlockSpec((tt, bw), lambda h, t: (t, h)),
        scratch_shapes=[pltpu.VMEM((hb, HEAD_DIM, HEAD_DIM), F32)],
        compiler_params=_cparams(("parallel", "arbitrary")),
        name="hgrn2",
    )(pq, pm, pm, hf, lb_table, onorm_g.reshape(1, width))


def _split3(x):
    hi = x.astype(BF16).astype(F32)
    r = x - hi
    mid = r.astype(BF16).astype(F32)
    lo = (r - mid).astype(BF16).astype(F32)
    return hi, mid, lo


def _foxprep_kernel(q_ref, k_ref, ff_ref, fb_ref, gq_ref, gk_ref, qo_ref, kto_ref, r_ref, carry_ref, *, nh, scale):
    @pl.when(pl.program_id(0) == 0)
    def _():
        carry_ref[...] = jnp.zeros_like(carry_ref)

    tk = q_ref.shape[0]
    x = ff_ref[...] + fb_ref[...]
    ls = (jnp.minimum(x, 0.0) - jnp.log(1.0 + jnp.exp(-jnp.abs(x)))) * LOG2E
    ri = lax.broadcasted_iota(jnp.int32, (tk, tk), 0)
    ci = lax.broadcasted_iota(jnp.int32, (tk, tk), 1)
    lower = (ci <= ri).astype(F32)
    rel = jnp.dot(lower, ls, precision=lax.Precision.HIGHEST, preferred_element_type=F32)
    base = carry_ref[...]
    r_ref[0] = base
    carry_ref[...] = base + rel[tk - 1:tk, :]

    cq = _split3(rel)
    ck = _split3(rel.T)
    lane = lax.broadcasted_iota(jnp.int32, (tk, LANES), 1)
    sub = lax.broadcasted_iota(jnp.int32, (LANES, tk), 0)
    ones_q = jnp.where((lane >= 3) & (lane < 6), 1.0, 0.0)
    ones_k = jnp.where(sub < 3, 1.0, 0.0)
    for h in range(nh):
        cs = slice(h * LANES, (h + 1) * LANES)
        q = q_ref[:, cs].astype(F32)
        qn = q * lax.rsqrt(jnp.mean(q * q, axis=-1, keepdims=True) + EPS) * gq_ref[:, cs]
        k = k_ref[:, cs].astype(F32)
        kn = k * lax.rsqrt(jnp.mean(k * k, axis=-1, keepdims=True) + EPS) * gk_ref[:, cs]
        eq = ones_q
        ek = ones_k
        for t in range(3):
            eq = jnp.where(lane == t, cq[t][:, h:h + 1], eq)
            ek = jnp.where(sub == 3 + t, -ck[t][h:h + 1, :], ek)
        c0 = h * FOX_KDIM
        qo_ref[:, c0:c0 + LANES] = (qn * scale).astype(qo_ref.dtype)
        qo_ref[:, c0 + LANES:c0 + FOX_KDIM] = eq.astype(qo_ref.dtype)
        kto_ref[0, c0:c0 + LANES, :] = kn.T.astype(kto_ref.dtype)
        kto_ref[0, c0 + LANES:c0 + FOX_KDIM, :] = ek.astype(kto_ref.dtype)


def _foxprep(pm, ff, fbias, gq, gk, width, q_col, k_col, tk):
    s = pm.shape[0]
    nh = width // HEAD_DIM
    nb = s // tk
    kern = functools.partial(_foxprep_kernel, nh=nh, scale=LOG2E * float(HEAD_DIM) ** -0.5)
    qa, kta, r = pl.pallas_call(
        kern,
        out_shape=(jax.ShapeDtypeStruct((s, nh * FOX_KDIM), BF16),
                   jax.ShapeDtypeStruct((nb, nh * FOX_KDIM, tk), BF16),
                   jax.ShapeDtypeStruct((nb, 1, LANES), F32)),
        grid=(nb,),
        in_specs=[pl.BlockSpec((tk, width), lambda i: (i, q_col // width)),
                  pl.BlockSpec((tk, width), lambda i: (i, k_col // width)),
                  pl.BlockSpec((tk, LANES), lambda i: (i, 0)),
                  pl.BlockSpec((1, LANES), lambda i: (0, 0)),
                  pl.BlockSpec((1, width), lambda i: (0, 0)),
                  pl.BlockSpec((1, width), lambda i: (0, 0))],
        out_specs=(pl.BlockSpec((tk, nh * FOX_KDIM), lambda i: (i, 0)),
                   pl.BlockSpec((1, nh * FOX_KDIM, tk), lambda i: (i, 0, 0)),
                   pl.BlockSpec((1, 1, LANES), lambda i: (i, 0, 0))),
        scratch_shapes=[pltpu.VMEM((1, LANES), F32)],
        compiler_params=_cparams(("arbitrary",)),
        name="fox_prep",
    )(pm, pm, ff, fbias, gq.reshape(1, width), gk.reshape(1, width))
    return qa, kta, r.reshape(nb, LANES)


def _fox_kernel(r_ref, q_ref, kt_ref, v_ref, o_ref, v2_ref, s_ref, m_ref, acc_ref, *, tk):
    h = pl.program_id(0)
    qi = pl.program_id(1)

    @pl.when(qi == 0)
    def _():
        v2_ref[:, :LANES] = v_ref[...]
        lane = lax.broadcasted_iota(jnp.int32, v_ref.shape, 1)
        v2_ref[:, LANES:] = jnp.where(lane == 0, 1.0, 0.0).astype(v2_ref.dtype)

    m_ref[...] = jnp.full(m_ref.shape, -jnp.inf, F32)
    acc_ref[...] = jnp.zeros(acc_ref.shape, F32)
    rr = lax.broadcasted_iota(jnp.int32, (tk, tk), 0)
    cc = lax.broadcasted_iota(jnp.int32, (tk, tk), 1)

    def scores(c, j, slot):
        s_ref[slot, c] = jnp.dot(q_ref[c * tk:(c + 1) * tk, :], kt_ref[j], preferred_element_type=F32)

    def fold(c, j, slot, diag):
        s = s_ref[slot, c]
        if diag:
            s = jnp.where(cc <= rr, s, -jnp.inf)
        d = r_ref[2 * qi + c, h] - r_ref[j, h]
        m = m_ref[c]
        m_new = jnp.maximum(m, jnp.max(s, axis=-1, keepdims=True) + d)
        p = jnp.exp2(s - (m_new - d))
        r0 = pl.multiple_of(j * tk, tk)
        pv = jnp.dot(p.astype(BF16), v2_ref[pl.ds(r0, tk), :], preferred_element_type=F32)
        acc_ref[c] = jnp.exp2(m - m_new) * acc_ref[c] + pv
        m_ref[c] = m_new

    scores(0, 0, 0)
    scores(1, 0, 0)

    def pair(j):
        for slot in range(2):
            scores(0, j + slot + 1, 1 - slot)
            scores(1, j + slot + 1, 1 - slot)
            fold(0, j + slot, slot, False)
            fold(1, j + slot, slot, False)

    def body(jp, carry):
        pair(2 * jp)
        return carry

    lax.fori_loop(0, qi, body, 0)
    scores(1, 2 * qi + 1, 1)
    fold(0, 2 * qi, 0, True)
    fold(1, 2 * qi, 0, False)
    fold(1, 2 * qi + 1, 1, True)
    for c in range(2):
        acc = acc_ref[c]
        o_ref[c * tk:(c + 1) * tk, :] = (acc[:, :LANES] / acc[:, LANES:LANES + 1]).astype(o_ref.dtype)


def _fox(qa, kta, r, pm, v_col, tk):
    s = qa.shape[0]
    nh = qa.shape[1] // FOX_KDIM
    nb = s // tk
    tq = 2 * tk
    kern = functools.partial(_fox_kernel, tk=tk)
    return pl.pallas_call(
        kern,
        out_shape=jax.ShapeDtypeStruct((s, nh * HEAD_DIM), BF16),
        grid=(nh, s // tq),
        in_specs=[pl.BlockSpec(memory_space=pltpu.SMEM),
                  pl.BlockSpec((tq, FOX_KDIM), lambda h, i: (i, h)),
                  pl.BlockSpec((nb, FOX_KDIM, tk), lambda h, i: (0, h, 0)),
                  pl.BlockSpec((s, HEAD_DIM), lambda h, i: (0, v_col // HEAD_DIM + h))],
        out_specs=pl.BlockSpec((tq, HEAD_DIM), lambda h, i: (i, h)),
        scratch_shapes=[pltpu.VMEM((s, FOX_KDIM), BF16),
                        pltpu.VMEM((2, 2, tk, tk), F32),
                        pltpu.VMEM((2, tk, 1), F32),
                        pltpu.VMEM((2, tk, FOX_KDIM), F32)],
        compiler_params=_cparams(("parallel", "arbitrary")),
        name="fox_attn",
    )(r, qa, kta, pm)


def _merge_kernel(oa_ref, ob_ref, wa_ref, wb_ref, ga_ref, gb_ref, y_ref):
    sa = _sigmoid(ga_ref[...].astype(F32))
    sb = _sigmoid(gb_ref[...].astype(F32))
    ya = jnp.dot(oa_ref[...], wa_ref[...], preferred_element_type=F32)
    yb = jnp.dot(ob_ref[...], wb_ref[...], preferred_element_type=F32)
    y_ref[...] = (sa * ya + sb * yb).astype(y_ref.dtype)


def _merge(oa, ob, wa, wb, pg):
    m, ka = oa.shape
    kb = ob.shape[1]
    n = wa.shape[1]
    tm = _tile(m, 1024, SUBLANES)
    tn = _tile(n, 1024)
    return pl.pallas_call(
        _merge_kernel,
        out_shape=jax.ShapeDtypeStruct((m, n), BF16),
        grid=(m // tm, n // tn),
        in_specs=[pl.BlockSpec((tm, ka), lambda i, j: (i, 0)),
                  pl.BlockSpec((tm, kb), lambda i, j: (i, 0)),
                  pl.BlockSpec((ka, tn), lambda i, j: (0, j)),
                  pl.BlockSpec((kb, tn), lambda i, j: (0, j)),
                  pl.BlockSpec((tm, tn), lambda i, j: (i, j)),
                  pl.BlockSpec((tm, tn), lambda i, j: (i, n // tn + j))],
        out_specs=pl.BlockSpec((tm, tn), lambda i, j: (i, j)),
        compiler_params=_cparams(("parallel", "arbitrary")),
        name="branch_merge",
    )(oa, ob, wa, wb, pg, pg)


def _resproj_kernel(a_ref, w_ref, x_ref, gt_ref, o_ref):
    y = jnp.dot(a_ref[...], w_ref[...], preferred_element_type=F32)
    o_ref[...] = x_ref[...] + gt_ref[...] * y


def _resproj(a, w, x, mod, gate_idx, tm_pref, tn_pref, name):
    m, k = a.shape
    n = w.shape[1]
    tm = _tile(m, tm_pref, SUBLANES)
    tn = _tile(n, tn_pref)
    return pl.pallas_call(
        _resproj_kernel,
        out_shape=jax.ShapeDtypeStruct((m, n), F32),
        grid=(m // tm, n // tn),
        in_specs=[pl.BlockSpec((tm, k), lambda i, j: (i, 0)),
                  pl.BlockSpec((k, tn), lambda i, j: (0, j)),
                  pl.BlockSpec((tm, tn), lambda i, j: (i, j)),
                  pl.BlockSpec((1, tn), lambda i, j: (0, gate_idx * (n // tn) + j))],
        out_specs=pl.BlockSpec((tm, tn), lambda i, j: (i, j)),
        compiler_params=_cparams(("parallel", "arbitrary")),
        name=name,
    )(a, w, x, mod)


def _swiglu_kernel(h_ref, wg_ref, wu_ref, *refs):
    n_side = (len(refs) - 1) // 2
    o_ref = refs[n_side]
    h = h_ref[...]
    g = jnp.dot(h, wg_ref[...], preferred_element_type=F32)
    u = jnp.dot(h, wu_ref[...], preferred_element_type=F32)
    o_ref[...] = (g * _sigmoid(g) * u).astype(o_ref.dtype)
    _side_cast(refs[:n_side] + refs[n_side + 1:])


def _swiglu(h, w, side=()):
    m, k = h.shape
    dff = w.shape[1] // 2
    tm = _tile(m, 2048, SUBLANES)
    tn = _tile(dff, 512)
    nj = dff // tn
    side_in, side_out, side_shapes = _side_cast_specs(side, m // tm, nj)
    out = pl.pallas_call(
        _swiglu_kernel,
        out_shape=[jax.ShapeDtypeStruct((m, dff), BF16)] + side_shapes,
        grid=(m // tm, nj),
        in_specs=[pl.BlockSpec((tm, k), lambda i, j: (i, 0)),
                  pl.BlockSpec((k, tn), lambda i, j: (0, j)),
                  pl.BlockSpec((k, tn), lambda i, j: (0, nj + j))] + side_in,
        out_specs=[pl.BlockSpec((tm, tn), lambda i, j: (i, j))] + side_out,
        compiler_params=_cparams(("arbitrary" if side else "parallel", "arbitrary")),
        name="ffn_swiglu",
    )(h, w, w, *_side_arrays(side))
    return out if side else out[0]


def kernel(x, c, w_ada, b_ada, norm_mix_g, norm_ffn_g, w_in, fox_f_bias, hgrn_lb_table, hgrn_onorm_g,
           fox_q_norm_g, fox_k_norm_g, w_branch_a, w_branch_b, w_out, w_ffn_in, w_ffn_out):
    batch, seq, d = x.shape
    assert batch == 1 and w_ada.shape[0] == 1, "single sequence, single layer"
    hw = hgrn_onorm_g.shape[1] * hgrn_onorm_g.shape[2]
    fw = fox_q_norm_g.shape[1] * fox_q_norm_g.shape[2]
    nfh = fox_q_norm_g.shape[1]
    assert hgrn_onorm_g.shape[2] == HEAD_DIM and fox_q_norm_g.shape[2] == HEAD_DIM
    assert nfh <= LANES and hw == fw and seq % (2 * FOX_TK) == 0
    x2 = x.reshape(seq, d)

    f_col = 4 * hw + 3 * fw
    wt = jnp.swapaxes(w_in, 1, 2)[0]
    fbias = jnp.pad(fox_f_bias[0], (0, LANES - nfh)).reshape(1, LANES)

    c_col, w_ada2, b_ada2 = c.reshape(d, 1), w_ada[0], b_ada.reshape(1, b_ada.shape[-1])
    mod_a = _ada_mod(c_col, w_ada2, b_ada2, 2 * d)

    n_main = 2 * hw + 3 * fw
    h, ff, w_qf = _normmod_proj(x2, norm_mix_g[0], mod_a, 0, 1, wt, f_col, LANES,
                                side=((wt, 0, 2 * hw),))
    pq, w_main = _matmul(h, w_qf, 0, hw, BF16, "in_proj_q", tn_pref=512,
                         side=((wt, 2 * hw, n_main),))
    hf, w_gates = _matmul(h, w_qf, hw, hw, F32, "in_proj_f", tn_pref=512,
                          side=((wt, f_col, nfh + 2 * d),))
    pm, mod_b, w_out_bf, w_a_bf, w_b_bf = _matmul(
        h, w_main, 0, n_main, BF16, "in_proj_main", ada=(c_col, w_ada2, b_ada2, 2 * d, 4 * d),
        side=(w_out[0], w_branch_a[0], w_branch_b[0]))
    pg, w_ffn_in_bf = _matmul(h, w_gates, nfh, 2 * d, BF16, "in_proj_gates",
                              side=(w_ffn_in[0],))

    o_a = _hgrn(pq, pm, hf, hgrn_lb_table, hgrn_onorm_g[0], hw, 0, hw)

    qa, kta, r = _foxprep(pm, ff, fbias, fox_q_norm_g[0], fox_k_norm_g[0], fw, 2 * hw, 2 * hw + fw, FOX_TK)
    o_b = _fox(qa, kta, r, pm, 2 * hw + 2 * fw, FOX_TK)

    y = _merge(o_a, o_b, w_a_bf, w_b_bf, pg)
    x1 = _resproj(y, w_out_bf, x2, mod_b, 0, 1024, 512, "out_proj")

    h2 = _normmod(x1, norm_ffn_g[0], mod_b, 1, 2)
    act, w_ffn_out_bf = _swiglu(h2, w_ffn_in_bf, side=(w_ffn_out[0],))
    x3 = _resproj(act, w_ffn_out_bf, x1, mod_b, 3, 512, 512, "ffn_out")
    return x3.reshape(batch, seq, d)
```

```python
import functools
import math

import jax
import jax.numpy as jnp
from jax import lax
from jax.experimental import pallas as pl
from jax.experimental.pallas import tpu as pltpu

F32 = jnp.float32
BF16 = jnp.bfloat16

EPS = 1e-6
LOG2E = 1.4426950408889634
LANES = 128
SUBLANES = 8
BF16_SUBLANES = 16
HEAD_DIM = 128
HGRN_SUB = 16
HGRN_FSUB = 32
HGRN_UNROLL = 4
HGRN_MIN_LB = 2.0 ** -6
FOX_TK = 512
FOX_KDIM = 2 * HEAD_DIM
V7X_VMEM_LIMIT_BYTES = 56 * 1024 * 1024


def _cparams(semantics, vmem_bytes=V7X_VMEM_LIMIT_BYTES):
    return pltpu.CompilerParams(dimension_semantics=semantics, vmem_limit_bytes=vmem_bytes)


def _tile(dim, pref, align=LANES):
    if dim <= pref:
        return dim
    t = (pref // align) * align
    while t >= align:
        if dim % t == 0:
            return t
        t -= align
    raise ValueError(f"no {align}-aligned tile for {dim}")


def _sigmoid(x):
    return 1.0 / (1.0 + jnp.exp(-x))


def _ada_kernel(c_ref, w_ref, b_ref, o_ref):
    c = c_ref[...]
    s = c * _sigmoid(c)
    o_ref[...] = jnp.sum(w_ref[...] * s, axis=0, keepdims=True) + b_ref[...]


def _ada_mod(c_col, w_ada, b_ada, ncols):
    d = w_ada.shape[0]
    tn = _tile(ncols, 512)
    return pl.pallas_call(
        _ada_kernel,
        out_shape=jax.ShapeDtypeStruct((1, ncols), F32),
        grid=(ncols // tn,),
        in_specs=[pl.BlockSpec((d, 1), lambda j: (0, 0)),
                  pl.BlockSpec((d, tn), lambda j: (0, j)),
                  pl.BlockSpec((1, tn), lambda j: (0, j))],
        out_specs=pl.BlockSpec((1, tn), lambda j: (0, j)),
        compiler_params=_cparams(("parallel",)),
        name="ada_mod",
    )(c_col, w_ada, b_ada)


def _ada_side_specs(ada, n_i, n_j):
    c_col, w_ada, b_ada, col0, ncols = ada
    d = w_ada.shape[0]
    units = ncols // LANES
    n_blk = max(dv for dv in range(1, min(units, n_i * n_j) + 1) if units % dv == 0)
    cb = ncols // n_blk
    assert col0 % cb == 0
    blk = lambda i, j: jnp.minimum(i * n_j + j, n_blk - 1)
    in_specs = [pl.BlockSpec((d, 1), lambda i, j: (0, 0)),
                pl.BlockSpec((d, cb), lambda i, j: (0, col0 // cb + blk(i, j))),
                pl.BlockSpec((1, cb), lambda i, j: (0, col0 // cb + blk(i, j)))]
    out_spec = pl.BlockSpec((1, cb), lambda i, j: (0, blk(i, j)))
    return in_specs, out_spec, jax.ShapeDtypeStruct((1, ncols), F32)


def _normmod_kernel(x_ref, g_ref, sh_ref, sc_ref, o_ref):
    x = x_ref[...]
    ms = jnp.mean(x * x, axis=-1, keepdims=True)
    xn = x * lax.rsqrt(ms + EPS)
    o_ref[...] = (xn * g_ref[...] * (1.0 + sc_ref[...]) + sh_ref[...]).astype(o_ref.dtype)


def _normmod(x, gain, mod, shift_idx, scale_idx):
    s, d = x.shape
    tm = _tile(s, 512, SUBLANES)
    return pl.pallas_call(
        _normmod_kernel,
        out_shape=jax.ShapeDtypeStruct((s, d), BF16),
        grid=(s // tm,),
        in_specs=[pl.BlockSpec((tm, d), lambda i: (i, 0)),
                  pl.BlockSpec((1, d), lambda i: (0, 0)),
                  pl.BlockSpec((1, d), lambda i: (0, shift_idx)),
                  pl.BlockSpec((1, d), lambda i: (0, scale_idx))],
        out_specs=pl.BlockSpec((tm, d), lambda i: (i, 0)),
        compiler_params=_cparams(("parallel",)),
        name="normmod",
    )(x, gain.reshape(1, d), mod, mod)


def _normmod_proj_kernel(x_ref, g_ref, sh_ref, sc_ref, wt_ref, *refs):
    n_side = (len(refs) - 2) // 2
    h_ref, o_ref = refs[n_side], refs[n_side + 1]
    _normmod_kernel(x_ref, g_ref, sh_ref, sc_ref, h_ref)
    o_ref[...] = lax.dot_general(h_ref[...], wt_ref[...].astype(BF16), (((1,), (1,)), ((), ())),
                                 preferred_element_type=F32)
    _side_cast(refs[:n_side] + refs[n_side + 2:])


def _normmod_proj(x, gain, mod, shift_idx, scale_idx, wt, row0, nrows, side):
    s, d = x.shape
    tm = _tile(s, 512, SUBLANES)
    n_i = s // tm
    side_in, side_out, side_shapes = _side_cast_specs(side, n_i, 1)
    return pl.pallas_call(
        _normmod_proj_kernel,
        out_shape=[jax.ShapeDtypeStruct((s, d), BF16), jax.ShapeDtypeStruct((s, nrows), F32)] + side_shapes,
        grid=(n_i, 1),
        in_specs=[pl.BlockSpec((tm, d), lambda i, j: (i, 0)),
                  pl.BlockSpec((1, d), lambda i, j: (0, 0)),
                  pl.BlockSpec((1, d), lambda i, j: (0, shift_idx)),
                  pl.BlockSpec((1, d), lambda i, j: (0, scale_idx)),
                  pl.BlockSpec((pl.Element(nrows), pl.Element(d)), lambda i, j: (row0, 0))] + side_in,
        out_specs=[pl.BlockSpec((tm, d), lambda i, j: (i, 0)),
                   pl.BlockSpec((tm, nrows), lambda i, j: (i, 0))] + side_out,
        compiler_params=_cparams(("arbitrary", "arbitrary")),
        name="normmod_proj_ff",
    )(x, gain.reshape(1, d), mod, mod, wt, *_side_arrays(side))


def _side_cast_specs(side, n_i, n_j):
    in_specs, out_specs, shapes = [], [], []
    for item in side:
        w, row0, rows = item if isinstance(item, tuple) else (item, 0, item.shape[0])
        cols = w.shape[1]
        assert rows % BF16_SUBLANES == 0 and row0 % BF16_SUBLANES == 0, (row0, rows)
        units = rows // BF16_SUBLANES
        n_slabs = max(dv for dv in range(1, min(units, n_i * n_j) + 1) if units % dv == 0)
        slab = rows // n_slabs

        def slab_index(i, j, n_slabs=n_slabs):
            return jnp.minimum(i * n_j + j, n_slabs - 1)

        in_specs.append(pl.BlockSpec(
            (pl.Element(slab), pl.Element(cols)),
            lambda i, j, f=slab_index, row0=row0, slab=slab:
                (pl.multiple_of(row0 + f(i, j) * slab, BF16_SUBLANES), 0)))
        out_specs.append(pl.BlockSpec((slab, cols), lambda i, j, f=slab_index: (f(i, j), 0)))
        shapes.append(jax.ShapeDtypeStruct((rows, cols), BF16))
    return in_specs, out_specs, shapes


def _side_cast(side_refs):
    n = len(side_refs) // 2
    for src, dst in zip(side_refs[:n], side_refs[n:]):
        dst[...] = src[...].astype(dst.dtype)


def _mm_kernel(x_ref, wt_ref, *refs, with_ada):
    n_ada = 3 if with_ada else 0
    n_side = (len(refs) - n_ada - 1 - (1 if with_ada else 0)) // 2
    ada_in, side_in = refs[:n_ada], refs[n_ada:n_ada + n_side]
    outs = refs[n_ada + n_side:]
    o_ref = outs[0]
    y = lax.dot_general(x_ref[...], wt_ref[...].astype(BF16), (((1,), (1,)), ((), ())),
                        preferred_element_type=F32)
    o_ref[...] = y.astype(o_ref.dtype)
    if with_ada:
        _ada_kernel(*ada_in, outs[1])
    _side_cast(side_in + outs[1 + (1 if with_ada else 0):])


def _side_arrays(side):
    return [item[0] if isinstance(item, tuple) else item for item in side]


def _matmul(x, wt, row0, nrows, out_dtype, name, side=(), tn_pref=1024, ada=None):
    m, k = x.shape
    assert row0 % BF16_SUBLANES == 0
    tm = _tile(m, 1024, SUBLANES)
    tn = _tile(nrows, tn_pref)
    n_i, n_j = m // tm, nrows // tn
    side_in, side_out, side_shapes = _side_cast_specs(side, n_i, n_j)
    ada_in, ada_out, ada_shape, ada_args = [], [], [], []
    if ada is not None:
        ada_in, out_spec, shape = _ada_side_specs(ada, n_i, n_j)
        ada_out, ada_shape, ada_args = [out_spec], [shape], list(ada[:3])
    has_jobs = bool(side) or ada is not None
    out = pl.pallas_call(
        functools.partial(_mm_kernel, with_ada=ada is not None),
        out_shape=[jax.ShapeDtypeStruct((m, nrows), out_dtype)] + ada_shape + side_shapes,
        grid=(n_i, n_j),
        in_specs=[pl.BlockSpec((tm, k), lambda i, j: (i, 0)),
                  pl.BlockSpec((pl.Element(tn), pl.Element(k)),
                               lambda i, j: (pl.multiple_of(row0 + j * tn, BF16_SUBLANES), 0))]
                 + ada_in + side_in,
        out_specs=[pl.BlockSpec((tm, tn), lambda i, j: (i, j))] + ada_out + side_out,
        compiler_params=_cparams(("arbitrary" if has_jobs else "parallel", "arbitrary")),
        name=name,
    )(x, wt, *ada_args, *_side_arrays(side))
    return out if has_jobs else out[0]


def _hgrn_kernel(q_ref, i_ref, g_ref, f_ref, lbt_ref, og_ref, o_ref, st_ref, *, hb, tt):
    @pl.when(pl.program_id(1) == 0)
    def _():
        st_ref[...] = jnp.zeros_like(st_ref)

    tbl = lbt_ref[...]
    e = jnp.exp(tbl - jnp.max(tbl, axis=0, keepdims=True))
    lb = e[0:1, :] / jnp.sum(e, axis=0, keepdims=True)
    half = SUBLANES
    nt_dims = (((1,), (1,)), ((), ()))
    tn_dims = (((0,), (0,)), ((), ()))

    def gates(rows, cs, nrows):
        lbh = lb[:, cs]
        f = lbh + (1.0 - lbh) * _sigmoid(f_ref[rows, cs])
        b = jnp.log2(f)
        row = lax.broadcasted_iota(jnp.int32, (nrows, LANES), 0)
        sh = 1
        while sh < nrows:
            b = b + jnp.where(row >= sh, pltpu.roll(b, sh, axis=0), 0.0)
            sh *= 2
        return 1.0 - f, b

    def finish(o, rows, cs):
        ms = jnp.mean(o * o, axis=-1, keepdims=True)
        on = o * lax.rsqrt(ms + EPS) * og_ref[:, cs]
        gate = g_ref[rows, cs].astype(F32)
        o_ref[rows, cs] = (on * (gate * _sigmoid(gate))).astype(o_ref.dtype)

    mid = HGRN_FSUB // 2
    tri_r = lax.broadcasted_iota(jnp.int32, (HGRN_FSUB, HGRN_FSUB), 0)
    tri_c = lax.broadcasted_iota(jnp.int32, (HGRN_FSUB, HGRN_FSUB), 1)

    def body_factored(n, carry):
        pending = []
        for u, h in [(u, h) for u in range(HGRN_UNROLL) for h in range(hb)]:
            r0 = pl.multiple_of((n * HGRN_UNROLL + u) * HGRN_FSUB, HGRN_FSUB)
            rows = pl.ds(r0, HGRN_FSUB)
            cs = slice(h * LANES, (h + 1) * LANES)
            q = q_ref[rows, cs].astype(F32)
            v = i_ref[rows, cs]
            k, b = gates(rows, cs, HGRN_FSUB)
            r = b[mid - 1:mid, :]
            b_last = b[HGRN_FSUB - 1:HGRN_FSUB, :]
            qc = q * jnp.exp2(b - r)
            kc = (k * jnp.exp2(r - b)).astype(BF16)
            a = lax.dot_general(qc.astype(BF16), kc, nt_dims, preferred_element_type=F32)
            st = st_ref[h]
            o = lax.dot_general((qc * jnp.exp2(r)).astype(BF16), st.astype(BF16), nt_dims,
                                preferred_element_type=F32)
            kt = (k * jnp.exp2(b_last - b)).astype(BF16)
            upd = lax.dot_general(v, kt, tn_dims, preferred_element_type=F32)
            st_ref[h] = st * jnp.exp2(b_last) + upd
            pending.append((a, o, v, rows, cs))
        for a, o, v, rows, cs in pending:
            p = jnp.where(tri_c <= tri_r, a, 0.0).astype(BF16)
            finish(o + jnp.dot(p, v, preferred_element_type=F32), rows, cs)
        return carry

    rowc = lax.broadcasted_iota(jnp.int32, (SUBLANES, 1), 0)

    def body_exact(n, carry):
        r0 = pl.multiple_of(n * HGRN_SUB, HGRN_SUB)
        rows = pl.ds(r0, HGRN_SUB)
        for h in range(hb):
            cs = slice(h * LANES, (h + 1) * LANES)
            q = q_ref[rows, cs].astype(F32)
            v = i_ref[rows, cs]
            vf = v.astype(F32)
            k, b = gates(rows, cs, HGRN_SUB)
            b_last = b[HGRN_SUB - 1:HGRN_SUB, :]
            st = st_ref[h]
            qt = (q * jnp.exp2(b)).astype(BF16)
            o = lax.dot_general(qt, st.astype(BF16), nt_dims, preferred_element_type=F32)
            o_lo, o_hi = o[:half], o[half:]
            q_lo, q_hi = q[:half], q[half:]
            b_lo, b_hi = b[:half], b[half:]
            for s in range(HGRN_SUB):
                ks, bs, vs = k[s:s + 1, :], b[s:s + 1, :], vf[s:s + 1, :]
                c_hi = jnp.sum(q_hi * ks * jnp.exp2(b_hi - bs), axis=-1, keepdims=True)
                if s < half:
                    c_lo = jnp.sum(q_lo * ks * jnp.exp2(b_lo - bs), axis=-1, keepdims=True)
                    o_lo = o_lo + jnp.where(rowc >= s, c_lo, 0.0) * vs
                else:
                    c_hi = jnp.where(rowc >= s - half, c_hi, 0.0)
                o_hi = o_hi + c_hi * vs
            kt = (k * jnp.exp2(b_last - b)).astype(BF16)
            upd = lax.dot_general(v, kt, tn_dims, preferred_element_type=F32)
            st_ref[h] = st * jnp.exp2(b_last) + upd
            finish(jnp.concatenate([o_lo, o_hi], axis=0), rows, cs)
        return carry

    factorable = jnp.min(lb) >= HGRN_MIN_LB

    @pl.when(factorable)
    def _():
        lax.fori_loop(0, tt // (HGRN_FSUB * HGRN_UNROLL), body_factored, 0)

    @pl.when(jnp.logical_not(factorable))
    def _():
        lax.fori_loop(0, tt // HGRN_SUB, body_exact, 0)


def _hgrn(pq, pm, hf, lb_table, onorm_g, width, i_col, g_col):
    s = pq.shape[0]
    hb = min(16, width // HEAD_DIM)
    bw = hb * HEAD_DIM
    tt = _tile(s, 512, HGRN_SUB)
    nl = lb_table.shape[0]
    kern = functools.partial(_hgrn_kernel, hb=hb, tt=tt)
    return pl.pallas_call(
        kern,
        out_shape=jax.ShapeDtypeStruct((s, width), BF16),
        grid=(width // bw, s // tt),
        in_specs=[pl.BlockSpec((tt, bw), lambda h, t: (t, h)),
                  pl.BlockSpec((tt, bw), lambda h, t: (t, i_col // bw + h)),
                  pl.BlockSpec((tt, bw), lambda h, t: (t, g_col // bw + h)),
                  pl.BlockSpec((tt, bw), lambda h, t: (t, h)),
                  pl.BlockSpec((nl, bw), lambda h, t: (0, h)),
                  pl.BlockSpec((1, bw), lambda h, t: (0, h))],
        out_specs=pl.BlockSpec((tt, bw), lambda h, t: (t, h)),
        scratch_shapes=[pltpu.VMEM((hb, HEAD_DIM, HEAD_DIM), F32)],
        compiler_params=_cparams(("parallel", "arbitrary")),
        name="hgrn2",
    )(pq, pm, pm, hf, lb_table, onorm_g.reshape(1, width))


def _split3(x):
    hi = x.astype(BF16).astype(F32)
    r = x - hi
    mid = r.astype(BF16).astype(F32)
    lo = (r - mid).astype(BF16).astype(F32)
    return hi, mid, lo


def _foxprep_kernel(q_ref, k_ref, ff_ref, fb_ref, gq_ref, gk_ref, qo_ref, kto_ref, r_ref, carry_ref, *, nh, scale):
    @pl.when(pl.program_id(0) == 0)
    def _():
        carry_ref[...] = jnp.zeros_like(carry_ref)

    tk = q_ref.shape[0]
    x = ff_ref[...] + fb_ref[...]
    ls = (jnp.minimum(x, 0.0) - jnp.log(1.0 + jnp.exp(-jnp.abs(x)))) * LOG2E
    ri = lax.broadcasted_iota(jnp.int32, (tk, tk), 0)
    ci = lax.broadcasted_iota(jnp.int32, (tk, tk), 1)
    lower = (ci <= ri).astype(F32)
    rel = jnp.dot(lower, ls, precision=lax.Precision.HIGHEST, preferred_element_type=F32)
    base = carry_ref[...]
    r_ref[0] = base
    carry_ref[...] = base + rel[tk - 1:tk, :]

    cq = _split3(rel)
    ck = _split3(rel.T)
    lane = lax.broadcasted_iota(jnp.int32, (tk, LANES), 1)
    sub = lax.broadcasted_iota(jnp.int32, (LANES, tk), 0)
    ones_q = jnp.where((lane >= 3) & (lane < 6), 1.0, 0.0)
    ones_k = jnp.where(sub < 3, 1.0, 0.0)
    for h in range(nh):
        cs = slice(h * LANES, (h + 1) * LANES)
        q = q_ref[:, cs].astype(F32)
        qn = q * lax.rsqrt(jnp.mean(q * q, axis=-1, keepdims=True) + EPS) * gq_ref[:, cs]
        k = k_ref[:, cs].astype(F32)
        kn = k * lax.rsqrt(jnp.mean(k * k, axis=-1, keepdims=True) + EPS) * gk_ref[:, cs]
        eq = ones_q
        ek = ones_k
        for t in range(3):
            eq = jnp.where(lane == t, cq[t][:, h:h + 1], eq)
            ek = jnp.where(sub == 3 + t, -ck[t][h:h + 1, :], ek)
        c0 = h * FOX_KDIM
        qo_ref[:, c0:c0 + LANES] = (qn * scale).astype(qo_ref.dtype)
        qo_ref[:, c0 + LANES:c0 + FOX_KDIM] = eq.astype(qo_ref.dtype)
        kto_ref[0, c0:c0 + LANES, :] = kn.T.astype(kto_ref.dtype)
        kto_ref[0, c0 + LANES:c0 + FOX_KDIM, :] = ek.astype(kto_ref.dtype)


def _foxprep(pm, ff, fbias, gq, gk, width, q_col, k_col, tk):
    s = pm.shape[0]
    nh = width // HEAD_DIM
    nb = s // tk
    kern = functools.partial(_foxprep_kernel, nh=nh, scale=LOG2E * float(HEAD_DIM) ** -0.5)
    qa, kta, r = pl.pallas_call(
        kern,
        out_shape=(jax.ShapeDtypeStruct((s, nh * FOX_KDIM), BF16),
                   jax.ShapeDtypeStruct((nb, nh * FOX_KDIM, tk), BF16),
                   jax.ShapeDtypeStruct((nb, 1, LANES), F32)),
        grid=(nb,),
        in_specs=[pl.BlockSpec((tk, width), lambda i: (i, q_col // width)),
                  pl.BlockSpec((tk, width), lambda i: (i, k_col // width)),
                  pl.BlockSpec((tk, LANES), lambda i: (i, 0)),
                  pl.BlockSpec((1, LANES), lambda i: (0, 0)),
                  pl.BlockSpec((1, width), lambda i: (0, 0)),
                  pl.BlockSpec((1, width), lambda i: (0, 0))],
        out_specs=(pl.BlockSpec((tk, nh * FOX_KDIM), lambda i: (i, 0)),
                   pl.BlockSpec((1, nh * FOX_KDIM, tk), lambda i: (i, 0, 0)),
                   pl.BlockSpec((1, 1, LANES), lambda i: (i, 0, 0))),
        scratch_shapes=[pltpu.VMEM((1, LANES), F32)],
        compiler_params=_cparams(("arbitrary",)),
        name="fox_prep",
    )(pm, pm, ff, fbias, gq.reshape(1, width), gk.reshape(1, width))
    return qa, kta, r.reshape(nb, LANES)


def _fox_kernel(r_ref, q_ref, kt_ref, v_ref, o_ref, v2_ref, s_ref, m_ref, acc_ref, *, tk):
    h = pl.program_id(0)
    qi = pl.program_id(1)

    @pl.when(qi == 0)
    def _():
        v2_ref[:, :LANES] = v_ref[...]
        lane = lax.broadcasted_iota(jnp.int32, v_ref.shape, 1)
        v2_ref[:, LANES:] = jnp.where(lane == 0, 1.0, 0.0).astype(v2_ref.dtype)

    m_ref[...] = jnp.full(m_ref.shape, -jnp.inf, F32)
    acc_ref[...] = jnp.zeros(acc_ref.shape, F32)
    rr = lax.broadcasted_iota(jnp.int32, (tk, tk), 0)
    cc = lax.broadcasted_iota(jnp.int32, (tk, tk), 1)

    def scores(c, j, slot):
        s_ref[slot, c] = jnp.dot(q_ref[c * tk:(c + 1) * tk, :], kt_ref[j], preferred_element_type=F32)

    def fold(c, j, slot, diag):
        s = s_ref[slot, c]
        if diag:
            s = jnp.where(cc <= rr, s, -jnp.inf)
        d = r_ref[2 * qi + c, h] - r_ref[j, h]
        m = m_ref[c]
        m_new = jnp.maximum(m, jnp.max(s, axis=-1, keepdims=True) + d)
        p = jnp.exp2(s - (m_new - d))
        r0 = pl.multiple_of(j * tk, tk)
        pv = jnp.dot(p.astype(BF16), v2_ref[pl.ds(r0, tk), :], preferred_element_type=F32)
        acc_ref[c] = jnp.exp2(m - m_new) * acc_ref[c] + pv
        m_ref[c] = m_new

    scores(0, 0, 0)
    scores(1, 0, 0)

    def pair(j):
        for slot in range(2):
            scores(0, j + slot + 1, 1 - slot)
            scores(1, j + slot + 1, 1 - slot)
            fold(0, j + slot, slot, False)
            fold(1, j + slot, slot, False)

    def body(jp, carry):
        pair(2 * jp)
        return carry

    lax.fori_loop(0, qi, body, 0)
    scores(1, 2 * qi + 1, 1)
    fold(0, 2 * qi, 0, True)
    fold(1, 2 * qi, 0, False)
    fold(1, 2 * qi + 1, 1, True)
    for c in range(2):
        acc = acc_ref[c]
        o_ref[c * tk:(c + 1) * tk, :] = (acc[:, :LANES] / acc[:, LANES:LANES + 1]).astype(o_ref.dtype)


def _fox(qa, kta, r, pm, v_col, tk):
    s = qa.shape[0]
    nh = qa.shape[1] // FOX_KDIM
    nb = s // tk
    tq = 2 * tk
    kern = functools.partial(_fox_kernel, tk=tk)
    return pl.pallas_call(
        kern,
        out_shape=jax.ShapeDtypeStruct((s, nh * HEAD_DIM), BF16),
        grid=(nh, s // tq),
        in_specs=[pl.BlockSpec(memory_space=pltpu.SMEM),
                  pl.BlockSpec((tq, FOX_KDIM), lambda h, i: (i, h)),
                  pl.BlockSpec((nb, FOX_KDIM, tk), lambda h, i: (0, h, 0)),
                  pl.BlockSpec((s, HEAD_DIM), lambda h, i: (0, v_col // HEAD_DIM + h))],
        out_specs=pl.BlockSpec((tq, HEAD_DIM), lambda h, i: (i, h)),
        scratch_shapes=[pltpu.VMEM((s, FOX_KDIM), BF16),
                        pltpu.VMEM((2, 2, tk, tk), F32),
                        pltpu.VMEM((2, tk, 1), F32),
                        pltpu.VMEM((2, tk, FOX_KDIM), F32)],
        compiler_params=_cparams(("parallel", "arbitrary")),
        name="fox_attn",
    )(r, qa, kta, pm)


def _merge_kernel(oa_ref, ob_ref, wa_ref, wb_ref, ga_ref, gb_ref, y_ref):
    sa = _sigmoid(ga_ref[...].astype(F32))
    sb = _sigmoid(gb_ref[...].astype(F32))
    ya = jnp.dot(oa_ref[...], wa_ref[...], preferred_element_type=F32)
    yb = jnp.dot(ob_ref[...], wb_ref[...], preferred_element_type=F32)
    y_ref[...] = (sa * ya + sb * yb).astype(y_ref.dtype)


def _merge(oa, ob, wa, wb, pg):
    m, ka = oa.shape
    kb = ob.shape[1]
    n = wa.shape[1]
    tm = _tile(m, 1024, SUBLANES)
    tn = _tile(n, 1024)
    return pl.pallas_call(
        _merge_kernel,
        out_shape=jax.ShapeDtypeStruct((m, n), BF16),
        grid=(m // tm, n // tn),
        in_specs=[pl.BlockSpec((tm, ka), lambda i, j: (i, 0)),
                  pl.BlockSpec((tm, kb), lambda i, j: (i, 0)),
                  pl.BlockSpec((ka, tn), lambda i, j: (0, j)),
                  pl.BlockSpec((kb, tn), lambda i, j: (0, j)),
                  pl.BlockSpec((tm, tn), lambda i, j: (i, j)),
                  pl.BlockSpec((tm, tn), lambda i, j: (i, n // tn + j))],
        out_specs=pl.BlockSpec((tm, tn), lambda i, j: (i, j)),
        compiler_params=_cparams(("parallel", "arbitrary")),
        name="branch_merge",
    )(oa, ob, wa, wb, pg, pg)


def _resproj_kernel(a_ref, w_ref, x_ref, gt_ref, *refs):
    n_side = (len(refs) - 1) // 2
    o_ref = refs[n_side]
    y = jnp.dot(a_ref[...], w_ref[...], preferred_element_type=F32)
    o_ref[...] = x_ref[...] + gt_ref[...] * y
    _side_cast(refs[:n_side] + refs[n_side + 1:])


def _resproj(a, w, x, mod, gate_idx, tm_pref, tn_pref, name, side=()):
    m, k = a.shape
    n = w.shape[1]
    tm = _tile(m, tm_pref, SUBLANES)
    tn = _tile(n, tn_pref)
    side_in, side_out, side_shapes = _side_cast_specs(side, m // tm, n // tn)
    out = pl.pallas_call(
        _resproj_kernel,
        out_shape=[jax.ShapeDtypeStruct((m, n), F32)] + side_shapes,
        grid=(m // tm, n // tn),
        in_specs=[pl.BlockSpec((tm, k), lambda i, j: (i, 0)),
                  pl.BlockSpec((k, tn), lambda i, j: (0, j)),
                  pl.BlockSpec((tm, tn), lambda i, j: (i, j)),
                  pl.BlockSpec((1, tn), lambda i, j: (0, gate_idx * (n // tn) + j))] + side_in,
        out_specs=[pl.BlockSpec((tm, tn), lambda i, j: (i, j))] + side_out,
        compiler_params=_cparams(("arbitrary" if side else "parallel", "arbitrary")),
        name=name,
    )(a, w, x, mod, *_side_arrays(side))
    return out if side else out[0]


def _swiglu_kernel(h_ref, wg_ref, wu_ref, *refs):
    n_side = (len(refs) - 1) // 2
    o_ref = refs[n_side]
    h = h_ref[...]
    g = jnp.dot(h, wg_ref[...], preferred_element_type=F32)
    u = jnp.dot(h, wu_ref[...], preferred_element_type=F32)
    o_ref[...] = (g * _sigmoid(g) * u).astype(o_ref.dtype)
    _side_cast(refs[:n_side] + refs[n_side + 1:])


def _swiglu(h, w, side=()):
    m, k = h.shape
    dff = w.shape[1] // 2
    tm = _tile(m, 4096, SUBLANES)
    tn = _tile(dff, 512)
    nj = dff // tn
    side_in, side_out, side_shapes = _side_cast_specs(side, m // tm, nj)
    out = pl.pallas_call(
        _swiglu_kernel,
        out_shape=[jax.ShapeDtypeStruct((m, dff), BF16)] + side_shapes,
        grid=(m // tm, nj),
        in_specs=[pl.BlockSpec((tm, k), lambda i, j: (i, 0), pipeline_mode=pl.Buffered(1)),
                  pl.BlockSpec((k, tn), lambda i, j: (0, j)),
                  pl.BlockSpec((k, tn), lambda i, j: (0, nj + j))] + side_in,
        out_specs=[pl.BlockSpec((tm, tn), lambda i, j: (i, j))] + side_out,
        compiler_params=_cparams(("arbitrary" if side else "parallel", "arbitrary")),
        name="ffn_swiglu",
    )(h, w, w, *_side_arrays(side))
    return out if side else out[0]


def kernel(x, c, w_ada, b_ada, norm_mix_g, norm_ffn_g, w_in, fox_f_bias, hgrn_lb_table, hgrn_onorm_g,
           fox_q_norm_g, fox_k_norm_g, w_branch_a, w_branch_b, w_out, w_ffn_in, w_ffn_out):
    batch, seq, d = x.shape
    assert batch == 1 and w_ada.shape[0] == 1, "single sequence, single layer"
    hw = hgrn_onorm_g.shape[1] * hgrn_onorm_g.shape[2]
    fw = fox_q_norm_g.shape[1] * fox_q_norm_g.shape[2]
    nfh = fox_q_norm_g.shape[1]
    assert hgrn_onorm_g.shape[2] == HEAD_DIM and fox_q_norm_g.shape[2] == HEAD_DIM
    assert nfh <= LANES and hw == fw and seq % (2 * FOX_TK) == 0
    x2 = x.reshape(seq, d)

    f_col = 4 * hw + 3 * fw
    wt = jnp.swapaxes(w_in, 1, 2)[0]
    fbias = jnp.pad(fox_f_bias[0], (0, LANES - nfh)).reshape(1, LANES)

    c_col, w_ada2, b_ada2 = c.reshape(d, 1), w_ada[0], b_ada.reshape(1, b_ada.shape[-1])
    mod_a = _ada_mod(c_col, w_ada2, b_ada2, 2 * d)

    n_main = 2 * hw + 3 * fw
    h, ff, w_qf = _normmod_proj(x2, norm_mix_g[0], mod_a, 0, 1, wt, f_col, LANES,
                                side=((wt, 0, 2 * hw),))
    pq, w_main = _matmul(h, w_qf, 0, hw, BF16, "in_proj_q", tn_pref=512,
                         side=((wt, 2 * hw, n_main),))
    hf, w_gates = _matmul(h, w_qf, hw, hw, F32, "in_proj_f", tn_pref=512,
                          side=((wt, f_col, nfh + 2 * d),))
    pm, mod_b, w_out_bf, w_a_bf, w_b_bf = _matmul(
        h, w_main, 0, n_main, BF16, "in_proj_main", ada=(c_col, w_ada2, b_ada2, 2 * d, 4 * d),
        side=(w_out[0], w_branch_a[0], w_branch_b[0]))
    pg, w_ffn_in_bf = _matmul(h, w_gates, nfh, 2 * d, BF16, "in_proj_gates",
                              side=(w_ffn_in[0],))

    o_a = _hgrn(pq, pm, hf, hgrn_lb_table, hgrn_onorm_g[0], hw, 0, hw)

    qa, kta, r = _foxprep(pm, ff, fbias, fox_q_norm_g[0], fox_k_norm_g[0], fw, 2 * hw, 2 * hw + fw, FOX_TK)
    o_b = _fox(qa, kta, r, pm, 2 * hw + 2 * fw, FOX_TK)

    y = _merge(o_a, o_b, w_a_bf, w_b_bf, pg)
    x1, w_ffn_out_bf = _resproj(y, w_out_bf, x2, mod_b, 0, 1024, 512, "out_proj",
                                side=(w_ffn_out[0],))

    h2 = _normmod(x1, norm_ffn_g[0], mod_b, 1, 2)
    act = _swiglu(h2, w_ffn_in_bf)
    x3 = _resproj(act, w_ffn_out_bf, x1, mod_b, 3, 512, 512, "ffn_out")
    return x3.reshape(batch, seq, d)
```

```python
import functools
import math

import jax
import jax.numpy as jnp
from jax import lax
from jax.experimental import pallas as pl
from jax.experimental.pallas import tpu as pltpu

F32 = jnp.float32
BF16 = jnp.bfloat16

EPS = 1e-6
LOG2E = 1.4426950408889634
LANES = 128
SUBLANES = 8
BF16_SUBLANES = 16
HEAD_DIM = 128
HGRN_SUB = 16
HGRN_FSUB = 32
HGRN_UNROLL = 4
HGRN_MIN_LB = 2.0 ** -6
FOX_TK = 512
FOX_KDIM = 2 * HEAD_DIM
V7X_VMEM_LIMIT_BYTES = 56 * 1024 * 1024


def _cparams(semantics, vmem_bytes=V7X_VMEM_LIMIT_BYTES):
    return pltpu.CompilerParams(dimension_semantics=semantics, vmem_limit_bytes=vmem_bytes)


def _tile(dim, pref, align=LANES):
    if dim <= pref:
        return dim
    t = (pref // align) * align
    while t >= align:
        if dim % t == 0:
            return t
        t -= align
    raise ValueError(f"no {align}-aligned tile for {dim}")


def _sigmoid(x):
    return 1.0 / (1.0 + jnp.exp(-x))


def _ada_kernel(c_ref, w_ref, b_ref, o_ref):
    c = c_ref[...]
    s = c * _sigmoid(c)
    o_ref[...] = jnp.sum(w_ref[...] * s, axis=0, keepdims=True) + b_ref[...]


def _ada_mod(c_col, w_ada, b_ada, ncols):
    d = w_ada.shape[0]
    tn = _tile(ncols, 512)
    return pl.pallas_call(
        _ada_kernel,
        out_shape=jax.ShapeDtypeStruct((1, ncols), F32),
        grid=(ncols // tn,),
        in_specs=[pl.BlockSpec((d, 1), lambda j: (0, 0)),
                  pl.BlockSpec((d, tn), lambda j: (0, j)),
                  pl.BlockSpec((1, tn), lambda j: (0, j))],
        out_specs=pl.BlockSpec((1, tn), lambda j: (0, j)),
        compiler_params=_cparams(("parallel",)),
        name="ada_mod",
    )(c_col, w_ada, b_ada)


def _ada_side_specs(ada, n_i, n_j):
    c_col, w_ada, b_ada, col0, ncols = ada
    d = w_ada.shape[0]
    units = ncols // LANES
    n_blk = max(dv for dv in range(1, min(units, n_i * n_j) + 1) if units % dv == 0)
    cb = ncols // n_blk
    assert col0 % cb == 0
    blk = lambda i, j: jnp.minimum(i * n_j + j, n_blk - 1)
    in_specs = [pl.BlockSpec((d, 1), lambda i, j: (0, 0)),
                pl.BlockSpec((d, cb), lambda i, j: (0, col0 // cb + blk(i, j))),
                pl.BlockSpec((1, cb), lambda i, j: (0, col0 // cb + blk(i, j)))]
    out_spec = pl.BlockSpec((1, cb), lambda i, j: (0, blk(i, j)))
    return in_specs, out_spec, jax.ShapeDtypeStruct((1, ncols), F32)


def _normmod_kernel(x_ref, g_ref, sh_ref, sc_ref, o_ref):
    x = x_ref[...]
    ms = jnp.mean(x * x, axis=-1, keepdims=True)
    xn = x * lax.rsqrt(ms + EPS)
    o_ref[...] = (xn * g_ref[...] * (1.0 + sc_ref[...]) + sh_ref[...]).astype(o_ref.dtype)


def _normmod(x, gain, mod, shift_idx, scale_idx):
    s, d = x.shape
    tm = _tile(s, 512, SUBLANES)
    return pl.pallas_call(
        _normmod_kernel,
        out_shape=jax.ShapeDtypeStruct((s, d), BF16),
        grid=(s // tm,),
        in_specs=[pl.BlockSpec((tm, d), lambda i: (i, 0)),
                  pl.BlockSpec((1, d), lambda i: (0, 0)),
                  pl.BlockSpec((1, d), lambda i: (0, shift_idx)),
                  pl.BlockSpec((1, d), lambda i: (0, scale_idx))],
        out_specs=pl.BlockSpec((tm, d), lambda i: (i, 0)),
        compiler_params=_cparams(("parallel",)),
        name="normmod",
    )(x, gain.reshape(1, d), mod, mod)


def _normmod_proj_kernel(x_ref, g_ref, sh_ref, sc_ref, wt_ref, *refs):
    n_side = (len(refs) - 2) // 2
    h_ref, o_ref = refs[n_side], refs[n_side + 1]
    _normmod_kernel(x_ref, g_ref, sh_ref, sc_ref, h_ref)
    o_ref[...] = lax.dot_general(h_ref[...], wt_ref[...].astype(BF16), (((1,), (1,)), ((), ())),
                                 preferred_element_type=F32)
    _side_cast(refs[:n_side] + refs[n_side + 2:])


def _normmod_proj(x, gain, mod, shift_idx, scale_idx, wt, row0, nrows, side):
    s, d = x.shape
    tm = _tile(s, 512, SUBLANES)
    n_i = s // tm
    side_in, side_out, side_shapes = _side_cast_specs(side, n_i, 1)
    return pl.pallas_call(
        _normmod_proj_kernel,
        out_shape=[jax.ShapeDtypeStruct((s, d), BF16), jax.ShapeDtypeStruct((s, nrows), F32)] + side_shapes,
        grid=(n_i, 1),
        in_specs=[pl.BlockSpec((tm, d), lambda i, j: (i, 0)),
                  pl.BlockSpec((1, d), lambda i, j: (0, 0)),
                  pl.BlockSpec((1, d), lambda i, j: (0, shift_idx)),
                  pl.BlockSpec((1, d), lambda i, j: (0, scale_idx)),
                  pl.BlockSpec((pl.Element(nrows), pl.Element(d)), lambda i, j: (row0, 0))] + side_in,
        out_specs=[pl.BlockSpec((tm, d), lambda i, j: (i, 0)),
                   pl.BlockSpec((tm, nrows), lambda i, j: (i, 0))] + side_out,
        compiler_params=_cparams(("arbitrary", "arbitrary")),
        name="normmod_proj_ff",
    )(x, gain.reshape(1, d), mod, mod, wt, *_side_arrays(side))


def _side_cast_specs(side, n_i, n_j):
    in_specs, out_specs, shapes = [], [], []
    for item in side:
        w, row0, rows = item if isinstance(item, tuple) else (item, 0, item.shape[0])
        cols = w.shape[1]
        assert rows % BF16_SUBLANES == 0 and row0 % BF16_SUBLANES == 0, (row0, rows)
        units = rows // BF16_SUBLANES
        n_slabs = max(dv for dv in range(1, min(units, n_i * n_j) + 1) if units % dv == 0)
        slab = rows // n_slabs

        def slab_index(i, j, n_slabs=n_slabs):
            return jnp.minimum(i * n_j + j, n_slabs - 1)

        in_specs.append(pl.BlockSpec(
            (pl.Element(slab), pl.Element(cols)),
            lambda i, j, f=slab_index, row0=row0, slab=slab:
                (pl.multiple_of(row0 + f(i, j) * slab, BF16_SUBLANES), 0)))
        out_specs.append(pl.BlockSpec((slab, cols), lambda i, j, f=slab_index: (f(i, j), 0)))
        shapes.append(jax.ShapeDtypeStruct((rows, cols), BF16))
    return in_specs, out_specs, shapes


def _side_cast(side_refs):
    n = len(side_refs) // 2
    for src, dst in zip(side_refs[:n], side_refs[n:]):
        dst[...] = src[...].astype(dst.dtype)


def _mm_kernel(x_ref, wt_ref, *refs, with_ada):
    n_ada = 3 if with_ada else 0
    n_side = (len(refs) - n_ada - 1 - (1 if with_ada else 0)) // 2
    ada_in, side_in = refs[:n_ada], refs[n_ada:n_ada + n_side]
    outs = refs[n_ada + n_side:]
    o_ref = outs[0]
    y = lax.dot_general(x_ref[...], wt_ref[...].astype(BF16), (((1,), (1,)), ((), ())),
                        preferred_element_type=F32)
    o_ref[...] = y.astype(o_ref.dtype)
    if with_ada:
        _ada_kernel(*ada_in, outs[1])
    _side_cast(side_in + outs[1 + (1 if with_ada else 0):])


def _side_arrays(side):
    return [item[0] if isinstance(item, tuple) else item for item in side]


def _matmul(x, wt, row0, nrows, out_dtype, name, side=(), tn_pref=1024, ada=None):
    m, k = x.shape
    assert row0 % BF16_SUBLANES == 0
    tm = _tile(m, 1024, SUBLANES)
    tn = _tile(nrows, tn_pref)
    n_i, n_j = m // tm, nrows // tn
    side_in, side_out, side_shapes = _side_cast_specs(side, n_i, n_j)
    ada_in, ada_out, ada_shape, ada_args = [], [], [], []
    if ada is not None:
        ada_in, out_spec, shape = _ada_side_specs(ada, n_i, n_j)
        ada_out, ada_shape, ada_args = [out_spec], [shape], list(ada[:3])
    has_jobs = bool(side) or ada is not None
    out = pl.pallas_call(
        functools.partial(_mm_kernel, with_ada=ada is not None),
        out_shape=[jax.ShapeDtypeStruct((m, nrows), out_dtype)] + ada_shape + side_shapes,
        grid=(n_i, n_j),
        in_specs=[pl.BlockSpec((tm, k), lambda i, j: (i, 0)),
                  pl.BlockSpec((pl.Element(tn), pl.Element(k)),
                               lambda i, j: (pl.multiple_of(row0 + j * tn, BF16_SUBLANES), 0))]
                 + ada_in + side_in,
        out_specs=[pl.BlockSpec((tm, tn), lambda i, j: (i, j))] + ada_out + side_out,
        compiler_params=_cparams(("arbitrary" if has_jobs else "parallel", "arbitrary")),
        name=name,
    )(x, wt, *ada_args, *_side_arrays(side))
    return out if has_jobs else out[0]


def _hgrn_kernel(q_ref, i_ref, g_ref, f_ref, lbt_ref, og_ref, o_ref, st_ref, *, hb, tt):
    @pl.when(pl.program_id(1) == 0)
    def _():
        st_ref[...] = jnp.zeros_like(st_ref)

    tbl = lbt_ref[...]
    e = jnp.exp(tbl - jnp.max(tbl, axis=0, keepdims=True))
    lb = e[0:1, :] / jnp.sum(e, axis=0, keepdims=True)
    half = SUBLANES
    nt_dims = (((1,), (1,)), ((), ()))
    tn_dims = (((0,), (0,)), ((), ()))

    def gates(rows, cs, nrows):
        lbh = lb[:, cs]
        f = lbh + (1.0 - lbh) * _sigmoid(f_ref[rows, cs])
        b = jnp.log2(f)
        row = lax.broadcasted_iota(jnp.int32, (nrows, LANES), 0)
        sh = 1
        while sh < nrows:
            b = b + jnp.where(row >= sh, pltpu.roll(b, sh, axis=0), 0.0)
            sh *= 2
        return 1.0 - f, b

    def finish(o, rows, cs):
        ms = jnp.mean(o * o, axis=-1, keepdims=True)
        on = o * lax.rsqrt(ms + EPS) * og_ref[:, cs]
        gate = g_ref[rows, cs].astype(F32)
        o_ref[rows, cs] = (on * (gate * _sigmoid(gate))).astype(o_ref.dtype)

    mid = HGRN_FSUB // 2
    tri_r = lax.broadcasted_iota(jnp.int32, (HGRN_FSUB, HGRN_FSUB), 0)
    tri_c = lax.broadcasted_iota(jnp.int32, (HGRN_FSUB, HGRN_FSUB), 1)

    def body_factored(n, carry):
        pending = []
        for u, h in [(u, h) for u in range(HGRN_UNROLL) for h in range(hb)]:
            r0 = pl.multiple_of((n * HGRN_UNROLL + u) * HGRN_FSUB, HGRN_FSUB)
            rows = pl.ds(r0, HGRN_FSUB)
            cs = slice(h * LANES, (h + 1) * LANES)
            q = q_ref[rows, cs].astype(F32)
            v = i_ref[rows, cs]
            k, b = gates(rows, cs, HGRN_FSUB)
            r = b[mid - 1:mid, :]
            b_last = b[HGRN_FSUB - 1:HGRN_FSUB, :]
            qc = q * jnp.exp2(b - r)
            kc = (k * jnp.exp2(r - b)).astype(BF16)
            a = lax.dot_general(qc.astype(BF16), kc, nt_dims, preferred_element_type=F32)
            st = st_ref[h]
            o = lax.dot_general((qc * jnp.exp2(r)).astype(BF16), st.astype(BF16), nt_dims,
                                preferred_element_type=F32)
            kt = (k * jnp.exp2(b_last - b)).astype(BF16)
            upd = lax.dot_general(v, kt, tn_dims, preferred_element_type=F32)
            st_ref[h] = st * jnp.exp2(b_last) + upd
            pending.append((a, o, v, rows, cs))
        for a, o, v, rows, cs in pending:
            p = jnp.where(tri_c <= tri_r, a, 0.0).astype(BF16)
            finish(o + jnp.dot(p, v, preferred_element_type=F32), rows, cs)
        return carry

    rowc = lax.broadcasted_iota(jnp.int32, (SUBLANES, 1), 0)

    def body_exact(n, carry):
        r0 = pl.multiple_of(n * HGRN_SUB, HGRN_SUB)
        rows = pl.ds(r0, HGRN_SUB)
        for h in range(hb):
            cs = slice(h * LANES, (h + 1) * LANES)
            q = q_ref[rows, cs].astype(F32)
            v = i_ref[rows, cs]
            vf = v.astype(F32)
            k, b = gates(rows, cs, HGRN_SUB)
            b_last = b[HGRN_SUB - 1:HGRN_SUB, :]
            st = st_ref[h]
            qt = (q * jnp.exp2(b)).astype(BF16)
            o = lax.dot_general(qt, st.astype(BF16), nt_dims, preferred_element_type=F32)
            o_lo, o_hi = o[:half], o[half:]
            q_lo, q_hi = q[:half], q[half:]
            b_lo, b_hi = b[:half], b[half:]
            for s in range(HGRN_SUB):
                ks, bs, vs = k[s:s + 1, :], b[s:s + 1, :], vf[s:s + 1, :]
                c_hi = jnp.sum(q_hi * ks * jnp.exp2(b_hi - bs), axis=-1, keepdims=True)
                if s < half:
                    c_lo = jnp.sum(q_lo * ks * jnp.exp2(b_lo - bs), axis=-1, keepdims=True)
                    o_lo = o_lo + jnp.where(rowc >= s, c_lo, 0.0) * vs
                else:
                    c_hi = jnp.where(rowc >= s - half, c_hi, 0.0)
                o_hi = o_hi + c_hi * vs
            kt = (k * jnp.exp2(b_last - b)).astype(BF16)
            upd = lax.dot_general(v, kt, tn_dims, preferred_element_type=F32)
            st_ref[h] = st * jnp.exp2(b_last) + upd
            finish(jnp.concatenate([o_lo, o_hi], axis=0), rows, cs)
        return carry

    factorable = jnp.min(lb) >= HGRN_MIN_LB

    @pl.when(factorable)
    def _():
        lax.fori_loop(0, tt // (HGRN_FSUB * HGRN_UNROLL), body_factored, 0)

    @pl.when(jnp.logical_not(factorable))
    def _():
        lax.fori_loop(0, tt // HGRN_SUB, body_exact, 0)


def _hgrn(pq, pm, hf, lb_table, onorm_g, width, i_col, g_col):
    s = pq.shape[0]
    hb = min(16, width // HEAD_DIM)
    bw = hb * HEAD_DIM
    tt = _tile(s, 512, HGRN_SUB)
    nl = lb_table.shape[0]
    kern = functools.partial(_hgrn_kernel, hb=hb, tt=tt)
    return pl.pallas_call(
        kern,
        out_shape=jax.ShapeDtypeStruct((s, width), BF16),
        grid=(width // bw, s // tt),
        in_specs=[pl.BlockSpec((tt, bw), lambda h, t: (t, h)),
                  pl.BlockSpec((tt, bw), lambda h, t: (t, i_col // bw + h)),
                  pl.BlockSpec((tt, bw), lambda h, t: (t, g_col // bw + h)),
                  pl.BlockSpec((tt, bw), lambda h, t: (t, h)),
                  pl.BlockSpec((nl, bw), lambda h, t: (0, h)),
                  pl.BlockSpec((1, bw), lambda h, t: (0, h))],
        out_specs=pl.BlockSpec((tt, bw), lambda h, t: (t, h)),
        scratch_shapes=[pltpu.VMEM((hb, HEAD_DIM, HEAD_DIM), F32)],
        compiler_params=_cparams(("parallel", "arbitrary")),
        name="hgrn2",
    )(pq, pm, pm, hf, lb_table, onorm_g.reshape(1, width))


def _split3(x):
    hi = x.astype(BF16).astype(F32)
    r = x - hi
    mid = r.astype(BF16).astype(F32)
    lo = (r - mid).astype(BF16).astype(F32)
    return hi, mid, lo


def _foxprep_kernel(q_ref, k_ref, ff_ref, fb_ref, gq_ref, gk_ref, qo_ref, kto_ref, r_ref, carry_ref, *, nh, scale):
    @pl.when(pl.program_id(0) == 0)
    def _():
        carry_ref[...] = jnp.zeros_like(carry_ref)

    tk = q_ref.shape[0]
    x = ff_ref[...] + fb_ref[...]
    ls = (jnp.minimum(x, 0.0) - jnp.log(1.0 + jnp.exp(-jnp.abs(x)))) * LOG2E
    ri = lax.broadcasted_iota(jnp.int32, (tk, tk), 0)
    ci = lax.broadcasted_iota(jnp.int32, (tk, tk), 1)
    lower = (ci <= ri).astype(F32)
    rel = jnp.dot(lower, ls, precision=lax.Precision.HIGHEST, preferred_element_type=F32)
    base = carry_ref[...]
    r_ref[0] = base
    carry_ref[...] = base + rel[tk - 1:tk, :]

    cq = _split3(rel)
    ck = _split3(rel.T)
    lane = lax.broadcasted_iota(jnp.int32, (tk, LANES), 1)
    sub = lax.broadcasted_iota(jnp.int32, (LANES, tk), 0)
    ones_q = jnp.where((lane >= 3) & (lane < 6), 1.0, 0.0)
    ones_k = jnp.where(sub < 3, 1.0, 0.0)
    for h in range(nh):
        cs = slice(h * LANES, (h + 1) * LANES)
        q = q_ref[:, cs].astype(F32)
        qn = q * lax.rsqrt(jnp.mean(q * q, axis=-1, keepdims=True) + EPS) * gq_ref[:, cs]
        k = k_ref[:, cs].astype(F32)
        kn = k * lax.rsqrt(jnp.mean(k * k, axis=-1, keepdims=True) + EPS) * gk_ref[:, cs]
        eq = ones_q
        ek = ones_k
        for t in range(3):
            eq = jnp.where(lane == t, cq[t][:, h:h + 1], eq)
            ek = jnp.where(sub == 3 + t, -ck[t][h:h + 1, :], ek)
        c0 = h * FOX_KDIM
        qo_ref[:, c0:c0 + LANES] = (qn * scale).astype(qo_ref.dtype)
        qo_ref[:, c0 + LANES:c0 + FOX_KDIM] = eq.astype(qo_ref.dtype)
        kto_ref[0, c0:c0 + LANES, :] = kn.T.astype(kto_ref.dtype)
        kto_ref[0, c0 + LANES:c0 + FOX_KDIM, :] = ek.astype(kto_ref.dtype)


def _foxprep(pm, ff, fbias, gq, gk, width, q_col, k_col, tk):
    s = pm.shape[0]
    nh = width // HEAD_DIM
    nb = s // tk
    kern = functools.partial(_foxprep_kernel, nh=nh, scale=LOG2E * float(HEAD_DIM) ** -0.5)
    qa, kta, r = pl.pallas_call(
        kern,
        out_shape=(jax.ShapeDtypeStruct((s, nh * FOX_KDIM), BF16),
                   jax.ShapeDtypeStruct((nb, nh * FOX_KDIM, tk), BF16),
                   jax.ShapeDtypeStruct((nb, 1, LANES), F32)),
        grid=(nb,),
        in_specs=[pl.BlockSpec((tk, width), lambda i: (i, q_col // width)),
                  pl.BlockSpec((tk, width), lambda i: (i, k_col // width)),
                  pl.BlockSpec((tk, LANES), lambda i: (i, 0)),
                  pl.BlockSpec((1, LANES), lambda i: (0, 0)),
                  pl.BlockSpec((1, width), lambda i: (0, 0)),
                  pl.BlockSpec((1, width), lambda i: (0, 0))],
        out_specs=(pl.BlockSpec((tk, nh * FOX_KDIM), lambda i: (i, 0)),
                   pl.BlockSpec((1, nh * FOX_KDIM, tk), lambda i: (i, 0, 0)),
                   pl.BlockSpec((1, 1, LANES), lambda i: (i, 0, 0))),
        scratch_shapes=[pltpu.VMEM((1, LANES), F32)],
        compiler_params=_cparams(("arbitrary",)),
        name="fox_prep",
    )(pm, pm, ff, fbias, gq.reshape(1, width), gk.reshape(1, width))
    return qa, kta, r.reshape(nb, LANES)


def _fox_kernel(r_ref, q_ref, kt_ref, v_ref, *refs, tk, n_side):
    o_ref = refs[n_side]
    v2_ref, s_ref, m_ref, acc_ref = refs[2 * n_side + 1:]
    _side_cast(refs[:n_side] + refs[n_side + 1:2 * n_side + 1])
    h = pl.program_id(0)
    qi = pl.program_id(1)

    @pl.when(qi == 0)
    def _():
        v2_ref[:, :LANES] = v_ref[...]
        lane = lax.broadcasted_iota(jnp.int32, v_ref.shape, 1)
        v2_ref[:, LANES:] = jnp.where(lane == 0, 1.0, 0.0).astype(v2_ref.dtype)

    m_ref[...] = jnp.full(m_ref.shape, -jnp.inf, F32)
    acc_ref[...] = jnp.zeros(acc_ref.shape, F32)
    rr = lax.broadcasted_iota(jnp.int32, (tk, tk), 0)
    cc = lax.broadcasted_iota(jnp.int32, (tk, tk), 1)

    def scores(c, j, slot):
        s_ref[slot, c] = jnp.dot(q_ref[c * tk:(c + 1) * tk, :], kt_ref[j], preferred_element_type=F32)

    def fold(c, j, slot, diag):
        s = s_ref[slot, c]
        if diag:
            s = jnp.where(cc <= rr, s, -jnp.inf)
        d = r_ref[2 * qi + c, h] - r_ref[j, h]
        m = m_ref[c]
        m_new = jnp.maximum(m, jnp.max(s, axis=-1, keepdims=True) + d)
        p = jnp.exp2(s - (m_new - d))
        r0 = pl.multiple_of(j * tk, tk)
        pv = jnp.dot(p.astype(BF16), v2_ref[pl.ds(r0, tk), :], preferred_element_type=F32)
        acc_ref[c] = jnp.exp2(m - m_new) * acc_ref[c] + pv
        m_ref[c] = m_new

    scores(0, 0, 0)
    scores(1, 0, 0)

    def pair(j):
        for slot in range(2):
            scores(0, j + slot + 1, 1 - slot)
            scores(1, j + slot + 1, 1 - slot)
            fold(0, j + slot, slot, False)
            fold(1, j + slot, slot, False)

    def body(jp, carry):
        pair(2 * jp)
        return carry

    lax.fori_loop(0, qi, body, 0)
    scores(1, 2 * qi + 1, 1)
    fold(0, 2 * qi, 0, True)
    fold(1, 2 * qi, 0, False)
    fold(1, 2 * qi + 1, 1, True)
    for c in range(2):
        acc = acc_ref[c]
        o_ref[c * tk:(c + 1) * tk, :] = (acc[:, :LANES] / acc[:, LANES:LANES + 1]).astype(o_ref.dtype)


def _fox(qa, kta, r, pm, v_col, tk, side=()):
    s = qa.shape[0]
    nh = qa.shape[1] // FOX_KDIM
    nb = s // tk
    tq = 2 * tk
    kern = functools.partial(_fox_kernel, tk=tk, n_side=len(side))
    side_in, side_out, side_shapes = _side_cast_specs(side, nh, s // tq)
    out = pl.pallas_call(
        kern,
        out_shape=[jax.ShapeDtypeStruct((s, nh * HEAD_DIM), BF16)] + side_shapes,
        grid=(nh, s // tq),
        in_specs=[pl.BlockSpec(memory_space=pltpu.SMEM),
                  pl.BlockSpec((tq, FOX_KDIM), lambda h, i: (i, h)),
                  pl.BlockSpec((nb, FOX_KDIM, tk), lambda h, i: (0, h, 0)),
                  pl.BlockSpec((s, HEAD_DIM), lambda h, i: (0, v_col // HEAD_DIM + h))] + side_in,
        out_specs=[pl.BlockSpec((tq, HEAD_DIM), lambda h, i: (i, h))] + side_out,
        scratch_shapes=[pltpu.VMEM((s, FOX_KDIM), BF16),
                        pltpu.VMEM((2, 2, tk, tk), F32),
                        pltpu.VMEM((2, tk, 1), F32),
                        pltpu.VMEM((2, tk, FOX_KDIM), F32)],
        compiler_params=_cparams(("arbitrary" if side else "parallel", "arbitrary")),
        name="fox_attn",
    )(r, qa, kta, pm, *_side_arrays(side))
    return out if side else out[0]


def _merge_kernel(oa_ref, ob_ref, wa_ref, wb_ref, ga_ref, gb_ref, y_ref):
    sa = _sigmoid(ga_ref[...].astype(F32))
    sb = _sigmoid(gb_ref[...].astype(F32))
    ya = jnp.dot(oa_ref[...], wa_ref[...], preferred_element_type=F32)
    yb = jnp.dot(ob_ref[...], wb_ref[...], preferred_element_type=F32)
    y_ref[...] = (sa * ya + sb * yb).astype(y_ref.dtype)


def _merge(oa, ob, wa, wb, pg):
    m, ka = oa.shape
    kb = ob.shape[1]
    n = wa.shape[1]
    tm = _tile(m, 1024, SUBLANES)
    tn = _tile(n, 1024)
    return pl.pallas_call(
        _merge_kernel,
        out_shape=jax.ShapeDtypeStruct((m, n), BF16),
        grid=(m // tm, n // tn),
        in_specs=[pl.BlockSpec((tm, ka), lambda i, j: (i, 0)),
                  pl.BlockSpec((tm, kb), lambda i, j: (i, 0)),
                  pl.BlockSpec((ka, tn), lambda i, j: (0, j)),
                  pl.BlockSpec((kb, tn), lambda i, j: (0, j)),
                  pl.BlockSpec((tm, tn), lambda i, j: (i, j)),
                  pl.BlockSpec((tm, tn), lambda i, j: (i, n // tn + j))],
        out_specs=pl.BlockSpec((tm, tn), lambda i, j: (i, j)),
        compiler_params=_cparams(("parallel", "arbitrary")),
        name="branch_merge",
    )(oa, ob, wa, wb, pg, pg)


def _resproj_kernel(a_ref, w_ref, x_ref, gt_ref, *refs):
    n_side = (len(refs) - 1) // 2
    o_ref = refs[n_side]
    y = jnp.dot(a_ref[...], w_ref[...], preferred_element_type=F32)
    o_ref[...] = x_ref[...] + gt_ref[...] * y
    _side_cast(refs[:n_side] + refs[n_side + 1:])


def _resproj(a, w, x, mod, gate_idx, tm_pref, tn_pref, name, side=()):
    m, k = a.shape
    n = w.shape[1]
    tm = _tile(m, tm_pref, SUBLANES)
    tn = _tile(n, tn_pref)
    side_in, side_out, side_shapes = _side_cast_specs(side, m // tm, n // tn)
    out = pl.pallas_call(
        _resproj_kernel,
        out_shape=[jax.ShapeDtypeStruct((m, n), F32)] + side_shapes,
        grid=(m // tm, n // tn),
        in_specs=[pl.BlockSpec((tm, k), lambda i, j: (i, 0)),
                  pl.BlockSpec((k, tn), lambda i, j: (0, j)),
                  pl.BlockSpec((tm, tn), lambda i, j: (i, j)),
                  pl.BlockSpec((1, tn), lambda i, j: (0, gate_idx * (n // tn) + j))] + side_in,
        out_specs=[pl.BlockSpec((tm, tn), lambda i, j: (i, j))] + side_out,
        compiler_params=_cparams(("arbitrary" if side else "parallel", "arbitrary")),
        name=name,
    )(a, w, x, mod, *_side_arrays(side))
    return out if side else out[0]


def _swiglu_kernel(h_ref, wg_ref, wu_ref, *refs):
    n_side = (len(refs) - 1) // 2
    o_ref = refs[n_side]
    h = h_ref[...]
    g = jnp.dot(h, wg_ref[...], preferred_element_type=F32)
    u = jnp.dot(h, wu_ref[...], preferred_element_type=F32)
    o_ref[...] = (g * _sigmoid(g) * u).astype(o_ref.dtype)
    _side_cast(refs[:n_side] + refs[n_side + 1:])


def _swiglu(h, w, side=()):
    m, k = h.shape
    dff = w.shape[1] // 2
    tm = _tile(m, 4096, SUBLANES)
    tn = _tile(dff, 512)
    nj = dff // tn
    side_in, side_out, side_shapes = _side_cast_specs(side, m // tm, nj)
    out = pl.pallas_call(
        _swiglu_kernel,
        out_shape=[jax.ShapeDtypeStruct((m, dff), BF16)] + side_shapes,
        grid=(m // tm, nj),
        in_specs=[pl.BlockSpec((tm, k), lambda i, j: (i, 0), pipeline_mode=pl.Buffered(1)),
                  pl.BlockSpec((k, tn), lambda i, j: (0, j)),
                  pl.BlockSpec((k, tn), lambda i, j: (0, nj + j))] + side_in,
        out_specs=[pl.BlockSpec((tm, tn), lambda i, j: (i, j))] + side_out,
        compiler_params=_cparams(("arbitrary" if side else "parallel", "arbitrary")),
        name="ffn_swiglu",
    )(h, w, w, *_side_arrays(side))
    return out if side else out[0]


def kernel(x, c, w_ada, b_ada, norm_mix_g, norm_ffn_g, w_in, fox_f_bias, hgrn_lb_table, hgrn_onorm_g,
           fox_q_norm_g, fox_k_norm_g, w_branch_a, w_branch_b, w_out, w_ffn_in, w_ffn_out):
    batch, seq, d = x.shape
    assert batch == 1 and w_ada.shape[0] == 1, "single sequence, single layer"
    hw = hgrn_onorm_g.shape[1] * hgrn_onorm_g.shape[2]
    fw = fox_q_norm_g.shape[1] * fox_q_norm_g.shape[2]
    nfh = fox_q_norm_g.shape[1]
    assert hgrn_onorm_g.shape[2] == HEAD_DIM and fox_q_norm_g.shape[2] == HEAD_DIM
    assert nfh <= LANES and hw == fw and seq % (2 * FOX_TK) == 0
    x2 = x.reshape(seq, d)

    f_col = 4 * hw + 3 * fw
    wt = jnp.swapaxes(w_in, 1, 2)[0]
    fbias = jnp.pad(fox_f_bias[0], (0, LANES - nfh)).reshape(1, LANES)

    c_col, w_ada2, b_ada2 = c.reshape(d, 1), w_ada[0], b_ada.reshape(1, b_ada.shape[-1])
    mod_a = _ada_mod(c_col, w_ada2, b_ada2, 2 * d)

    n_main = 2 * hw + 3 * fw
    h, ff, w_qf = _normmod_proj(x2, norm_mix_g[0], mod_a, 0, 1, wt, f_col, LANES,
                                side=((wt, 0, 2 * hw),))
    pq, w_main = _matmul(h, w_qf, 0, hw, BF16, "in_proj_q", tn_pref=512,
                         side=((wt, 2 * hw, n_main),))
    hf, w_gates = _matmul(h, w_qf, hw, hw, F32, "in_proj_f", tn_pref=512,
                          side=((wt, f_col, nfh + 2 * d),))
    pm, mod_b, w_out_bf, w_a_bf, w_b_bf = _matmul(
        h, w_main, 0, n_main, BF16, "in_proj_main", ada=(c_col, w_ada2, b_ada2, 2 * d, 4 * d),
        side=(w_out[0], w_branch_a[0], w_branch_b[0]))
    pg, w_ffn_in_bf = _matmul(h, w_gates, nfh, 2 * d, BF16, "in_proj_gates",
                              side=(w_ffn_in[0],))

    o_a = _hgrn(pq, pm, hf, hgrn_lb_table, hgrn_onorm_g[0], hw, 0, hw)

    qa, kta, r = _foxprep(pm, ff, fbias, fox_q_norm_g[0], fox_k_norm_g[0], fw, 2 * hw, 2 * hw + fw, FOX_TK)
    o_b, w_ffn_out_bf = _fox(qa, kta, r, pm, 2 * hw + 2 * fw, FOX_TK, side=(w_ffn_out[0],))

    y = _merge(o_a, o_b, w_a_bf, w_b_bf, pg)
    x1 = _resproj(y, w_out_bf, x2, mod_b, 0, 1024, 512, "out_proj")

    h2 = _normmod(x1, norm_ffn_g[0], mod_b, 1, 2)
    act = _swiglu(h2, w_ffn_in_bf)
    x3 = _resproj(act, w_ffn_out_bf, x1, mod_b, 3, 512, 512, "ffn_out")
    return x3.reshape(batch, seq, d)
```

```python
import functools
import math

import jax
import jax.numpy as jnp
from jax import lax
from jax.experimental import pallas as pl
from jax.experimental.pallas import tpu as pltpu

F32 = jnp.float32
BF16 = jnp.bfloat16

EPS = 1e-6
LOG2E = 1.4426950408889634
LANES = 128
SUBLANES = 8
BF16_SUBLANES = 16
HEAD_DIM = 128
HGRN_SUB = 16
HGRN_FSUB = 32
HGRN_UNROLL = 8
HGRN_MIN_LB = 2.0 ** -6
FOX_TK = 512
FOX_KDIM = 2 * HEAD_DIM
V7X_VMEM_LIMIT_BYTES = 56 * 1024 * 1024


def _cparams(semantics, vmem_bytes=V7X_VMEM_LIMIT_BYTES):
    return pltpu.CompilerParams(dimension_semantics=semantics, vmem_limit_bytes=vmem_bytes)


def _tile(dim, pref, align=LANES):
    if dim <= pref:
        return dim
    t = (pref // align) * align
    while t >= align:
        if dim % t == 0:
            return t
        t -= align
    raise ValueError(f"no {align}-aligned tile for {dim}")


def _sigmoid(x):
    return 1.0 / (1.0 + jnp.exp(-x))


def _ada_kernel(c_ref, w_ref, b_ref, o_ref):
    c = c_ref[...]
    s = c * _sigmoid(c)
    o_ref[...] = jnp.sum(w_ref[...] * s, axis=0, keepdims=True) + b_ref[...]


def _ada_mod(c_col, w_ada, b_ada, ncols):
    d = w_ada.shape[0]
    tn = _tile(ncols, 512)
    return pl.pallas_call(
        _ada_kernel,
        out_shape=jax.ShapeDtypeStruct((1, ncols), F32),
        grid=(ncols // tn,),
        in_specs=[pl.BlockSpec((d, 1), lambda j: (0, 0)),
                  pl.BlockSpec((d, tn), lambda j: (0, j)),
                  pl.BlockSpec((1, tn), lambda j: (0, j))],
        out_specs=pl.BlockSpec((1, tn), lambda j: (0, j)),
        compiler_params=_cparams(("parallel",)),
        name="ada_mod",
    )(c_col, w_ada, b_ada)


def _ada_side_specs(ada, n_i, n_j):
    c_col, w_ada, b_ada, col0, ncols = ada
    d = w_ada.shape[0]
    units = ncols // LANES
    n_blk = max(dv for dv in range(1, min(units, n_i * n_j) + 1) if units % dv == 0)
    cb = ncols // n_blk
    assert col0 % cb == 0
    blk = lambda i, j: jnp.minimum(i * n_j + j, n_blk - 1)
    in_specs = [pl.BlockSpec((d, 1), lambda i, j: (0, 0)),
                pl.BlockSpec((d, cb), lambda i, j: (0, col0 // cb + blk(i, j))),
                pl.BlockSpec((1, cb), lambda i, j: (0, col0 // cb + blk(i, j)))]
    out_spec = pl.BlockSpec((1, cb), lambda i, j: (0, blk(i, j)))
    return in_specs, out_spec, jax.ShapeDtypeStruct((1, ncols), F32)


def _normmod_kernel(x_ref, g_ref, sh_ref, sc_ref, o_ref):
    x = x_ref[...]
    ms = jnp.mean(x * x, axis=-1, keepdims=True)
    xn = x * lax.rsqrt(ms + EPS)
    o_ref[...] = (xn * g_ref[...] * (1.0 + sc_ref[...]) + sh_ref[...]).astype(o_ref.dtype)


def _normmod(x, gain, mod, shift_idx, scale_idx):
    s, d = x.shape
    tm = _tile(s, 512, SUBLANES)
    return pl.pallas_call(
        _normmod_kernel,
        out_shape=jax.ShapeDtypeStruct((s, d), BF16),
        grid=(s // tm,),
        in_specs=[pl.BlockSpec((tm, d), lambda i: (i, 0)),
                  pl.BlockSpec((1, d), lambda i: (0, 0)),
                  pl.BlockSpec((1, d), lambda i: (0, shift_idx)),
                  pl.BlockSpec((1, d), lambda i: (0, scale_idx))],
        out_specs=pl.BlockSpec((tm, d), lambda i: (i, 0)),
        compiler_params=_cparams(("parallel",)),
        name="normmod",
    )(x, gain.reshape(1, d), mod, mod)


def _normmod_proj_kernel(x_ref, g_ref, sh_ref, sc_ref, wt_ref, *refs):
    n_side = (len(refs) - 2) // 2
    h_ref, o_ref = refs[n_side], refs[n_side + 1]
    _normmod_kernel(x_ref, g_ref, sh_ref, sc_ref, h_ref)
    o_ref[...] = lax.dot_general(h_ref[...], wt_ref[...].astype(BF16), (((1,), (1,)), ((), ())),
                                 preferred_element_type=F32)
    _side_cast(refs[:n_side] + refs[n_side + 2:])


def _normmod_proj(x, gain, mod, shift_idx, scale_idx, wt, row0, nrows, side):
    s, d = x.shape
    tm = _tile(s, 512, SUBLANES)
    n_i = s // tm
    side_in, side_out, side_shapes = _side_cast_specs(side, n_i, 1)
    return pl.pallas_call(
        _normmod_proj_kernel,
        out_shape=[jax.ShapeDtypeStruct((s, d), BF16), jax.ShapeDtypeStruct((s, nrows), F32)] + side_shapes,
        grid=(n_i, 1),
        in_specs=[pl.BlockSpec((tm, d), lambda i, j: (i, 0)),
                  pl.BlockSpec((1, d), lambda i, j: (0, 0)),
                  pl.BlockSpec((1, d), lambda i, j: (0, shift_idx)),
                  pl.BlockSpec((1, d), lambda i, j: (0, scale_idx)),
                  pl.BlockSpec((pl.Element(nrows), pl.Element(d)), lambda i, j: (row0, 0))] + side_in,
        out_specs=[pl.BlockSpec((tm, d), lambda i, j: (i, 0)),
                   pl.BlockSpec((tm, nrows), lambda i, j: (i, 0))] + side_out,
        compiler_params=_cparams(("arbitrary", "arbitrary")),
        name="normmod_proj_ff",
    )(x, gain.reshape(1, d), mod, mod, wt, *_side_arrays(side))


def _side_cast_specs(side, n_i, n_j):
    in_specs, out_specs, shapes = [], [], []
    for item in side:
        w, row0, rows = item if isinstance(item, tuple) else (item, 0, item.shape[0])
        cols = w.shape[1]
        assert rows % BF16_SUBLANES == 0 and row0 % BF16_SUBLANES == 0, (row0, rows)
        units = rows // BF16_SUBLANES
        n_slabs = max(dv for dv in range(1, min(units, n_i * n_j) + 1) if units % dv == 0)
        slab = rows // n_slabs

        def slab_index(i, j, n_slabs=n_slabs):
            return jnp.minimum(i * n_j + j, n_slabs - 1)

        in_specs.append(pl.BlockSpec(
            (pl.Element(slab), pl.Element(cols)),
            lambda i, j, f=slab_index, row0=row0, slab=slab:
                (pl.multiple_of(row0 + f(i, j) * slab, BF16_SUBLANES), 0)))
        out_specs.append(pl.BlockSpec((slab, cols), lambda i, j, f=slab_index: (f(i, j), 0)))
        shapes.append(jax.ShapeDtypeStruct((rows, cols), BF16))
    return in_specs, out_specs, shapes


def _side_cast(side_refs):
    n = len(side_refs) // 2
    for src, dst in zip(side_refs[:n], side_refs[n:]):
        dst[...] = src[...].astype(dst.dtype)


def _mm_kernel(x_ref, wt_ref, *refs, with_ada):
    n_ada = 3 if with_ada else 0
    n_side = (len(refs) - n_ada - 1 - (1 if with_ada else 0)) // 2
    ada_in, side_in = refs[:n_ada], refs[n_ada:n_ada + n_side]
    outs = refs[n_ada + n_side:]
    o_ref = outs[0]
    y = lax.dot_general(x_ref[...], wt_ref[...].astype(BF16), (((1,), (1,)), ((), ())),
                        preferred_element_type=F32)
    o_ref[...] = y.astype(o_ref.dtype)
    if with_ada:
        _ada_kernel(*ada_in, outs[1])
    _side_cast(side_in + outs[1 + (1 if with_ada else 0):])


def _side_arrays(side):
    return [item[0] if isinstance(item, tuple) else item for item in side]


def _matmul(x, wt, row0, nrows, out_dtype, name, side=(), tn_pref=1024, ada=None):
    m, k = x.shape
    assert row0 % BF16_SUBLANES == 0
    tm = _tile(m, 1024, SUBLANES)
    tn = _tile(nrows, tn_pref)
    n_i, n_j = m // tm, nrows // tn
    side_in, side_out, side_shapes = _side_cast_specs(side, n_i, n_j)
    ada_in, ada_out, ada_shape, ada_args = [], [], [], []
    if ada is not None:
        ada_in, out_spec, shape = _ada_side_specs(ada, n_i, n_j)
        ada_out, ada_shape, ada_args = [out_spec], [shape], list(ada[:3])
    has_jobs = bool(side) or ada is not None
    out = pl.pallas_call(
        functools.partial(_mm_kernel, with_ada=ada is not None),
        out_shape=[jax.ShapeDtypeStruct((m, nrows), out_dtype)] + ada_shape + side_shapes,
        grid=(n_i, n_j),
        in_specs=[pl.BlockSpec((tm, k), lambda i, j: (i, 0)),
                  pl.BlockSpec((pl.Element(tn), pl.Element(k)),
                               lambda i, j: (pl.multiple_of(row0 + j * tn, BF16_SUBLANES), 0))]
                 + ada_in + side_in,
        out_specs=[pl.BlockSpec((tm, tn), lambda i, j: (i, j))] + ada_out + side_out,
        compiler_params=_cparams(("arbitrary" if has_jobs else "parallel", "arbitrary")),
        name=name,
    )(x, wt, *ada_args, *_side_arrays(side))
    return out if has_jobs else out[0]


def _hgrn_kernel(q_ref, i_ref, g_ref, f_ref, lbt_ref, og_ref, o_ref, st_ref, *, hb, tt):
    @pl.when(pl.program_id(1) == 0)
    def _():
        st_ref[...] = jnp.zeros_like(st_ref)

    tbl = lbt_ref[...]
    e = jnp.exp(tbl - jnp.max(tbl, axis=0, keepdims=True))
    lb = e[0:1, :] / jnp.sum(e, axis=0, keepdims=True)
    half = SUBLANES
    nt_dims = (((1,), (1,)), ((), ()))
    tn_dims = (((0,), (0,)), ((), ()))

    def gates(rows, cs, nrows):
        lbh = lb[:, cs]
        f = lbh + (1.0 - lbh) * _sigmoid(f_ref[rows, cs])
        b = jnp.log2(f)
        row = lax.broadcasted_iota(jnp.int32, (nrows, LANES), 0)
        sh = 1
        while sh < nrows:
            b = b + jnp.where(row >= sh, pltpu.roll(b, sh, axis=0), 0.0)
            sh *= 2
        return 1.0 - f, b

    def finish(o, rows, cs):
        ms = jnp.mean(o * o, axis=-1, keepdims=True)
        on = o * lax.rsqrt(ms + EPS) * og_ref[:, cs]
        gate = g_ref[rows, cs].astype(F32)
        o_ref[rows, cs] = (on * (gate * _sigmoid(gate))).astype(o_ref.dtype)

    mid = HGRN_FSUB // 2
    tri_r = lax.broadcasted_iota(jnp.int32, (HGRN_FSUB, HGRN_FSUB), 0)
    tri_c = lax.broadcasted_iota(jnp.int32, (HGRN_FSUB, HGRN_FSUB), 1)

    def body_factored(n, carry):
        pending = []
        for u, h in [(u, h) for u in range(HGRN_UNROLL) for h in range(hb)]:
            r0 = pl.multiple_of((n * HGRN_UNROLL + u) * HGRN_FSUB, HGRN_FSUB)
            rows = pl.ds(r0, HGRN_FSUB)
            cs = slice(h * LANES, (h + 1) * LANES)
            q = q_ref[rows, cs].astype(F32)
            v = i_ref[rows, cs]
            k, b = gates(rows, cs, HGRN_FSUB)
            r = b[mid - 1:mid, :]
            b_last = b[HGRN_FSUB - 1:HGRN_FSUB, :]
            qc = q * jnp.exp2(b - r)
            kc = (k * jnp.exp2(r - b)).astype(BF16)
            a = lax.dot_general(qc.astype(BF16), kc, nt_dims, preferred_element_type=F32)
            st = st_ref[h]
            o = lax.dot_general((qc * jnp.exp2(r)).astype(BF16), st.astype(BF16), nt_dims,
                                preferred_element_type=F32)
            kt = (k * jnp.exp2(b_last - b)).astype(BF16)
            upd = lax.dot_general(v, kt, tn_dims, preferred_element_type=F32)
            st_ref[h] = st * jnp.exp2(b_last) + upd
            pending.append((a, o, v, rows, cs))
        for a, o, v, rows, cs in pending:
            p = jnp.where(tri_c <= tri_r, a, 0.0).astype(BF16)
            finish(o + jnp.dot(p, v, preferred_element_type=F32), rows, cs)
        return carry

    rowc = lax.broadcasted_iota(jnp.int32, (SUBLANES, 1), 0)

    def body_exact(n, carry):
        r0 = pl.multiple_of(n * HGRN_SUB, HGRN_SUB)
        rows = pl.ds(r0, HGRN_SUB)
        for h in range(hb):
            cs = slice(h * LANES, (h + 1) * LANES)
            q = q_ref[rows, cs].astype(F32)
            v = i_ref[rows, cs]
            vf = v.astype(F32)
            k, b = gates(rows, cs, HGRN_SUB)
            b_last = b[HGRN_SUB - 1:HGRN_SUB, :]
            st = st_ref[h]
            qt = (q * jnp.exp2(b)).astype(BF16)
            o = lax.dot_general(qt, st.astype(BF16), nt_dims, preferred_element_type=F32)
            o_lo, o_hi = o[:half], o[half:]
            q_lo, q_hi = q[:half], q[half:]
            b_lo, b_hi = b[:half], b[half:]
            for s in range(HGRN_SUB):
                ks, bs, vs = k[s:s + 1, :], b[s:s + 1, :], vf[s:s + 1, :]
                c_hi = jnp.sum(q_hi * ks * jnp.exp2(b_hi - bs), axis=-1, keepdims=True)
                if s < half:
                    c_lo = jnp.sum(q_lo * ks * jnp.exp2(b_lo - bs), axis=-1, keepdims=True)
                    o_lo = o_lo + jnp.where(rowc >= s, c_lo, 0.0) * vs
                else:
                    c_hi = jnp.where(rowc >= s - half, c_hi, 0.0)
                o_hi = o_hi + c_hi * vs
            kt = (k * jnp.exp2(b_last - b)).astype(BF16)
            upd = lax.dot_general(v, kt, tn_dims, preferred_element_type=F32)
            st_ref[h] = st * jnp.exp2(b_last) + upd
            finish(jnp.concatenate([o_lo, o_hi], axis=0), rows, cs)
        return carry

    factorable = jnp.min(lb) >= HGRN_MIN_LB

    @pl.when(factorable)
    def _():
        lax.fori_loop(0, tt // (HGRN_FSUB * HGRN_UNROLL), body_factored, 0)

    @pl.when(jnp.logical_not(factorable))
    def _():
        lax.fori_loop(0, tt // HGRN_SUB, body_exact, 0)


def _hgrn(pq, pm, hf, lb_table, onorm_g, width, i_col, g_col):
    s = pq.shape[0]
    hb = min(16, width // HEAD_DIM)
    bw = hb * HEAD_DIM
    tt = _tile(s, 512, HGRN_SUB)
    nl = lb_table.shape[0]
    kern = functools.partial(_hgrn_kernel, hb=hb, tt=tt)
    return pl.pallas_call(
        kern,
        out_shape=jax.ShapeDtypeStruct((s, width), BF16),
        grid=(width // bw, s // tt),
        in_specs=[pl.BlockSpec((tt, bw), lambda h, t: (t, h)),
                  pl.BlockSpec((tt, bw), lambda h, t: (t, i_col // bw + h)),
                  pl.BlockSpec((tt, bw), lambda h, t: (t, g_col // bw + h)),
                  pl.BlockSpec((tt, bw), lambda h, t: (t, h)),
                  pl.BlockSpec((nl, bw), lambda h, t: (0, h)),
                  pl.BlockSpec((1, bw), lambda h, t: (0, h))],
        out_specs=pl.BlockSpec((tt, bw), lambda h, t: (t, h)),
        scratch_shapes=[pltpu.VMEM((hb, HEAD_DIM, HEAD_DIM), F32)],
        compiler_params=_cparams(("parallel", "arbitrary")),
        name="hgrn2",
    )(pq, pm, pm, hf, lb_table, onorm_g.reshape(1, width))


def _split3(x):
    hi = x.astype(BF16).astype(F32)
    r = x - hi
    mid = r.astype(BF16).astype(F32)
    lo = (r - mid).astype(BF16).astype(F32)
    return hi, mid, lo


def _foxprep_kernel(q_ref, k_ref, ff_ref, fb_ref, gq_ref, gk_ref, qo_ref, kto_ref, r_ref, carry_ref, *, nh, scale):
    @pl.when(pl.program_id(0) == 0)
    def _():
        carry_ref[...] = jnp.zeros_like(carry_ref)

    tk = q_ref.shape[0]
    x = ff_ref[...] + fb_ref[...]
    ls = (jnp.minimum(x, 0.0) - jnp.log(1.0 + jnp.exp(-jnp.abs(x)))) * LOG2E
    ri = lax.broadcasted_iota(jnp.int32, (tk, tk), 0)
    ci = lax.broadcasted_iota(jnp.int32, (tk, tk), 1)
    lower = (ci <= ri).astype(F32)
    rel = jnp.dot(lower, ls, precision=lax.Precision.HIGHEST, preferred_element_type=F32)
    base = carry_ref[...]
    r_ref[0] = base
    carry_ref[...] = base + rel[tk - 1:tk, :]

    cq = _split3(rel)
    ck = _split3(rel.T)
    lane = lax.broadcasted_iota(jnp.int32, (tk, LANES), 1)
    sub = lax.broadcasted_iota(jnp.int32, (LANES, tk), 0)
    ones_q = jnp.where((lane >= 3) & (lane < 6), 1.0, 0.0)
    ones_k = jnp.where(sub < 3, 1.0, 0.0)
    for h in range(nh):
        cs = slice(h * LANES, (h + 1) * LANES)
        q = q_ref[:, cs].astype(F32)
        qn = q * lax.rsqrt(jnp.mean(q * q, axis=-1, keepdims=True) + EPS) * gq_ref[:, cs]
        k = k_ref[:, cs].astype(F32)
        kn = k * lax.rsqrt(jnp.mean(k * k, axis=-1, keepdims=True) + EPS) * gk_ref[:, cs]
        eq = ones_q
        ek = ones_k
        for t in range(3):
            eq = jnp.where(lane == t, cq[t][:, h:h + 1], eq)
            ek = jnp.where(sub == 3 + t, -ck[t][h:h + 1, :], ek)
        c0 = h * FOX_KDIM
        qo_ref[:, c0:c0 + LANES] = (qn * scale).astype(qo_ref.dtype)
        qo_ref[:, c0 + LANES:c0 + FOX_KDIM] = eq.astype(qo_ref.dtype)
        kto_ref[0, c0:c0 + LANES, :] = kn.T.astype(kto_ref.dtype)
        kto_ref[0, c0 + LANES:c0 + FOX_KDIM, :] = ek.astype(kto_ref.dtype)


def _foxprep(pm, ff, fbias, gq, gk, width, q_col, k_col, tk):
    s = pm.shape[0]
    nh = width // HEAD_DIM
    nb = s // tk
    kern = functools.partial(_foxprep_kernel, nh=nh, scale=LOG2E * float(HEAD_DIM) ** -0.5)
    qa, kta, r = pl.pallas_call(
        kern,
        out_shape=(jax.ShapeDtypeStruct((s, nh * FOX_KDIM), BF16),
                   jax.ShapeDtypeStruct((nb, nh * FOX_KDIM, tk), BF16),
                   jax.ShapeDtypeStruct((nb, 1, LANES), F32)),
        grid=(nb,),
        in_specs=[pl.BlockSpec((tk, width), lambda i: (i, q_col // width)),
                  pl.BlockSpec((tk, width), lambda i: (i, k_col // width)),
                  pl.BlockSpec((tk, LANES), lambda i: (i, 0)),
                  pl.BlockSpec((1, LANES), lambda i: (0, 0)),
                  pl.BlockSpec((1, width), lambda i: (0, 0)),
                  pl.BlockSpec((1, width), lambda i: (0, 0))],
        out_specs=(pl.BlockSpec((tk, nh * FOX_KDIM), lambda i: (i, 0)),
                   pl.BlockSpec((1, nh * FOX_KDIM, tk), lambda i: (i, 0, 0)),
                   pl.BlockSpec((1, 1, LANES), lambda i: (i, 0, 0))),
        scratch_shapes=[pltpu.VMEM((1, LANES), F32)],
        compiler_params=_cparams(("arbitrary",)),
        name="fox_prep",
    )(pm, pm, ff, fbias, gq.reshape(1, width), gk.reshape(1, width))
    return qa, kta, r.reshape(nb, LANES)


def _fox_kernel(r_ref, q_ref, kt_ref, v_ref, o_ref, v2_ref, s_ref, m_ref, acc_ref, *, tk):
    h = pl.program_id(0)
    qi = pl.program_id(1)

    @pl.when(qi == 0)
    def _():
        v2_ref[:, :LANES] = v_ref[...]
        lane = lax.broadcasted_iota(jnp.int32, v_ref.shape, 1)
        v2_ref[:, LANES:] = jnp.where(lane == 0, 1.0, 0.0).astype(v2_ref.dtype)

    m_ref[...] = jnp.full(m_ref.shape, -jnp.inf, F32)
    acc_ref[...] = jnp.zeros(acc_ref.shape, F32)
    rr = lax.broadcasted_iota(jnp.int32, (tk, tk), 0)
    cc = lax.broadcasted_iota(jnp.int32, (tk, tk), 1)

    def scores(c, j, slot):
        s_ref[slot, c] = jnp.dot(q_ref[c * tk:(c + 1) * tk, :], kt_ref[j], preferred_element_type=F32)

    def fold(c, j, slot, diag):
        s = s_ref[slot, c]
        if diag:
            s = jnp.where(cc <= rr, s, -jnp.inf)
        d = r_ref[2 * qi + c, h] - r_ref[j, h]
        m = m_ref[c]
        m_new = jnp.maximum(m, jnp.max(s, axis=-1, keepdims=True) + d)
        p = jnp.exp2(s - (m_new - d))
        r0 = pl.multiple_of(j * tk, tk)
        pv = jnp.dot(p.astype(BF16), v2_ref[pl.ds(r0, tk), :], preferred_element_type=F32)
        acc_ref[c] = jnp.exp2(m - m_new) * acc_ref[c] + pv
        m_ref[c] = m_new

    scores(0, 0, 0)
    scores(1, 0, 0)

    def pair(j):
        for slot in range(2):
            scores(0, j + slot + 1, 1 - slot)
            scores(1, j + slot + 1, 1 - slot)
            fold(0, j + slot, slot, False)
            fold(1, j + slot, slot, False)

    def body(jp, carry):
        pair(2 * jp)
        return carry

    lax.fori_loop(0, qi, body, 0)
    scores(1, 2 * qi + 1, 1)
    fold(0, 2 * qi, 0, True)
    fold(1, 2 * qi, 0, False)
    fold(1, 2 * qi + 1, 1, True)
    for c in range(2):
        acc = acc_ref[c]
        o_ref[c * tk:(c + 1) * tk, :] = (acc[:, :LANES] / acc[:, LANES:LANES + 1]).astype(o_ref.dtype)


def _fox(qa, kta, r, pm, v_col, tk):
    s = qa.shape[0]
    nh = qa.shape[1] // FOX_KDIM
    nb = s // tk
    tq = 2 * tk
    kern = functools.partial(_fox_kernel, tk=tk)
    return pl.pallas_call(
        kern,
        out_shape=jax.ShapeDtypeStruct((s, nh * HEAD_DIM), BF16),
        grid=(nh, s // tq),
        in_specs=[pl.BlockSpec(memory_space=pltpu.SMEM),
                  pl.BlockSpec((tq, FOX_KDIM), lambda h, i: (i, h)),
                  pl.BlockSpec((nb, FOX_KDIM, tk), lambda h, i: (0, h, 0)),
                  pl.BlockSpec((s, HEAD_DIM), lambda h, i: (0, v_col // HEAD_DIM + h))],
        out_specs=pl.BlockSpec((tq, HEAD_DIM), lambda h, i: (i, h)),
        scratch_shapes=[pltpu.VMEM((s, FOX_KDIM), BF16),
                        pltpu.VMEM((2, 2, tk, tk), F32),
                        pltpu.VMEM((2, tk, 1), F32),
                        pltpu.VMEM((2, tk, FOX_KDIM), F32)],
        compiler_params=_cparams(("parallel", "arbitrary")),
        name="fox_attn",
    )(r, qa, kta, pm)


def _merge_kernel(oa_ref, ob_ref, wa_ref, wb_ref, ga_ref, gb_ref, y_ref):
    sa = _sigmoid(ga_ref[...].astype(F32))
    sb = _sigmoid(gb_ref[...].astype(F32))
    ya = jnp.dot(oa_ref[...], wa_ref[...], preferred_element_type=F32)
    yb = jnp.dot(ob_ref[...], wb_ref[...], preferred_element_type=F32)
    y_ref[...] = (sa * ya + sb * yb).astype(y_ref.dtype)


def _merge(oa, ob, wa, wb, pg):
    m, ka = oa.shape
    kb = ob.shape[1]
    n = wa.shape[1]
    tm = _tile(m, 1024, SUBLANES)
    tn = _tile(n, 1024)
    return pl.pallas_call(
        _merge_kernel,
        out_shape=jax.ShapeDtypeStruct((m, n), BF16),
        grid=(m // tm, n // tn),
        in_specs=[pl.BlockSpec((tm, ka), lambda i, j: (i, 0)),
                  pl.BlockSpec((tm, kb), lambda i, j: (i, 0)),
                  pl.BlockSpec((ka, tn), lambda i, j: (0, j)),
                  pl.BlockSpec((kb, tn), lambda i, j: (0, j)),
                  pl.BlockSpec((tm, tn), lambda i, j: (i, j)),
                  pl.BlockSpec((tm, tn), lambda i, j: (i, n // tn + j))],
        out_specs=pl.BlockSpec((tm, tn), lambda i, j: (i, j)),
        compiler_params=_cparams(("parallel", "arbitrary")),
        name="branch_merge",
    )(oa, ob, wa, wb, pg, pg)


def _resproj_kernel(a_ref, w_ref, x_ref, gt_ref, *refs):
    n_side = (len(refs) - 1) // 2
    o_ref = refs[n_side]
    y = jnp.dot(a_ref[...], w_ref[...], preferred_element_type=F32)
    o_ref[...] = x_ref[...] + gt_ref[...] * y
    _side_cast(refs[:n_side] + refs[n_side + 1:])


def _resproj(a, w, x, mod, gate_idx, tm_pref, tn_pref, name, side=()):
    m, k = a.shape
    n = w.shape[1]
    tm = _tile(m, tm_pref, SUBLANES)
    tn = _tile(n, tn_pref)
    side_in, side_out, side_shapes = _side_cast_specs(side, m // tm, n // tn)
    out = pl.pallas_call(
        _resproj_kernel,
        out_shape=[jax.ShapeDtypeStruct((m, n), F32)] + side_shapes,
        grid=(m // tm, n // tn),
        in_specs=[pl.BlockSpec((tm, k), lambda i, j: (i, 0)),
                  pl.BlockSpec((k, tn), lambda i, j: (0, j)),
                  pl.BlockSpec((tm, tn), lambda i, j: (i, j)),
                  pl.BlockSpec((1, tn), lambda i, j: (0, gate_idx * (n // tn) + j))] + side_in,
        out_specs=[pl.BlockSpec((tm, tn), lambda i, j: (i, j))] + side_out,
        compiler_params=_cparams(("arbitrary" if side else "parallel", "arbitrary")),
        name=name,
    )(a, w, x, mod, *_side_arrays(side))
    return out if side else out[0]


def _swiglu_kernel(h_ref, wg_ref, wu_ref, *refs):
    n_side = (len(refs) - 1) // 2
    o_ref = refs[n_side]
    h = h_ref[...]
    g = jnp.dot(h, wg_ref[...], preferred_element_type=F32)
    u = jnp.dot(h, wu_ref[...], preferred_element_type=F32)
    o_ref[...] = (g * _sigmoid(g) * u).astype(o_ref.dtype)
    _side_cast(refs[:n_side] + refs[n_side + 1:])


def _swiglu(h, w, side=()):
    m, k = h.shape
    dff = w.shape[1] // 2
    tm = _tile(m, 4096, SUBLANES)
    tn = _tile(dff, 512)
    nj = dff // tn
    side_in, side_out, side_shapes = _side_cast_specs(side, m // tm, nj)
    out = pl.pallas_call(
        _swiglu_kernel,
        out_shape=[jax.ShapeDtypeStruct((m, dff), BF16)] + side_shapes,
        grid=(m // tm, nj),
        in_specs=[pl.BlockSpec((tm, k), lambda i, j: (i, 0), pipeline_mode=pl.Buffered(1)),
                  pl.BlockSpec((k, tn), lambda i, j: (0, j)),
                  pl.BlockSpec((k, tn), lambda i, j: (0, nj + j))] + side_in,
        out_specs=[pl.BlockSpec((tm, tn), lambda i, j: (i, j))] + side_out,
        compiler_params=_cparams(("arbitrary" if side else "parallel", "arbitrary")),
        name="ffn_swiglu",
    )(h, w, w, *_side_arrays(side))
    return out if side else out[0]


def kernel(x, c, w_ada, b_ada, norm_mix_g, norm_ffn_g, w_in, fox_f_bias, hgrn_lb_table, hgrn_onorm_g,
           fox_q_norm_g, fox_k_norm_g, w_branch_a, w_branch_b, w_out, w_ffn_in, w_ffn_out):
    batch, seq, d = x.shape
    assert batch == 1 and w_ada.shape[0] == 1, "single sequence, single layer"
    hw = hgrn_onorm_g.shape[1] * hgrn_onorm_g.shape[2]
    fw = fox_q_norm_g.shape[1] * fox_q_norm_g.shape[2]
    nfh = fox_q_norm_g.shape[1]
    assert hgrn_onorm_g.shape[2] == HEAD_DIM and fox_q_norm_g.shape[2] == HEAD_DIM
    assert nfh <= LANES and hw == fw and seq % (2 * FOX_TK) == 0
    x2 = x.reshape(seq, d)

    f_col = 4 * hw + 3 * fw
    wt = jnp.swapaxes(w_in, 1, 2)[0]
    fbias = jnp.pad(fox_f_bias[0], (0, LANES - nfh)).reshape(1, LANES)

    c_col, w_ada2, b_ada2 = c.reshape(d, 1), w_ada[0], b_ada.reshape(1, b_ada.shape[-1])
    mod_a = _ada_mod(c_col, w_ada2, b_ada2, 2 * d)

    n_main = 2 * hw + 3 * fw
    h, ff, w_qf = _normmod_proj(x2, norm_mix_g[0], mod_a, 0, 1, wt, f_col, LANES,
                                side=((wt, 0, 2 * hw),))
    pq, w_main = _matmul(h, w_qf, 0, hw, BF16, "in_proj_q", tn_pref=512,
                         side=((wt, 2 * hw, n_main),))
    hf, w_gates = _matmul(h, w_qf, hw, hw, F32, "in_proj_f", tn_pref=512,
                          side=((wt, f_col, nfh + 2 * d),))
    pm, mod_b, w_out_bf, w_a_bf, w_b_bf = _matmul(
        h, w_main, 0, n_main, BF16, "in_proj_main", ada=(c_col, w_ada2, b_ada2, 2 * d, 4 * d),
        side=(w_out[0], w_branch_a[0], w_branch_b[0]))
    pg, w_ffn_in_bf = _matmul(h, w_gates, nfh, 2 * d, BF16, "in_proj_gates",
                              side=(w_ffn_in[0],))

    o_a = _hgrn(pq, pm, hf, hgrn_lb_table, hgrn_onorm_g[0], hw, 0, hw)

    qa, kta, r = _foxprep(pm, ff, fbias, fox_q_norm_g[0], fox_k_norm_g[0], fw, 2 * hw, 2 * hw + fw, FOX_TK)
    o_b = _fox(qa, kta, r, pm, 2 * hw + 2 * fw, FOX_TK)

    y = _merge(o_a, o_b, w_a_bf, w_b_bf, pg)
    x1, w_ffn_out_bf = _resproj(y, w_out_bf, x2, mod_b, 0, 1024, 512, "out_proj",
                                side=(w_ffn_out[0],))

    h2 = _normmod(x1, norm_ffn_g[0], mod_b, 1, 2)
    act = _swiglu(h2, w_ffn_in_bf)
    x3 = _resproj(act, w_ffn_out_bf, x1, mod_b, 3, 512, 512, "ffn_out")
    return x3.reshape(batch, seq, d)
```

```python
import functools
import math

import jax
import jax.numpy as jnp
from jax import lax
from jax.experimental import pallas as pl
from jax.experimental.pallas import tpu as pltpu

F32 = jnp.float32
BF16 = jnp.bfloat16

EPS = 1e-6
LOG2E = 1.4426950408889634
LANES = 128
SUBLANES = 8
BF16_SUBLANES = 16
HEAD_DIM = 128
HGRN_SUB = 16
HGRN_FSUB = 32
HGRN_UNROLL = 4
HGRN_MIN_LB = 2.0 ** -6
FOX_TK = 512
FOX_KDIM = 2 * HEAD_DIM
V7X_VMEM_LIMIT_BYTES = 56 * 1024 * 1024


def _cparams(semantics, vmem_bytes=V7X_VMEM_LIMIT_BYTES):
    return pltpu.CompilerParams(dimension_semantics=semantics, vmem_limit_bytes=vmem_bytes)


def _tile(dim, pref, align=LANES):
    if dim <= pref:
        return dim
    t = (pref // align) * align
    while t >= align:
        if dim % t == 0:
            return t
        t -= align
    raise ValueError(f"no {align}-aligned tile for {dim}")


def _sigmoid(x):
    return 1.0 / (1.0 + jnp.exp(-x))


def _ada_kernel(c_ref, w_ref, b_ref, o_ref):
    c = c_ref[...]
    s = c * _sigmoid(c)
    o_ref[...] = jnp.sum(w_ref[...] * s, axis=0, keepdims=True) + b_ref[...]


def _ada_mod(c_col, w_ada, b_ada, ncols):
    d = w_ada.shape[0]
    tn = _tile(ncols, 512)
    return pl.pallas_call(
        _ada_kernel,
        out_shape=jax.ShapeDtypeStruct((1, ncols), F32),
        grid=(ncols // tn,),
        in_specs=[pl.BlockSpec((d, 1), lambda j: (0, 0)),
                  pl.BlockSpec((d, tn), lambda j: (0, j)),
                  pl.BlockSpec((1, tn), lambda j: (0, j))],
        out_specs=pl.BlockSpec((1, tn), lambda j: (0, j)),
        compiler_params=_cparams(("parallel",)),
        name="ada_mod",
    )(c_col, w_ada, b_ada)


def _ada_side_specs(ada, n_i, n_j):
    c_col, w_ada, b_ada, col0, ncols = ada
    d = w_ada.shape[0]
    units = ncols // LANES
    n_blk = max(dv for dv in range(1, min(units, n_i * n_j) + 1) if units % dv == 0)
    cb = ncols // n_blk
    assert col0 % cb == 0
    blk = lambda i, j: jnp.minimum(i * n_j + j, n_blk - 1)
    in_specs = [pl.BlockSpec((d, 1), lambda i, j: (0, 0)),
                pl.BlockSpec((d, cb), lambda i, j: (0, col0 // cb + blk(i, j))),
                pl.BlockSpec((1, cb), lambda i, j: (0, col0 // cb + blk(i, j)))]
    out_spec = pl.BlockSpec((1, cb), lambda i, j: (0, blk(i, j)))
    return in_specs, out_spec, jax.ShapeDtypeStruct((1, ncols), F32)


def _normmod_kernel(x_ref, g_ref, sh_ref, sc_ref, o_ref):
    x = x_ref[...]
    ms = jnp.mean(x * x, axis=-1, keepdims=True)
    xn = x * lax.rsqrt(ms + EPS)
    o_ref[...] = (xn * g_ref[...] * (1.0 + sc_ref[...]) + sh_ref[...]).astype(o_ref.dtype)


def _normmod(x, gain, mod, shift_idx, scale_idx):
    s, d = x.shape
    tm = _tile(s, 512, SUBLANES)
    return pl.pallas_call(
        _normmod_kernel,
        out_shape=jax.ShapeDtypeStruct((s, d), BF16),
        grid=(s // tm,),
        in_specs=[pl.BlockSpec((tm, d), lambda i: (i, 0)),
                  pl.BlockSpec((1, d), lambda i: (0, 0)),
                  pl.BlockSpec((1, d), lambda i: (0, shift_idx)),
                  pl.BlockSpec((1, d), lambda i: (0, scale_idx))],
        out_specs=pl.BlockSpec((tm, d), lambda i: (i, 0)),
        compiler_params=_cparams(("parallel",)),
        name="normmod",
    )(x, gain.reshape(1, d), mod, mod)


def _normmod_proj_kernel(x_ref, g_ref, sh_ref, sc_ref, wt_ref, *refs):
    n_side = (len(refs) - 2) // 2
    h_ref, o_ref = refs[n_side], refs[n_side + 1]
    _normmod_kernel(x_ref, g_ref, sh_ref, sc_ref, h_ref)
    o_ref[...] = lax.dot_general(h_ref[...], wt_ref[...].astype(BF16), (((1,), (1,)), ((), ())),
                                 preferred_element_type=F32)
    _side_cast(refs[:n_side] + refs[n_side + 2:])


def _normmod_proj(x, gain, mod, shift_idx, scale_idx, wt, row0, nrows, side):
    s, d = x.shape
    tm = _tile(s, 512, SUBLANES)
    n_i = s // tm
    side_in, side_out, side_shapes = _side_cast_specs(side, n_i, 1)
    return pl.pallas_call(
        _normmod_proj_kernel,
        out_shape=[jax.ShapeDtypeStruct((s, d), BF16), jax.ShapeDtypeStruct((s, nrows), F32)] + side_shapes,
        grid=(n_i, 1),
        in_specs=[pl.BlockSpec((tm, d), lambda i, j: (i, 0)),
                  pl.BlockSpec((1, d), lambda i, j: (0, 0)),
                  pl.BlockSpec((1, d), lambda i, j: (0, shift_idx)),
                  pl.BlockSpec((1, d), lambda i, j: (0, scale_idx)),
                  pl.BlockSpec((pl.Element(nrows), pl.Element(d)), lambda i, j: (row0, 0))] + side_in,
        out_specs=[pl.BlockSpec((tm, d), lambda i, j: (i, 0)),
                   pl.BlockSpec((tm, nrows), lambda i, j: (i, 0))] + side_out,
        compiler_params=_cparams(("arbitrary", "arbitrary")),
        name="normmod_proj_ff",
    )(x, gain.reshape(1, d), mod, mod, wt, *_side_arrays(side))


def _side_cast_specs(side, n_i, n_j):
    in_specs, out_specs, shapes = [], [], []
    for item in side:
        w, row0, rows = item if isinstance(item, tuple) else (item, 0, item.shape[0])
        cols = w.shape[1]
        assert rows % BF16_SUBLANES == 0 and row0 % BF16_SUBLANES == 0, (row0, rows)
        units = rows // BF16_SUBLANES
        n_slabs = max(dv for dv in range(1, min(units, n_i * n_j) + 1) if units % dv == 0)
        slab = rows // n_slabs

        def slab_index(i, j, n_slabs=n_slabs):
            return jnp.minimum(i * n_j + j, n_slabs - 1)

        in_specs.append(pl.BlockSpec(
            (pl.Element(slab), pl.Element(cols)),
            lambda i, j, f=slab_index, row0=row0, slab=slab:
                (pl.multiple_of(row0 + f(i, j) * slab, BF16_SUBLANES), 0)))
        out_specs.append(pl.BlockSpec((slab, cols), lambda i, j, f=slab_index: (f(i, j), 0)))
        shapes.append(jax.ShapeDtypeStruct((rows, cols), BF16))
    return in_specs, out_specs, shapes


def _side_cast(side_refs):
    n = len(side_refs) // 2
    for src, dst in zip(side_refs[:n], side_refs[n:]):
        dst[...] = src[...].astype(dst.dtype)


def _mm_kernel(x_ref, wt_ref, *refs, with_ada):
    n_ada = 3 if with_ada else 0
    n_side = (len(refs) - n_ada - 1 - (1 if with_ada else 0)) // 2
    ada_in, side_in = refs[:n_ada], refs[n_ada:n_ada + n_side]
    outs = refs[n_ada + n_side:]
    o_ref = outs[0]
    y = lax.dot_general(x_ref[...], wt_ref[...].astype(BF16), (((1,), (1,)), ((), ())),
                        preferred_element_type=F32)
    o_ref[...] = y.astype(o_ref.dtype)
    if with_ada:
        _ada_kernel(*ada_in, outs[1])
    _side_cast(side_in + outs[1 + (1 if with_ada else 0):])


def _side_arrays(side):
    return [item[0] if isinstance(item, tuple) else item for item in side]


def _matmul(x, wt, row0, nrows, out_dtype, name, side=(), tn_pref=1024, ada=None):
    m, k = x.shape
    assert row0 % BF16_SUBLANES == 0
    tm = _tile(m, 1024, SUBLANES)
    tn = _tile(nrows, tn_pref)
    n_i, n_j = m // tm, nrows // tn
    side_in, side_out, side_shapes = _side_cast_specs(side, n_i, n_j)
    ada_in, ada_out, ada_shape, ada_args = [], [], [], []
    if ada is not None:
        ada_in, out_spec, shape = _ada_side_specs(ada, n_i, n_j)
        ada_out, ada_shape, ada_args = [out_spec], [shape], list(ada[:3])
    has_jobs = bool(side) or ada is not None
    out = pl.pallas_call(
        functools.partial(_mm_kernel, with_ada=ada is not None),
        out_shape=[jax.ShapeDtypeStruct((m, nrows), out_dtype)] + ada_shape + side_shapes,
        grid=(n_i, n_j),
        in_specs=[pl.BlockSpec((tm, k), lambda i, j: (i, 0)),
                  pl.BlockSpec((pl.Element(tn), pl.Element(k)),
                               lambda i, j: (pl.multiple_of(row0 + j * tn, BF16_SUBLANES), 0))]
                 + ada_in + side_in,
        out_specs=[pl.BlockSpec((tm, tn), lambda i, j: (i, j))] + ada_out + side_out,
        compiler_params=_cparams(("arbitrary" if has_jobs else "parallel", "arbitrary")),
        name=name,
    )(x, wt, *ada_args, *_side_arrays(side))
    return out if has_jobs else out[0]


def _hgrn_kernel(q_ref, i_ref, g_ref, f_ref, lbt_ref, og_ref, o_ref, st_ref, *, hb, tt):
    @pl.when(pl.program_id(1) == 0)
    def _():
        st_ref[...] = jnp.zeros_like(st_ref)

    tbl = lbt_ref[...]
    e = jnp.exp(tbl - jnp.max(tbl, axis=0, keepdims=True))
    lb = e[0:1, :] / jnp.sum(e, axis=0, keepdims=True)
    half = SUBLANES
    nt_dims = (((1,), (1,)), ((), ()))
    tn_dims = (((0,), (0,)), ((), ()))

    def gates(rows, cs, nrows):
        lbh = lb[:, cs]
        f = lbh + (1.0 - lbh) * _sigmoid(f_ref[rows, cs])
        b = jnp.log2(f)
        row = lax.broadcasted_iota(jnp.int32, (nrows, LANES), 0)
        sh = 1
        while sh < nrows:
            b = b + jnp.where(row >= sh, pltpu.roll(b, sh, axis=0), 0.0)
            sh *= 2
        return 1.0 - f, b

    def finish(o, rows, cs):
        ms = jnp.mean(o * o, axis=-1, keepdims=True)
        on = o * lax.rsqrt(ms + EPS) * og_ref[:, cs]
        gate = g_ref[rows, cs].astype(F32)
        o_ref[rows, cs] = (on * (gate * _sigmoid(gate))).astype(o_ref.dtype)

    mid = HGRN_FSUB // 2
    tri_r = lax.broadcasted_iota(jnp.int32, (HGRN_FSUB, HGRN_FSUB), 0)
    tri_c = lax.broadcasted_iota(jnp.int32, (HGRN_FSUB, HGRN_FSUB), 1)

    def body_factored(n, carry):
        pending = []
        for u, h in [(u, h) for u in range(HGRN_UNROLL) for h in range(hb)]:
            r0 = pl.multiple_of((n * HGRN_UNROLL + u) * HGRN_FSUB, HGRN_FSUB)
            rows = pl.ds(r0, HGRN_FSUB)
            cs = slice(h * LANES, (h + 1) * LANES)
            q = q_ref[rows, cs].astype(F32)
            v = i_ref[rows, cs]
            k, b = gates(rows, cs, HGRN_FSUB)
            r = b[mid - 1:mid, :]
            b_last = b[HGRN_FSUB - 1:HGRN_FSUB, :]
            qc = q * jnp.exp2(b - r)
            kc = (k * jnp.exp2(r - b)).astype(BF16)
            a = lax.dot_general(qc.astype(BF16), kc, nt_dims, preferred_element_type=F32)
            st = st_ref[h]
            o = lax.dot_general((qc * jnp.exp2(r)).astype(BF16), st.astype(BF16), nt_dims,
                                preferred_element_type=F32)
            kt = (k * jnp.exp2(b_last - b)).astype(BF16)
            upd = lax.dot_general(v, kt, tn_dims, preferred_element_type=F32)
            st_ref[h] = st * jnp.exp2(b_last) + upd
            pending.append((a, o, v, rows, cs))
        for a, o, v, rows, cs in pending:
            p = jnp.where(tri_c <= tri_r, a, 0.0).astype(BF16)
            finish(o + jnp.dot(p, v, preferred_element_type=F32), rows, cs)
        return carry

    rowc = lax.broadcasted_iota(jnp.int32, (SUBLANES, 1), 0)

    def body_exact(n, carry):
        r0 = pl.multiple_of(n * HGRN_SUB, HGRN_SUB)
        rows = pl.ds(r0, HGRN_SUB)
        for h in range(hb):
            cs = slice(h * LANES, (h + 1) * LANES)
            q = q_ref[rows, cs].astype(F32)
            v = i_ref[rows, cs]
            vf = v.astype(F32)
            k, b = gates(rows, cs, HGRN_SUB)
            b_last = b[HGRN_SUB - 1:HGRN_SUB, :]
            st = st_ref[h]
            qt = (q * jnp.exp2(b)).astype(BF16)
            o = lax.dot_general(qt, st.astype(BF16), nt_dims, preferred_element_type=F32)
            o_lo, o_hi = o[:half], o[half:]
            q_lo, q_hi = q[:half], q[half:]
            b_lo, b_hi = b[:half], b[half:]
            for s in range(HGRN_SUB):
                ks, bs, vs = k[s:s + 1, :], b[s:s + 1, :], vf[s:s + 1, :]
                c_hi = jnp.sum(q_hi * ks * jnp.exp2(b_hi - bs), axis=-1, keepdims=True)
                if s < half:
                    c_lo = jnp.sum(q_lo * ks * jnp.exp2(b_lo - bs), axis=-1, keepdims=True)
                    o_lo = o_lo + jnp.where(rowc >= s, c_lo, 0.0) * vs
                else:
                    c_hi = jnp.where(rowc >= s - half, c_hi, 0.0)
                o_hi = o_hi + c_hi * vs
            kt = (k * jnp.exp2(b_last - b)).astype(BF16)
            upd = lax.dot_general(v, kt, tn_dims, preferred_element_type=F32)
            st_ref[h] = st * jnp.exp2(b_last) + upd
            finish(jnp.concatenate([o_lo, o_hi], axis=0), rows, cs)
        return carry

    factorable = jnp.min(lb) >= HGRN_MIN_LB

    @pl.when(factorable)
    def _():
        lax.fori_loop(0, tt // (HGRN_FSUB * HGRN_UNROLL), body_factored, 0)

    @pl.when(jnp.logical_not(factorable))
    def _():
        lax.fori_loop(0, tt // HGRN_SUB, body_exact, 0)


def _hgrn(pq, pm, hf, lb_table, onorm_g, width, i_col, g_col):
    s = pq.shape[0]
    hb = min(16, width // HEAD_DIM)
    bw = hb * HEAD_DIM
    tt = _tile(s, 512, HGRN_SUB)
    nl = lb_table.shape[0]
    kern = functools.partial(_hgrn_kernel, hb=hb, tt=tt)
    return pl.pallas_call(
        kern,
        out_shape=jax.ShapeDtypeStruct((s, width), BF16),
        grid=(width // bw, s // tt),
        in_specs=[pl.BlockSpec((tt, bw), lambda h, t: (t, h)),
                  pl.BlockSpec((tt, bw), lambda h, t: (t, i_col // bw + h)),
                  pl.BlockSpec((tt, bw), lambda h, t: (t, g_col // bw + h)),
                  pl.BlockSpec((tt, bw), lambda h, t: (t, h)),
                  pl.BlockSpec((nl, bw), lambda h, t: (0, h)),
                  pl.BlockSpec((1, bw), lambda h, t: (0, h))],
        out_specs=pl.BlockSpec((tt, bw), lambda h, t: (t, h)),
        scratch_shapes=[pltpu.VMEM((hb, HEAD_DIM, HEAD_DIM), F32)],
        compiler_params=_cparams(("parallel", "arbitrary")),
        name="hgrn2",
    )(pq, pm, pm, hf, lb_table, onorm_g.reshape(1, width))


def _split3(x):
    hi = x.astype(BF16).astype(F32)
    r = x - hi
    mid = r.astype(BF16).astype(F32)
    lo = (r - mid).astype(BF16).astype(F32)
    return hi, mid, lo


def _foxprep_kernel(q_ref, k_ref, ff_ref, fb_ref, gq_ref, gk_ref, qo_ref, kto_ref, r_ref, carry_ref, *, nh, scale):
    @pl.when(pl.program_id(0) == 0)
    def _():
        carry_ref[...] = jnp.zeros_like(carry_ref)

    tk = q_ref.shape[0]
    x = ff_ref[...] + fb_ref[...]
    ls = (jnp.minimum(x, 0.0) - jnp.log(1.0 + jnp.exp(-jnp.abs(x)))) * LOG2E
    ri = lax.broadcasted_iota(jnp.int32, (tk, tk), 0)
    ci = lax.broadcasted_iota(jnp.int32, (tk, tk), 1)
    lower = (ci <= ri).astype(F32)
    rel = jnp.dot(lower, ls, precision=lax.Precision.HIGHEST, preferred_element_type=F32)
    base = carry_ref[...]
    r_ref[0] = base
    carry_ref[...] = base + rel[tk - 1:tk, :]

    cq = _split3(rel)
    ck = _split3(rel.T)
    lane = lax.broadcasted_iota(jnp.int32, (tk, LANES), 1)
    sub = lax.broadcasted_iota(jnp.int32, (LANES, tk), 0)
    ones_q = jnp.where((lane >= 3) & (lane < 6), 1.0, 0.0)
    ones_k = jnp.where(sub < 3, 1.0, 0.0)
    for h in range(nh):
        cs = slice(h * LANES, (h + 1) * LANES)
        q = q_ref[:, cs].astype(F32)
        qn = q * lax.rsqrt(jnp.mean(q * q, axis=-1, keepdims=True) + EPS) * gq_ref[:, cs]
        k = k_ref[:, cs].astype(F32)
        kn = k * lax.rsqrt(jnp.mean(k * k, axis=-1, keepdims=True) + EPS) * gk_ref[:, cs]
        eq = ones_q
        ek = ones_k
        for t in range(3):
            eq = jnp.where(lane == t, cq[t][:, h:h + 1], eq)
            ek = jnp.where(sub == 3 + t, -ck[t][h:h + 1, :], ek)
        c0 = h * FOX_KDIM
        qo_ref[:, c0:c0 + LANES] = (qn * scale).astype(qo_ref.dtype)
        qo_ref[:, c0 + LANES:c0 + FOX_KDIM] = eq.astype(qo_ref.dtype)
        kto_ref[0, c0:c0 + LANES, :] = kn.T.astype(kto_ref.dtype)
        kto_ref[0, c0 + LANES:c0 + FOX_KDIM, :] = ek.astype(kto_ref.dtype)


def _foxprep(pm, ff, fbias, gq, gk, width, q_col, k_col, tk):
    s = pm.shape[0]
    nh = width // HEAD_DIM
    nb = s // tk
    kern = functools.partial(_foxprep_kernel, nh=nh, scale=LOG2E * float(HEAD_DIM) ** -0.5)
    qa, kta, r = pl.pallas_call(
        kern,
        out_shape=(jax.ShapeDtypeStruct((s, nh * FOX_KDIM), BF16),
                   jax.ShapeDtypeStruct((nb, nh * FOX_KDIM, tk), BF16),
                   jax.ShapeDtypeStruct((nb, 1, LANES), F32)),
        grid=(nb,),
        in_specs=[pl.BlockSpec((tk, width), lambda i: (i, q_col // width)),
                  pl.BlockSpec((tk, width), lambda i: (i, k_col // width)),
                  pl.BlockSpec((tk, LANES), lambda i: (i, 0)),
                  pl.BlockSpec((1, LANES), lambda i: (0, 0)),
                  pl.BlockSpec((1, width), lambda i: (0, 0)),
                  pl.BlockSpec((1, width), lambda i: (0, 0))],
        out_specs=(pl.BlockSpec((tk, nh * FOX_KDIM), lambda i: (i, 0)),
                   pl.BlockSpec((1, nh * FOX_KDIM, tk), lambda i: (i, 0, 0)),
                   pl.BlockSpec((1, 1, LANES), lambda i: (i, 0, 0))),
        scratch_shapes=[pltpu.VMEM((1, LANES), F32)],
        compiler_params=_cparams(("arbitrary",)),
        name="fox_prep",
    )(pm, pm, ff, fbias, gq.reshape(1, width), gk.reshape(1, width))
    return qa, kta, r.reshape(nb, LANES)


def _fox_kernel(r_ref, q_ref, kt_ref, v_ref, o_ref, v2_ref, s_ref, m_ref, acc_ref, *, tk):
    h = pl.program_id(0)
    qi = pl.program_id(1)

    @pl.when(qi == 0)
    def _():
        v2_ref[:, :LANES] = v_ref[...]
        lane = lax.broadcasted_iota(jnp.int32, v_ref.shape, 1)
        v2_ref[:, LANES:] = jnp.where(lane == 0, 1.0, 0.0).astype(v2_ref.dtype)

    m_ref[...] = jnp.full(m_ref.shape, -jnp.inf, F32)
    acc_ref[...] = jnp.zeros(acc_ref.shape, F32)
    rr = lax.broadcasted_iota(jnp.int32, (tk, tk), 0)
    cc = lax.broadcasted_iota(jnp.int32, (tk, tk), 1)

    def scores(c, j, slot):
        s_ref[slot, c] = jnp.dot(q_ref[c * tk:(c + 1) * tk, :], kt_ref[j], preferred_element_type=F32)

    def fold(c, j, slot, diag):
        s = s_ref[slot, c]
        if diag:
            s = jnp.where(cc <= rr, s, -jnp.inf)
        d = r_ref[2 * qi + c, h] - r_ref[j, h]
        m = m_ref[c]
        m_new = jnp.maximum(m, jnp.max(s, axis=-1, keepdims=True) + d)
        p = jnp.exp2((s - (m_new - d)).astype(BF16))
        r0 = pl.multiple_of(j * tk, tk)
        pv = jnp.dot(p, v2_ref[pl.ds(r0, tk), :], preferred_element_type=F32)
        acc_ref[c] = jnp.exp2(m - m_new) * acc_ref[c] + pv
        m_ref[c] = m_new

    scores(0, 0, 0)
    scores(1, 0, 0)

    def pair(j):
        for slot in range(2):
            scores(0, j + slot + 1, 1 - slot)
            scores(1, j + slot + 1, 1 - slot)
            fold(0, j + slot, slot, False)
            fold(1, j + slot, slot, False)

    def body(jp, carry):
        pair(2 * jp)
        return carry

    lax.fori_loop(0, qi, body, 0)
    scores(1, 2 * qi + 1, 1)
    fold(0, 2 * qi, 0, True)
    fold(1, 2 * qi, 0, False)
    fold(1, 2 * qi + 1, 1, True)
    for c in range(2):
        acc = acc_ref[c]
        o_ref[c * tk:(c + 1) * tk, :] = (acc[:, :LANES] / acc[:, LANES:LANES + 1]).astype(o_ref.dtype)


def _fox(qa, kta, r, pm, v_col, tk):
    s = qa.shape[0]
    nh = qa.shape[1] // FOX_KDIM
    nb = s // tk
    tq = 2 * tk
    kern = functools.partial(_fox_kernel, tk=tk)
    return pl.pallas_call(
        kern,
        out_shape=jax.ShapeDtypeStruct((s, nh * HEAD_DIM), BF16),
        grid=(nh, s // tq),
        in_specs=[pl.BlockSpec(memory_space=pltpu.SMEM),
                  pl.BlockSpec((tq, FOX_KDIM), lambda h, i: (i, h)),
                  pl.BlockSpec((nb, FOX_KDIM, tk), lambda h, i: (0, h, 0)),
                  pl.BlockSpec((s, HEAD_DIM), lambda h, i: (0, v_col // HEAD_DIM + h))],
        out_specs=pl.BlockSpec((tq, HEAD_DIM), lambda h, i: (i, h)),
        scratch_shapes=[pltpu.VMEM((s, FOX_KDIM), BF16),
                        pltpu.VMEM((2, 2, tk, tk), F32),
                        pltpu.VMEM((2, tk, 1), F32),
                        pltpu.VMEM((2, tk, FOX_KDIM), F32)],
        compiler_params=_cparams(("parallel", "arbitrary")),
        name="fox_attn",
    )(r, qa, kta, pm)


def _merge_kernel(oa_ref, ob_ref, wa_ref, wb_ref, ga_ref, gb_ref, y_ref):
    sa = _sigmoid(ga_ref[...].astype(F32))
    sb = _sigmoid(gb_ref[...].astype(F32))
    ya = jnp.dot(oa_ref[...], wa_ref[...], preferred_element_type=F32)
    yb = jnp.dot(ob_ref[...], wb_ref[...], preferred_element_type=F32)
    y_ref[...] = (sa * ya + sb * yb).astype(y_ref.dtype)


def _merge(oa, ob, wa, wb, pg):
    m, ka = oa.shape
    kb = ob.shape[1]
    n = wa.shape[1]
    tm = _tile(m, 1024, SUBLANES)
    tn = _tile(n, 1024)
    return pl.pallas_call(
        _merge_kernel,
        out_shape=jax.ShapeDtypeStruct((m, n), BF16),
        grid=(m // tm, n // tn),
        in_specs=[pl.BlockSpec((tm, ka), lambda i, j: (i, 0)),
                  pl.BlockSpec((tm, kb), lambda i, j: (i, 0)),
                  pl.BlockSpec((ka, tn), lambda i, j: (0, j)),
                  pl.BlockSpec((kb, tn), lambda i, j: (0, j)),
                  pl.BlockSpec((tm, tn), lambda i, j: (i, j)),
                  pl.BlockSpec((tm, tn), lambda i, j: (i, n // tn + j))],
        out_specs=pl.BlockSpec((tm, tn), lambda i, j: (i, j)),
        compiler_params=_cparams(("parallel", "arbitrary")),
        name="branch_merge",
    )(oa, ob, wa, wb, pg, pg)


def _resproj_kernel(a_ref, w_ref, x_ref, gt_ref, *refs):
    n_side = (len(refs) - 1) // 2
    o_ref = refs[n_side]
    y = jnp.dot(a_ref[...], w_ref[...], preferred_element_type=F32)
    o_ref[...] = x_ref[...] + gt_ref[...] * y
    _side_cast(refs[:n_side] + refs[n_side + 1:])


def _resproj(a, w, x, mod, gate_idx, tm_pref, tn_pref, name, side=()):
    m, k = a.shape
    n = w.shape[1]
    tm = _tile(m, tm_pref, SUBLANES)
    tn = _tile(n, tn_pref)
    side_in, side_out, side_shapes = _side_cast_specs(side, m // tm, n // tn)
    out = pl.pallas_call(
        _resproj_kernel,
        out_shape=[jax.ShapeDtypeStruct((m, n), F32)] + side_shapes,
        grid=(m // tm, n // tn),
        in_specs=[pl.BlockSpec((tm, k), lambda i, j: (i, 0)),
                  pl.BlockSpec((k, tn), lambda i, j: (0, j)),
                  pl.BlockSpec((tm, tn), lambda i, j: (i, j)),
                  pl.BlockSpec((1, tn), lambda i, j: (0, gate_idx * (n // tn) + j))] + side_in,
        out_specs=[pl.BlockSpec((tm, tn), lambda i, j: (i, j))] + side_out,
        compiler_params=_cparams(("arbitrary" if side else "parallel", "arbitrary")),
        name=name,
    )(a, w, x, mod, *_side_arrays(side))
    return out if side else out[0]


def _swiglu_kernel(h_ref, wg_ref, wu_ref, *refs):
    n_side = (len(refs) - 1) // 2
    o_ref = refs[n_side]
    h = h_ref[...]
    g = jnp.dot(h, wg_ref[...], preferred_element_type=F32)
    u = jnp.dot(h, wu_ref[...], preferred_element_type=F32)
    o_ref[...] = (g * _sigmoid(g) * u).astype(o_ref.dtype)
    _side_cast(refs[:n_side] + refs[n_side + 1:])


def _swiglu(h, w, side=()):
    m, k = h.shape
    dff = w.shape[1] // 2
    tm = _tile(m, 4096, SUBLANES)
    tn = _tile(dff, 512)
    nj = dff // tn
    side_in, side_out, side_shapes = _side_cast_specs(side, m // tm, nj)
    out = pl.pallas_call(
        _swiglu_kernel,
        out_shape=[jax.ShapeDtypeStruct((m, dff), BF16)] + side_shapes,
        grid=(m // tm, nj),
        in_specs=[pl.BlockSpec((tm, k), lambda i, j: (i, 0), pipeline_mode=pl.Buffered(1)),
                  pl.BlockSpec((k, tn), lambda i, j: (0, j)),
                  pl.BlockSpec((k, tn), lambda i, j: (0, nj + j))] + side_in,
        out_specs=[pl.BlockSpec((tm, tn), lambda i, j: (i, j))] + side_out,
        compiler_params=_cparams(("arbitrary" if side else "parallel", "arbitrary")),
        name="ffn_swiglu",
    )(h, w, w, *_side_arrays(side))
    return out if side else out[0]


def kernel(x, c, w_ada, b_ada, norm_mix_g, norm_ffn_g, w_in, fox_f_bias, hgrn_lb_table, hgrn_onorm_g,
           fox_q_norm_g, fox_k_norm_g, w_branch_a, w_branch_b, w_out, w_ffn_in, w_ffn_out):
    batch, seq, d = x.shape
    assert batch == 1 and w_ada.shape[0] == 1, "single sequence, single layer"
    hw = hgrn_onorm_g.shape[1] * hgrn_onorm_g.shape[2]
    fw = fox_q_norm_g.shape[1] * fox_q_norm_g.shape[2]
    nfh = fox_q_norm_g.shape[1]
    assert hgrn_onorm_g.shape[2] == HEAD_DIM and fox_q_norm_g.shape[2] == HEAD_DIM
    assert nfh <= LANES and hw == fw and seq % (2 * FOX_TK) == 0
    x2 = x.reshape(seq, d)

    f_col = 4 * hw + 3 * fw
    wt = jnp.swapaxes(w_in, 1, 2)[0]
    fbias = jnp.pad(fox_f_bias[0], (0, LANES - nfh)).reshape(1, LANES)

    c_col, w_ada2, b_ada2 = c.reshape(d, 1), w_ada[0], b_ada.reshape(1, b_ada.shape[-1])
    mod_a = _ada_mod(c_col, w_ada2, b_ada2, 2 * d)

    n_main = 2 * hw + 3 * fw
    h, ff, w_qf = _normmod_proj(x2, norm_mix_g[0], mod_a, 0, 1, wt, f_col, LANES,
                                side=((wt, 0, 2 * hw),))
    pq, w_main = _matmul(h, w_qf, 0, hw, BF16, "in_proj_q", tn_pref=512,
                         side=((wt, 2 * hw, n_main),))
    hf, w_gates = _matmul(h, w_qf, hw, hw, F32, "in_proj_f", tn_pref=512,
                          side=((wt, f_col, nfh + 2 * d),))
    pm, mod_b, w_out_bf, w_a_bf, w_b_bf = _matmul(
        h, w_main, 0, n_main, BF16, "in_proj_main", ada=(c_col, w_ada2, b_ada2, 2 * d, 4 * d),
        side=(w_out[0], w_branch_a[0], w_branch_b[0]))
    pg, w_ffn_in_bf = _matmul(h, w_gates, nfh, 2 * d, BF16, "in_proj_gates",
                              side=(w_ffn_in[0],))

    o_a = _hgrn(pq, pm, hf, hgrn_lb_table, hgrn_onorm_g[0], hw, 0, hw)

    qa, kta, r = _foxprep(pm, ff, fbias, fox_q_norm_g[0], fox_k_norm_g[0], fw, 2 * hw, 2 * hw + fw, FOX_TK)
    o_b = _fox(qa, kta, r, pm, 2 * hw + 2 * fw, FOX_TK)

    y = _merge(o_a, o_b, w_a_bf, w_b_bf, pg)
    x1, w_ffn_out_bf = _resproj(y, w_out_bf, x2, mod_b, 0, 1024, 512, "out_proj",
                                side=(w_ffn_out[0],))

    h2 = _normmod(x1, norm_ffn_g[0], mod_b, 1, 2)
    act = _swiglu(h2, w_ffn_in_bf)
    x3 = _resproj(act, w_ffn_out_bf, x1, mod_b, 3, 512, 512, "ffn_out")
    return x3.reshape(batch, seq, d)
```
